```python
import math
import jax
import jax.numpy as jnp
from jax import lax
import numpy as np

D_MODEL = 1024
BATCH = 16
SEQ = 2048
DEPTH = 2

CTX_LEN = 256
GRID_W = 64
N_MIXERS = 2
N_SSD_LAYERS = (DEPTH + 1) // 2
N_MLP_LAYERS = DEPTH // 2
N_MOD = 6
EPS = 1e-6

SSD_INNER = 2 * D_MODEL
SSD_HEADDIM = 64
SSD_HEADS = SSD_INNER // SSD_HEADDIM
SSD_GROUPS = 4
SSD_HPG = SSD_HEADS // SSD_GROUPS
SSD_STATE = 128
SSD_CONV = 5
SSD_CHUNK = 128
SSD_BC = SSD_GROUPS * SSD_STATE
SSD_CONV_DIM = SSD_INNER + 2 * SSD_BC
SSD_PROJ = SSD_INNER + SSD_CONV_DIM + 2 * SSD_HEADS
DT_MIN = 1e-3
DT_MAX = 1e-1

MLP_INNER = 2 * D_MODEL
MLP_CHUNK = 128
MLP_GROUPS = 16
MLP_GROUP_DIM = MLP_INNER // MLP_GROUPS
MLP_ROW_GROUPS = MLP_GROUPS // 2

N_EXPERTS = 16
EXPERT_FF = 1536
CAPACITY_FACTOR = 2

kernel_name = 'hybrid_ssd_chunkmlp_ecmoe_dit'


def _rmsnorm(x, g):
    xf = x.astype(jnp.float32)
    xf = xf * lax.rsqrt(jnp.mean(xf * xf, axis=-1, keepdims=True) + EPS)
    return (xf * g.astype(jnp.float32)).astype(x.dtype)


def _layernorm(x, g):
    xf = x.astype(jnp.float32)
    xc = xf - jnp.mean(xf, axis=-1, keepdims=True)
    xc = xc * lax.rsqrt(jnp.mean(xc * xc, axis=-1, keepdims=True) + EPS)
    return (xc * g.astype(jnp.float32)).astype(x.dtype)


def _modulate(x, g, shift, scale):
    return _rmsnorm(x, g) * (1 + scale) + shift


def _flip_if(t, rev):
    return jnp.flip(t, axis=1) if rev else t


def _dwconv_centred(u, w, b):
    pad = (SSD_CONV - 1) // 2
    y = lax.conv_general_dilated(u, w[:, None, :].astype(u.dtype), window_strides=(1,),
                                 padding=[(pad, pad)], dimension_numbers=('NWC', 'WIO', 'NWC'),
                                 feature_group_count=u.shape[-1])
    return y + b.astype(u.dtype)


def _ssd_project(h, in_w, conv_w, conv_b, dt_bias):
    b, L, _ = h.shape
    zxbcdt = h @ in_w
    z, xbc, dt = jnp.split(zxbcdt, [SSD_INNER, SSD_INNER + SSD_CONV_DIM], axis=-1)
    xbc = jax.nn.silu(_dwconv_centred(xbc, conv_w, conv_b))
    xs, bs, cs = jnp.split(xbc, [SSD_INNER, SSD_INNER + SSD_BC], axis=-1)
    xs = xs.reshape(b, L, SSD_HEADS, SSD_HEADDIM)
    bs = bs.reshape(b, L, SSD_GROUPS, SSD_STATE)
    cs = cs.reshape(b, L, SSD_GROUPS, SSD_STATE)
    dt = jax.nn.softplus(dt.reshape(b, L, 2, SSD_HEADS).astype(jnp.float32) + dt_bias.astype(jnp.float32))
    return z, xs, bs, cs, dt


def _ssd_stream(xs, dts, bs, cs, a_neg, init_state, with_output):
    f32 = jnp.float32
    b, L = xs.shape[:2]
    nc = L // SSD_CHUNK
    dts = dts.astype(f32)
    xdt = (xs.astype(f32) * dts[..., None]).reshape(b, nc, SSD_CHUNK, SSD_GROUPS, SSD_HPG, SSD_HEADDIM)
    a = (dts * a_neg.astype(f32)).reshape(b, nc, SSD_CHUNK, SSD_GROUPS, SSD_HPG)
    bc = bs.astype(f32).reshape(b, nc, SSD_CHUNK, SSD_GROUPS, SSD_STATE)
    a_cs = jnp.cumsum(a, axis=2)
    decay_to_end = jnp.exp(a_cs[:, :, -1:] - a_cs)
    chunk_states = jnp.einsum('bclgn,bclgj,bclgjp->bcgjpn', bc, decay_to_end, xdt)
    chunk_decay = jnp.exp(a_cs[:, :, -1])

    def step(state, inp):
        s_c, d_c = inp
        return d_c[..., None, None] * state + s_c, state

    final, prev = lax.scan(step, init_state,
                           (jnp.moveaxis(chunk_states, 1, 0), jnp.moveaxis(chunk_decay, 1, 0)))
    if not with_output:
        return None, final
    prev = jnp.moveaxis(prev, 0, 1)
    cc = cs.astype(f32).reshape(b, nc, SSD_CHUNK, SSD_GROUPS, SSD_STATE)
    a_h = jnp.moveaxis(a_cs, 2, -1)
    lower = jnp.tril(jnp.ones((SSD_CHUNK, SSD_CHUNK), dtype=bool))
    decay = jnp.exp(jnp.where(lower, a_h[..., :, None] - a_h[..., None, :], -jnp.inf))
    cb = jnp.einsum('bclgn,bcsgn->bcgls', cc, bc)
    y_diag = jnp.einsum('bcgls,bcgjls,bcsgjp->bclgjp', cb, decay, xdt)
    y_off = jnp.einsum('bclgn,bcgjpn,bclgj->bclgjp', cc, prev, jnp.exp(a_cs))
    return (y_diag + y_off).reshape(b, L, SSD_HEADS, SSD_HEADDIM), final


def _ssd_finish(y, xs, z, d_skip, norm_g, out_w):
    b, L = xs.shape[:2]
    y = y + d_skip.astype(jnp.float32)[:, None] * xs.astype(jnp.float32)
    y = y.reshape(b, L, SSD_INNER).astype(z.dtype) * jax.nn.silu(z)
    y = _rmsnorm(y.reshape(b, L, SSD_GROUPS, SSD_INNER // SSD_GROUPS),
                 norm_g.reshape(SSD_GROUPS, -1)).reshape(b, L, SSD_INNER)
    return y @ out_w


def _ssd_mixer(h_ctx, h_lat, in_w, conv_w, conv_b, dt_bias, a_log, d_skip, norm_g, out_w, ctx_out):
    a_neg = -jnp.exp(a_log.astype(jnp.float32))
    zc, xc, bc, cc, dtc = _ssd_project(h_ctx, in_w, conv_w, conv_b, dt_bias)
    zl, xl, bl, cl, dtl = _ssd_project(h_lat, in_w, conv_w, conv_b, dt_bias)
    init = jnp.zeros((h_lat.shape[0], SSD_GROUPS, SSD_HPG, SSD_HEADDIM, SSD_STATE), jnp.float32)
    y_lat, y_ctx = None, None
    for d in range(2):
        rev = d == 1
        yc, st = _ssd_stream(_flip_if(xc, rev), _flip_if(dtc[:, :, d], rev), _flip_if(bc, rev),
                             _flip_if(cc, rev), a_neg[d], init, ctx_out)
        yl, _ = _ssd_stream(_flip_if(xl, rev), _flip_if(dtl[:, :, d], rev), _flip_if(bl, rev),
                            _flip_if(cl, rev), a_neg[d], st, True)
        yl = _flip_if(yl, rev)
        y_lat = yl if y_lat is None else y_lat + yl
        if ctx_out:
            yc = _flip_if(yc, rev)
            y_ctx = yc if y_ctx is None else y_ctx + yc
    out_lat = _ssd_finish(y_lat, xl, zl, d_skip, norm_g, out_w)
    out_ctx = _ssd_finish(y_ctx, xc, zc, d_skip, norm_g, out_w) if ctx_out else None
    return out_ctx, out_lat


def _chunk_mix(v, ws, bs):
    b, L, g, d = v.shape
    vc = v.reshape(b, L // MLP_CHUNK, MLP_CHUNK, g, d)
    s = jnp.einsum('gts,bcsgd->bctgd', ws.astype(v.dtype), vc) + bs.T.astype(v.dtype)[:, :, None]
    return s.reshape(b, L, g, d)


def _chunk_mlp_mixer(h, in_w, v_g, ws, bs, out_w, rows):
    b, L, _ = h.shape
    u, v = jnp.split(jax.nn.gelu(h @ in_w), 2, axis=-1)
    v = _layernorm(v, v_g).reshape(b, L, MLP_GROUPS, MLP_GROUP_DIM)
    if rows is None:
        s = _chunk_mix(v, ws, bs)
    else:
        hr = MLP_ROW_GROUPS
        hc = MLP_GROUPS - hr
        s_row = _chunk_mix(v[:, :, :hr], ws[:hr], bs[:hr])
        v_col = v[:, :, hr:].reshape(b, rows, GRID_W, hc, MLP_GROUP_DIM).swapaxes(1, 2)
        s_col = _chunk_mix(v_col.reshape(b, L, hc, MLP_GROUP_DIM), ws[hr:], bs[hr:])
        s_col = s_col.reshape(b, GRID_W, rows, hc, MLP_GROUP_DIM).swapaxes(1, 2).reshape(b, L, hc, MLP_GROUP_DIM)
        s = jnp.concatenate([s_row, s_col], axis=2)
    return (u * s.reshape(b, L, MLP_INNER)) @ out_w


def _expert_choice_ffn(h, router_w, w_gate, w_up, w_down):
    b, T, _ = h.shape
    cap = (CAPACITY_FACTOR * T) // N_EXPERTS
    aff = jax.nn.softmax((h @ router_w).astype(jnp.float32), axis=-1)
    gates, idx = lax.top_k(jnp.swapaxes(aff, 1, 2), cap)
    bidx = jnp.arange(b)[:, None, None]
    xin = h[bidx, idx]
    hid = jax.nn.silu(jnp.einsum('becd,edf->becf', xin, w_gate)) * jnp.einsum('becd,edf->becf', xin, w_up)
    y = jnp.einsum('becf,efd->becd', hid, w_down) * gates.astype(h.dtype)[..., None]
    return jnp.zeros_like(h).at[bidx, idx].add(y)


def setup_inputs(seed: int = 0) -> dict:
    key = jax.random.key(seed)
    ks = jax.random.split(key, 26)
    f32 = jnp.float32

    def nrm(k, shape, scale):
        return jax.random.normal(k, shape, f32) * scale

    def gain(k, shape):
        return 1.0 + 0.05 * jax.random.normal(k, shape, f32)

    dt0 = jnp.exp(jax.random.uniform(ks[9], (N_SSD_LAYERS, 2, SSD_HEADS), f32,
                                     math.log(DT_MIN), math.log(DT_MAX)))
    return {
        'x': nrm(ks[0], (BATCH, SEQ, D_MODEL), 1.0),
        'c': nrm(ks[1], (BATCH, D_MODEL), 1.0),
        'ctx': nrm(ks[2], (BATCH, CTX_LEN, D_MODEL), 1.0),
        'c_ctx': nrm(ks[3], (D_MODEL,), 1.0),
        'mod_w': nrm(ks[4], (DEPTH, D_MODEL, N_MOD * D_MODEL), 0.5 * D_MODEL ** -0.5),
        'mod_b': nrm(ks[5], (DEPTH, N_MOD * D_MODEL), 0.02),
        'norm_g': gain(ks[6], (DEPTH, 4, D_MODEL)),
        'ssd_in_w': nrm(ks[7], (N_SSD_LAYERS, D_MODEL, SSD_PROJ), D_MODEL ** -0.5),
        'ssd_conv_w': nrm(ks[8], (N_SSD_LAYERS, SSD_CONV, SSD_CONV_DIM), SSD_CONV ** -0.5),
        'ssd_conv_b': nrm(ks[10], (N_SSD_LAYERS, SSD_CONV_DIM), 0.02),
        'ssd_dt_bias': dt0 + jnp.log(-jnp.expm1(-dt0)),
        'ssd_a_log': jnp.log(jax.random.uniform(ks[11], (N_SSD_LAYERS, 2, SSD_HEADS), f32, 1.0, 16.0)),
        'ssd_d': gain(ks[12], (N_SSD_LAYERS, SSD_HEADS)),
        'ssd_norm_g': gain(ks[13], (N_SSD_LAYERS, SSD_INNER)),
        'ssd_out_w': nrm(ks[14], (N_SSD_LAYERS, SSD_INNER, D_MODEL), SSD_INNER ** -0.5),
        'mlp_in_w': nrm(ks[15], (N_MLP_LAYERS, D_MODEL, 2 * MLP_INNER), D_MODEL ** -0.5),
        'mlp_v_g': gain(ks[16], (N_MLP_LAYERS, MLP_INNER)),
        'mlp_ws': nrm(ks[17], (N_MLP_LAYERS, MLP_GROUPS, MLP_CHUNK, MLP_CHUNK), 0.5 * MLP_CHUNK ** -0.5),
        'mlp_bs': 1.0 + nrm(ks[18], (N_MLP_LAYERS, MLP_GROUPS, MLP_CHUNK), 0.1),
        'mlp_out_w': nrm(ks[19], (N_MLP_LAYERS, MLP_INNER, D_MODEL), MLP_INNER ** -0.5),
        'router_w': nrm(ks[20], (DEPTH, D_MODEL, N_EXPERTS), D_MODEL ** -0.5),
        'exp_w_gate': nrm(ks[21], (DEPTH, N_EXPERTS, D_MODEL, EXPERT_FF), D_MODEL ** -0.5),
        'exp_w_up': nrm(ks[22], (DEPTH, N_EXPERTS, D_MODEL, EXPERT_FF), D_MODEL ** -0.5),
        'exp_w_down': nrm(ks[23], (DEPTH, N_EXPERTS, EXPERT_FF, D_MODEL), EXPERT_FF ** -0.5),
    }


def reference(x, c, ctx, c_ctx, mod_w, mod_b, norm_g, ssd_in_w, ssd_conv_w, ssd_conv_b, ssd_dt_bias,
              ssd_a_log, ssd_d, ssd_norm_g, ssd_out_w, mlp_in_w, mlp_v_g, mlp_ws, mlp_bs, mlp_out_w,
              router_w, exp_w_gate, exp_w_up, exp_w_down):
    rows = x.shape[1] // GRID_W
    last_ssd = (DEPTH - 1) - ((DEPTH - 1) % N_MIXERS)
    for i in range(DEPTH):
        ctx_live = i <= last_ssd
        ctx_next = i < last_ssd
        g = norm_g[i]
        m_lat = jnp.split((jax.nn.silu(c) @ mod_w[i] + mod_b[i])[:, None, :], N_MOD, axis=-1)
        h_lat = _modulate(x, g[0], m_lat[0], m_lat[1])
        if ctx_live:
            m_ctx = jnp.split((jax.nn.silu(c_ctx) @ mod_w[i] + mod_b[i])[None, None, :], N_MOD, axis=-1)
            h_ctx = _modulate(ctx, g[0], m_ctx[0], m_ctx[1])
        j = i // N_MIXERS
        if i % N_MIXERS == 0:
            y_ctx, y_lat = _ssd_mixer(h_ctx, h_lat, ssd_in_w[j], ssd_conv_w[j], ssd_conv_b[j], ssd_dt_bias[j],
                                      ssd_a_log[j], ssd_d[j], ssd_norm_g[j], ssd_out_w[j], ctx_next)
        else:
            y_lat = _chunk_mlp_mixer(h_lat, mlp_in_w[j], mlp_v_g[j], mlp_ws[j], mlp_bs[j], mlp_out_w[j], rows)
            y_ctx = (_chunk_mlp_mixer(h_ctx, mlp_in_w[j], mlp_v_g[j], mlp_ws[j], mlp_bs[j], mlp_out_w[j], None)
                     if ctx_next else None)
        x = x + m_lat[2] * _rmsnorm(y_lat, g[1])
        h_lat = _modulate(x, g[2], m_lat[3], m_lat[4])
        x = x + m_lat[5] * _rmsnorm(_expert_choice_ffn(h_lat, router_w[i], exp_w_gate[i], exp_w_up[i],
                                                       exp_w_down[i]), g[3])
        if ctx_next:
            ctx = ctx + m_ctx[2] * _rmsnorm(y_ctx, g[1])
            h_ctx = _modulate(ctx, g[2], m_ctx[3], m_ctx[4])
            ctx = ctx + m_ctx[5] * _rmsnorm(_expert_choice_ffn(h_ctx, router_w[i], exp_w_gate[i], exp_w_up[i],
                                                               exp_w_down[i]), g[3])
    return x
```

```python
import functools

import jax
import jax.numpy as jnp
from jax import lax
from jax.experimental import pallas as pl
from jax.experimental.pallas import tpu as pltpu

F32 = jnp.float32
BF16 = jnp.bfloat16
HIGHEST = lax.Precision.HIGHEST
EPS = 1e-6

LANES = 128
SUBLANES = 8
MIB = 1024 * 1024

N_MOD = 6
GRID_W = 64
SSD_HEADDIM = 64
SSD_HEADS = 32
SSD_GROUPS = 4
SSD_HPG = SSD_HEADS // SSD_GROUPS
SSD_STATE = 128
SSD_CONV = 5
SSD_CHUNK = 128
MLP_CHUNK = 128
MLP_GROUPS = 16
MLP_ROW_GROUPS = 8
N_EXPERTS = 16
CAPACITY_FACTOR = 2

CONV_HALO = SUBLANES
TOKEN_TILE = 256


def _cparams(n_axes, vmem_mib):
    return pltpu.CompilerParams(dimension_semantics=("arbitrary",) * n_axes,
                                vmem_limit_bytes=vmem_mib * MIB)


def _sigmoid(x):
    return 1.0 / (1.0 + jnp.exp(-x))


def _rms(x, g):
    return x * lax.rsqrt(jnp.mean(x * x, axis=-1, keepdims=True) + EPS) * g


def _dot(a, b):
    return jnp.dot(a, b, preferred_element_type=F32)


def _dot_nt(a, b, precision=None):
    return lax.dot_general(a, b, (((1,), (1,)), ((), ())), preferred_element_type=F32, precision=precision)


def _dot_tn(a, b):
    return lax.dot_general(a, b, (((0,), (0,)), ((), ())), preferred_element_type=F32)


def _mod_kernel(c_ref, w_ref, b_ref, o_ref):
    c = c_ref[...]
    s = c * _sigmoid(c)
    o_ref[0] = jnp.dot(s, w_ref[0], preferred_element_type=F32, precision=HIGHEST) + b_ref[0]


def _modulation(crows, mod_w, mod_b):
    depth, d, n = mod_w.shape
    rows = crows.shape[0]
    tn = 1536
    return pl.pallas_call(
        _mod_kernel,
        grid=(depth, n // tn),
        in_specs=[pl.BlockSpec((rows, d), lambda i, j: (0, 0)),
                  pl.BlockSpec((1, d, tn), lambda i, j: (i, 0, j)),
                  pl.BlockSpec((1, 1, tn), lambda i, j: (i, 0, j))],
        out_specs=pl.BlockSpec((1, rows, tn), lambda i, j: (i, 0, j)),
        out_shape=jax.ShapeDtypeStruct((depth, rows, n), F32),
        compiler_params=_cparams(2, 40),
        name="modulation",
    )(crows, mod_w, mod_b.reshape(depth, 1, n))


def _ssd_in_kernel(ctx_ref, x_ref, xp_ref, xn_ref, mod_ref, g_ref, w_ref, cw_ref, cb_ref, dtb_ref,
                   z_ref, xs_ref, bc_ref, dt_ref, cbuf_ref, *, n_tiles, inner, conv_dim):
    j = pl.program_id(1)
    tm = x_ref.shape[1]
    halo = xp_ref.shape[1]
    xc = jnp.where(j == 0, ctx_ref[0], x_ref[0])
    xa = jnp.concatenate([xp_ref[0], xc, xn_ref[0]], axis=0)
    h = _rms(xa, g_ref[0:1, :]) * (1.0 + mod_ref[0, 0, 1:2, :]) + mod_ref[0, 0, 0:1, :]
    r = _dot(h.astype(BF16), w_ref[...])
    z_ref[0] = r[halo:halo + tm, :inner].astype(BF16)
    row = lax.broadcasted_iota(jnp.int32, (tm + 2 * halo, 1), 0)
    drop_top = jnp.logical_and(row < halo, j <= 1)
    drop_bot = jnp.logical_and(row >= halo + tm, jnp.logical_or(j == 0, j == n_tiles - 1))
    keep = jnp.logical_not(jnp.logical_or(drop_top, drop_bot))
    cbuf_ref[...] = jnp.where(keep, r[:, inner:inner + conv_dim], 0.0)
    pad = (SSD_CONV - 1) // 2
    acc = cb_ref[...] + cw_ref[0:1, :] * cbuf_ref[pl.ds(halo - pad, tm), :]
    for k in range(1, SSD_CONV):
        acc = acc + cw_ref[k:k + 1, :] * cbuf_ref[pl.ds(halo - pad + k, tm), :]
    act = acc * _sigmoid(acc)
    xs_ref[0] = act[:, :inner].astype(BF16)
    bc_ref[0] = act[:, inner:].astype(BF16)
    dtr = r[halo:halo + tm, inner + conv_dim:] + dtb_ref[...]
    dt_ref[0] = jnp.maximum(dtr, 0.0) + jnp.log1p(jnp.exp(-jnp.abs(dtr)))


def _ssd_in_proj(ctx, x, mods01, g, w, conv_w, conv_b, dt_bias):
    b, l, d = x.shape
    lc = ctx.shape[1]
    tm = TOKEN_TILE
    assert lc == tm and l % tm == 0
    n_tiles = 1 + l // tm
    inner = SSD_HEADS * SSD_HEADDIM
    conv_dim = conv_w.shape[1]
    bc_dim = conv_dim - inner
    halo = CONV_HALO
    blocks_per_tile = tm // halo
    last_halo_block = l // halo - 1
    lt = lc + l
    kern = functools.partial(_ssd_in_kernel, n_tiles=n_tiles, inner=inner, conv_dim=conv_dim)
    return pl.pallas_call(
        kern,
        grid=(b, n_tiles),
        in_specs=[
            pl.BlockSpec((1, lc, d), lambda i, j: (i, 0, 0)),
            pl.BlockSpec((1, tm, d), lambda i, j: (i, jnp.maximum(j - 1, 0), 0)),
            pl.BlockSpec((1, halo, d), lambda i, j: (i, jnp.maximum((j - 1) * blocks_per_tile - 1, 0), 0)),
            pl.BlockSpec((1, halo, d), lambda i, j: (i, jnp.minimum(j * blocks_per_tile, last_halo_block), 0)),
            pl.BlockSpec((1, 1, SUBLANES, d), lambda i, j: (i, jnp.minimum(j, 1), 0, 0)),
            pl.BlockSpec((SUBLANES, d), lambda i, j: (0, 0)),
            pl.BlockSpec(w.shape, lambda i, j: (0, 0), pipeline_mode=pl.Buffered(1)),
            pl.BlockSpec((SUBLANES, conv_dim), lambda i, j: (0, 0)),
            pl.BlockSpec((1, conv_dim), lambda i, j: (0, 0)),
            pl.BlockSpec((1, LANES), lambda i, j: (0, 0)),
        ],
        out_specs=[
            pl.BlockSpec((1, tm, inner), lambda i, j: (i, j, 0)),
            pl.BlockSpec((1, tm, inner), lambda i, j: (i, j, 0)),
            pl.BlockSpec((1, tm, bc_dim), lambda i, j: (i, j, 0)),
            pl.BlockSpec((1, tm, LANES), lambda i, j: (i, j, 0)),
        ],
        out_shape=[
            jax.ShapeDtypeStruct((b, lt, inner), BF16),
            jax.ShapeDtypeStruct((b, lt, inner), BF16),
            jax.ShapeDtypeStruct((b, lt, bc_dim), BF16),
            jax.ShapeDtypeStruct((b, lt, LANES), F32),
        ],
        scratch_shapes=[pltpu.VMEM((tm + 2 * halo, conv_dim), F32)],
        compiler_params=_cparams(2, 56),
        name="ssd_in_proj",
    )(ctx, x, x, x, mods01, g, w, conv_w, conv_b, dt_bias)


def _expand_exact(row, e):
    r8 = jnp.broadcast_to(row, (SUBLANES, row.shape[1]))
    hi = r8.astype(BF16)
    r1 = r8 - hi.astype(F32)
    mid = r1.astype(BF16)
    lo = (r1 - mid.astype(F32)).astype(BF16)
    return (_dot(hi, e) + _dot(mid, e) + _dot(lo, e))[0:1, :]


def _ssd_step(rev, s, xs_ref, bc_ref, dt_ref, alog_ref, e_ref, dskip_ref, y_ref, st_ref, ybuf_ref, n_ctx_chunks,
              n_chunks):
    c = SSD_CHUNK
    if rev:
        ci = jnp.where(s < n_ctx_chunks, n_ctx_chunks - 1 - s, n_chunks + 2 * n_ctx_chunks - 1 - s)
    else:
        ci = s
    is_lat = ci >= n_ctx_chunks
    lc = jnp.maximum(ci - n_ctx_chunks, 0)
    dcol = SSD_HEADS * int(rev)

    bc = bc_ref[0]
    dt = dt_ref[0]
    a = dt * (-jnp.exp(alog_ref[int(rev):int(rev) + 1, :]))
    li = lax.broadcasted_iota(jnp.int32, (c, c), 0)
    si = lax.broadcasted_iota(jnp.int32, (c, c), 1)
    causal = (si >= li) if rev else (si <= li)
    tri = jnp.where(causal, 1.0, 0.0).astype(F32)
    cs = jnp.dot(tri, a, preferred_element_type=F32, precision=HIGHEST)
    tot = cs[0:1, :] if rev else cs[c - 1:c, :]
    e = e_ref[int(rev)]
    w = jnp.exp(tot - cs) * dt
    wx = _dot(w.astype(BF16), e)
    xw = (xs_ref[0].astype(F32) * wx).astype(BF16)
    gw = SSD_HPG * SSD_HEADDIM
    bc_off = SSD_GROUPS * SSD_STATE

    @pl.when(is_lat)
    def _():
        cs_t = cs.T
        dt_t = dt.T
        px = _dot(jnp.exp(cs).astype(BF16), e)
        lane = lax.broadcasted_iota(jnp.int32, (c, LANES), 1)
        for g in range(SSD_GROUPS):
            bg = bc[:, g * SSD_STATE:(g + 1) * SSD_STATE]
            cg = bc[:, bc_off + g * SSD_STATE:bc_off + (g + 1) * SSD_STATE]
            y_off = _dot(cg, st_ref[:, g * gw:(g + 1) * gw].astype(BF16)) * px[:, g * gw:(g + 1) * gw]
            cb = _dot_nt(cg, bg)
            for hp in range(SSD_HPG // 2):
                ms = []
                for hh in range(2):
                    hd = dcol + g * SSD_HPG + 2 * hp + hh
                    diff = cs[:, hd:hd + 1] - cs_t[hd:hd + 1, :]
                    dec = jnp.exp(jnp.where(causal, diff, -jnp.inf))
                    ms.append(cb * dec * dt_t[hd:hd + 1, :])
                mp = jnp.concatenate(ms, axis=1).astype(BF16)
                col0 = (g * SSD_HPG + 2 * hp) * SSD_HEADDIM
                xp = xs_ref[0, :, col0:col0 + LANES].astype(F32)
                rhs = jnp.concatenate([jnp.where(lane < SSD_HEADDIM, xp, 0.0),
                                       jnp.where(lane >= SSD_HEADDIM, xp, 0.0)], axis=0).astype(BF16)
                y_pair = _dot(mp, rhs) + y_off[:, 2 * hp * SSD_HEADDIM:2 * hp * SSD_HEADDIM + LANES]
                if rev:
                    tot_y = ybuf_ref[lc, :, col0:col0 + LANES] + y_pair + dskip_ref[:, col0:col0 + LANES] * xp
                    y_ref[0, :, col0:col0 + LANES] = tot_y.astype(BF16)
                else:
                    ybuf_ref[lc, :, col0:col0 + LANES] = y_pair

    decay = _expand_exact(jnp.exp(tot), e)
    for g in range(SSD_GROUPS):
        bg = bc[:, g * SSD_STATE:(g + 1) * SSD_STATE]
        upd = _dot_tn(bg, xw[:, g * gw:(g + 1) * gw])
        st_ref[:, g * gw:(g + 1) * gw] = st_ref[:, g * gw:(g + 1) * gw] * decay[:, g * gw:(g + 1) * gw] + upd


def _ssd_scan_kernel(xs_ref, bc_ref, dt_ref, alog_ref, e_ref, dskip_ref, y_ref, st_ref, ybuf_ref, *,
                     n_ctx_chunks, n_chunks):
    d = pl.program_id(1)
    s = pl.program_id(2)

    @pl.when(s == 0)
    def _():
        st_ref[...] = jnp.zeros_like(st_ref)

    for rev in (False, True):
        @pl.when(d == int(rev))
        def _(rev=rev):
            _ssd_step(rev, s, xs_ref, bc_ref, dt_ref, alog_ref, e_ref, dskip_ref, y_ref, st_ref, ybuf_ref,
                      n_ctx_chunks, n_chunks)


def _ssd_scan(xs_all, bc_all, dt_all, alog, expand, dskip, n_ctx_chunks, n_chunks):
    b, lt, inner = xs_all.shape
    c = SSD_CHUNK
    steps = n_ctx_chunks + n_chunks

    def chunk_idx(d, s):
        bwd = jnp.where(s < n_ctx_chunks, n_ctx_chunks - 1 - s, steps + n_ctx_chunks - 1 - s)
        return jnp.where(d == 0, s, bwd)

    def out_idx(d, s):
        return jnp.where(jnp.logical_and(d == 1, s >= n_ctx_chunks), steps - 1 - s, n_chunks - 1)

    kern = functools.partial(_ssd_scan_kernel, n_ctx_chunks=n_ctx_chunks, n_chunks=n_chunks)
    return pl.pallas_call(
        kern,
        grid=(b, 2, steps),
        in_specs=[
            pl.BlockSpec((1, c, inner), lambda i, d, s: (i, chunk_idx(d, s), 0)),
            pl.BlockSpec((1, c, bc_all.shape[2]), lambda i, d, s: (i, chunk_idx(d, s), 0)),
            pl.BlockSpec((1, c, LANES), lambda i, d, s: (i, chunk_idx(d, s), 0)),
            pl.BlockSpec((SUBLANES, LANES), lambda i, d, s: (0, 0)),
            pl.BlockSpec((2, LANES, inner), lambda i, d, s: (0, 0, 0)),
            pl.BlockSpec((1, inner), lambda i, d, s: (0, 0)),
        ],
        out_specs=pl.BlockSpec((1, c, inner), lambda i, d, s: (i, out_idx(d, s), 0)),
        out_shape=jax.ShapeDtypeStruct((b, n_chunks * c, inner), BF16),
        scratch_shapes=[pltpu.VMEM((SSD_STATE, inner), F32),
                        pltpu.VMEM((n_chunks, c, inner), F32)],
        compiler_params=_cparams(3, 48),
        name="ssd_scan",
    )(xs_all, bc_all, dt_all, alog, expand, dskip)


def _mixer_epilogue(o, x_ref, mod_ref, g_ref, rw_ref, xo_ref, h2_ref, aff_ref):
    xn = x_ref[0] + mod_ref[0, 2:3, :] * _rms(o, g_ref[1:2, :])
    xo_ref[0] = xn
    h2 = _rms(xn, g_ref[2:3, :]) * (1.0 + mod_ref[0, 4:5, :]) + mod_ref[0, 3:4, :]
    h2_ref[0] = h2.astype(BF16)
    lg = _dot_nt(rw_ref[...], h2, precision=HIGHEST)
    ex = jnp.exp(lg - jnp.max(lg, axis=0, keepdims=True))
    aff_ref[0] = ex / jnp.sum(ex, axis=0, keepdims=True)


def _ssd_out_kernel(y_ref, z_ref, x_ref, mod_ref, g_ref, ng_ref, w_ref, rw_ref, xo_ref, h2_ref, aff_ref):
    z = z_ref[0].astype(F32)
    yg = (y_ref[0].astype(F32) * (z * _sigmoid(z)))
    gdim = yg.shape[1] // SSD_GROUPS
    parts = [_rms(yg[:, g * gdim:(g + 1) * gdim], ng_ref[:, g * gdim:(g + 1) * gdim]).astype(BF16)
             for g in range(SSD_GROUPS)]
    o = _dot(jnp.concatenate(parts, axis=1), w_ref[...])
    _mixer_epilogue(o, x_ref, mod_ref, g_ref, rw_ref, xo_ref, h2_ref, aff_ref)


def _epilogue_outs(b, l, d, tm):
    out_specs = [pl.BlockSpec((1, tm, d), lambda i, j: (i, j, 0)),
                 pl.BlockSpec((1, tm, d), lambda i, j: (i, j, 0)),
                 pl.BlockSpec((1, N_EXPERTS, tm), lambda i, j: (i, 0, j))]
    out_shape = [jax.ShapeDtypeStruct((b, l, d), F32),
                 jax.ShapeDtypeStruct((b, l, d), BF16),
                 jax.ShapeDtypeStruct((b, N_EXPERTS, l), F32)]
    return out_specs, out_shape


def _ssd_out(y, z_all, x, mods, g, ng, w, rw_t, ctx_tiles):
    b, l, d = x.shape
    inner = y.shape[2]
    tm = TOKEN_TILE
    z_off = ctx_tiles
    out_specs, out_shape = _epilogue_outs(b, l, d, tm)
    return pl.pallas_call(
        _ssd_out_kernel,
        grid=(b, l // tm),
        in_specs=[
            pl.BlockSpec((1, tm, inner), lambda i, j: (i, j, 0)),
            pl.BlockSpec((1, tm, inner), lambda i, j: (i, j + z_off, 0)),
            pl.BlockSpec((1, tm, d), lambda i, j: (i, j, 0)),
            pl.BlockSpec((1, SUBLANES, d), lambda i, j: (i, 0, 0)),
            pl.BlockSpec((SUBLANES, d), lambda i, j: (0, 0)),
            pl.BlockSpec((1, inner), lambda i, j: (0, 0)),
            pl.BlockSpec(w.shape, lambda i, j: (0, 0)),
            pl.BlockSpec(rw_t.shape, lambda i, j: (0, 0)),
        ],
        out_specs=out_specs,
        out_shape=out_shape,
        compiler_params=_cparams(2, 48),
        name="ssd_out",
    )(y, z_all, x, mods, g, ng, w, rw_t)


def _epilogue_kernel(o_ref, x_ref, mod_ref, g_ref, rw_ref, xo_ref, h2_ref, aff_ref):
    _mixer_epilogue(o_ref[0], x_ref, mod_ref, g_ref, rw_ref, xo_ref, h2_ref, aff_ref)


def _epilogue(o, x, mods, g, rw_t):
    b, l, d = x.shape
    tm = 2 * TOKEN_TILE
    out_specs, out_shape = _epilogue_outs(b, l, d, tm)
    return pl.pallas_call(
        _epilogue_kernel,
        grid=(b, l // tm),
        in_specs=[
            pl.BlockSpec((1, tm, d), lambda i, j: (i, j, 0)),
            pl.BlockSpec((1, tm, d), lambda i, j: (i, j, 0)),
            pl.BlockSpec((1, SUBLANES, d), lambda i, j: (i, 0, 0)),
            pl.BlockSpec((SUBLANES, d), lambda i, j: (0, 0)),
            pl.BlockSpec(rw_t.shape, lambda i, j: (0, 0)),
        ],
        out_specs=out_specs,
        out_shape=out_shape,
        compiler_params=_cparams(2, 40),
        name="mixer_epilogue",
    )(o, x, mods, g, rw_t)


def _lane_prefix_exclusive(m01, upper):
    e, t = m01.shape
    carry = jnp.zeros((e, 1), F32)
    outs = []
    for k in range(t // LANES):
        tile = m01[:, k * LANES:(k + 1) * LANES]
        incl = _dot(tile.astype(BF16), upper)
        outs.append(incl - tile + carry)
        carry = carry + incl[:, LANES - 1:LANES]
    return jnp.concatenate(outs, axis=1)


def _route(aff, cap):
    e, t = aff.shape
    key = pltpu.bitcast(aff, jnp.int32)

    def body(i, thr):
        cand = jnp.bitwise_or(thr, jnp.left_shift(jnp.int32(1), 30 - i))
        cnt = jnp.sum(jnp.where(key >= cand, 1.0, 0.0), axis=1, keepdims=True)
        return jnp.where(cnt >= cap, cand, thr)

    thr = lax.fori_loop(0, 31, body, jnp.zeros((e, 1), jnp.int32))
    gt = jnp.where(key > thr, 1.0, 0.0)
    eq = jnp.where(key == thr, 1.0, 0.0)
    need = cap - jnp.sum(gt, axis=1, keepdims=True)
    r = lax.broadcasted_iota(jnp.int32, (LANES, LANES), 0)
    cidx = lax.broadcasted_iota(jnp.int32, (LANES, LANES), 1)
    upper = jnp.where(r <= cidx, 1.0, 0.0).astype(BF16)
    eq_rank = _lane_prefix_exclusive(eq, upper)
    sel = gt + eq * jnp.where(eq_rank < need, 1.0, 0.0)
    pos = _lane_prefix_exclusive(sel, upper)
    return jnp.where(sel > 0.5, pos, -1.0).astype(jnp.int32)


def _moe_kernel(h_ref, aff_ref, wg_ref, wu_ref, wd_ref, o_ref, slot_ref, *, cap):
    ex = pl.program_id(1)

    @pl.when(ex == 0)
    def _():
        slot_ref[...] = _route(aff_ref[0], cap)
        o_ref[...] = jnp.zeros_like(o_ref)

    t = h_ref.shape[1]
    srow = slot_ref[pl.ds(ex, 1), :]
    arow = aff_ref[0, pl.ds(ex, 1), :]
    pick = lax.broadcasted_iota(jnp.int32, (cap, t), 0) == srow
    p = jnp.where(pick, 1.0, 0.0).astype(BF16)
    gate = jnp.sum(jnp.where(pick, arow, 0.0), axis=1, keepdims=True)
    xin = _dot(p, h_ref[0]).astype(BF16)
    hid = _dot(xin, wg_ref[0])
    hid = (hid * _sigmoid(hid)) * _dot(xin, wu_ref[0])
    y = (_dot(hid.astype(BF16), wd_ref[0]) * gate).astype(BF16)
    cw = 2 * LANES
    for cb in range(y.shape[1] // cw):
        o_ref[0, :, cb * cw:(cb + 1) * cw] += _dot_tn(p, y[:, cb * cw:(cb + 1) * cw])


def _moe(h2, aff_t, wg, wu, wd):
    b, t, d = h2.shape
    ne, _, ff = wg.shape
    cap = (CAPACITY_FACTOR * t) // ne
    return pl.pallas_call(
        functools.partial(_moe_kernel, cap=cap),
        grid=(b, ne),
        in_specs=[
            pl.BlockSpec((1, t, d), lambda i, e: (i, 0, 0), pipeline_mode=pl.Buffered(1)),
            pl.BlockSpec((1, ne, t), lambda i, e: (i, 0, 0)),
            pl.BlockSpec((1, d, ff), lambda i, e: (e, 0, 0)),
            pl.BlockSpec((1, d, ff), lambda i, e: (e, 0, 0)),
            pl.BlockSpec((1, ff, d), lambda i, e: (e, 0, 0)),
        ],
        out_specs=pl.BlockSpec((1, t, d), lambda i, e: (i, 0, 0)),
        out_shape=jax.ShapeDtypeStruct((b, t, d), F32),
        scratch_shapes=[pltpu.VMEM((ne, t), jnp.int32)],
        compiler_params=_cparams(2, 60),
        name="moe",
    )(h2, aff_t, wg, wu, wd)


def _moe_residual(x_ref, moe_ref, mod_ref, g_ref):
    return x_ref[0] + mod_ref[0, 5:6, :] * _rms(moe_ref[0], g_ref[3:4, :])


def _mlp_in_kernel(xa_ref, moe_ref, modp_ref, gp_ref, mod_ref, g_ref, w_ref, vg_ref, x1_ref, u_ref, v_ref):
    x1 = _moe_residual(xa_ref, moe_ref, modp_ref, gp_ref)
    x1_ref[0] = x1
    h = _rms(x1, g_ref[0:1, :]) * (1.0 + mod_ref[0, 1:2, :]) + mod_ref[0, 0:1, :]
    r = _dot(h.astype(BF16), w_ref[...])
    ge = 0.5 * r * (1.0 + jnp.tanh(0.7978845608028654 * (r + 0.044715 * (r * r * r))))
    half = ge.shape[1] // 2
    u_ref[0] = ge[:, :half].astype(BF16)
    v = ge[:, half:]
    vc = v - jnp.mean(v, axis=-1, keepdims=True)
    vn = vc * lax.rsqrt(jnp.mean(vc * vc, axis=-1, keepdims=True) + EPS) * vg_ref[...]
    v_ref[0] = vn.astype(BF16)


def _mlp_in(xa, moe, mods_prev, g_prev, mods, g, w, vg):
    b, l, d = xa.shape
    tm = TOKEN_TILE
    half = w.shape[1] // 2
    tok = lambda i, j: (i, j, 0)
    return pl.pallas_call(
        _mlp_in_kernel,
        grid=(b, l // tm),
        in_specs=[
            pl.BlockSpec((1, tm, d), tok),
            pl.BlockSpec((1, tm, d), tok),
            pl.BlockSpec((1, SUBLANES, d), lambda i, j: (i, 0, 0)),
            pl.BlockSpec((SUBLANES, d), lambda i, j: (0, 0)),
            pl.BlockSpec((1, SUBLANES, d), lambda i, j: (i, 0, 0)),
            pl.BlockSpec((SUBLANES, d), lambda i, j: (0, 0)),
            pl.BlockSpec(w.shape, lambda i, j: (0, 0), pipeline_mode=pl.Buffered(1)),
            pl.BlockSpec((1, half), lambda i, j: (0, 0)),
        ],
        out_specs=[pl.BlockSpec((1, tm, d), tok),
                   pl.BlockSpec((1, tm, half), tok),
                   pl.BlockSpec((1, tm, half), tok)],
        out_shape=[jax.ShapeDtypeStruct((b, l, d), F32),
                   jax.ShapeDtypeStruct((b, l, half), BF16),
                   jax.ShapeDtypeStruct((b, l, half), BF16)],
        compiler_params=_cparams(2, 48),
        name="mlp_in_proj",
    )(xa, moe, mods_prev, g_prev, mods, g, w, vg)


def _mlp_mix_kernel(u_ref, v_ref, ws_ref, bs_ref, w_ref, o_ref, vf_ref, sf_ref, us_ref, *, rows):
    gp = pl.program_id(1)
    t = u_ref.shape[1]
    n_chunks = t // MLP_CHUNK
    per_step = ws_ref.shape[0]
    cols_per_chunk = MLP_CHUNK // rows

    @pl.when(gp < MLP_ROW_GROUPS // per_step)
    def _():
        for i in range(per_step):
            cs = slice(i * LANES, (i + 1) * LANES)
            for c in range(n_chunks):
                rs = slice(c * MLP_CHUNK, (c + 1) * MLP_CHUNK)
                s = _dot(ws_ref[i], v_ref[0, rs, cs]) + bs_ref[i]
                us_ref[rs, cs] = (u_ref[0, rs, cs].astype(F32) * s).astype(BF16)

    @pl.when(gp >= MLP_ROW_GROUPS // per_step)
    def _():
        for i in range(per_step):
            cs = slice(i * LANES, (i + 1) * LANES)
            vf_ref[...] = v_ref[0, :, cs].astype(F32)
            for k in range(n_chunks):
                xk = jnp.concatenate(
                    [vf_ref[pl.ds(k * cols_per_chunk + wl, rows, stride=GRID_W), :] for wl in range(cols_per_chunk)],
                    axis=0)
                s = _dot(ws_ref[i], xk.astype(BF16)) + bs_ref[i]
                for wl in range(cols_per_chunk):
                    sf_ref[pl.ds(k * cols_per_chunk + wl, rows, stride=GRID_W), :] = s[wl * rows:(wl + 1) * rows, :]
            us_ref[:, cs] = (u_ref[0, :, cs].astype(F32) * sf_ref[...]).astype(BF16)

    part = _dot(us_ref[...], w_ref[...])

    @pl.when(gp == 0)
    def _():
        o_ref[0] = part

    @pl.when(gp > 0)
    def _():
        o_ref[0] += part


def _mlp_mix(u, v, ws, bsb, w):
    b, l, inner = u.shape
    d = w.shape[1]
    per_step = 2
    steps = MLP_GROUPS // per_step
    kc = per_step * LANES
    rows = l // GRID_W
    return pl.pallas_call(
        functools.partial(_mlp_mix_kernel, rows=rows),
        grid=(b, steps),
        in_specs=[
            pl.BlockSpec((1, l, kc), lambda i, g: (i, 0, g)),
            pl.BlockSpec((1, l, kc), lambda i, g: (i, 0, g)),
            pl.BlockSpec((per_step, MLP_CHUNK, MLP_CHUNK), lambda i, g: (g, 0, 0)),
            pl.BlockSpec((per_step, MLP_CHUNK, LANES), lambda i, g: (g, 0, 0)),
            pl.BlockSpec((kc, d), lambda i, g: (g, 0)),
        ],
        out_specs=pl.BlockSpec((1, l, d), lambda i, g: (i, 0, 0)),
        out_shape=jax.ShapeDtypeStruct((b, l, d), F32),
        scratch_shapes=[pltpu.VMEM((l, LANES), F32), pltpu.VMEM((l, LANES), F32), pltpu.VMEM((l, kc), BF16)],
        compiler_params=_cparams(2, 48),
        name="mlp_mix",
    )(u, v, ws, bsb, w)


def _final_kernel(x_ref, moe_ref, mod_ref, g_ref, o_ref):
    o_ref[0] = _moe_residual(x_ref, moe_ref, mod_ref, g_ref)


def _final(x, moe, mods, g):
    b, l, d = x.shape
    tm = 2 * TOKEN_TILE
    tok = lambda i, j: (i, j, 0)
    return pl.pallas_call(
        _final_kernel,
        grid=(b, l // tm),
        in_specs=[pl.BlockSpec((1, tm, d), tok), pl.BlockSpec((1, tm, d), tok),
                  pl.BlockSpec((1, SUBLANES, d), lambda i, j: (i, 0, 0)),
                  pl.BlockSpec((SUBLANES, d), lambda i, j: (0, 0))],
        out_specs=pl.BlockSpec((1, tm, d), tok),
        out_shape=jax.ShapeDtypeStruct((b, l, d), F32),
        compiler_params=_cparams(2, 32),
        name="moe_residual",
    )(x, moe, mods, g)


def _pad_rows(a, rows):
    return jnp.pad(a, ((0, rows - a.shape[0]),) + ((0, 0),) * (a.ndim - 1))


def kernel(x, c, ctx, c_ctx, mod_w, mod_b, norm_g, ssd_in_w, ssd_conv_w, ssd_conv_b, ssd_dt_bias, ssd_a_log,
           ssd_d, ssd_norm_g, ssd_out_w, mlp_in_w, mlp_v_g, mlp_ws, mlp_bs, mlp_out_w, router_w, exp_w_gate,
           exp_w_up, exp_w_down):
    b, l, d = x.shape
    inner = SSD_HEADS * SSD_HEADDIM
    n_chunks = l // SSD_CHUNK
    n_ctx_chunks = ctx.shape[1] // SSD_CHUNK

    crows = _pad_rows(jnp.concatenate([c, c_ctx[None, :]], axis=0), -(-(b + 1) // SUBLANES) * SUBLANES)
    mod = _modulation(crows, mod_w, mod_b)
    mods_lat = [jnp.pad(mod[i, :b].reshape(b, N_MOD, d), ((0, 0), (0, SUBLANES - N_MOD), (0, 0)))
                for i in range(2)]
    mods_ctx = jnp.pad(mod[0, b].reshape(1, N_MOD, d), ((0, 0), (0, SUBLANES - N_MOD), (0, 0)))
    mods01 = jnp.stack([jnp.broadcast_to(mods_ctx, (b, SUBLANES, d)), mods_lat[0]], axis=1)
    gains = [_pad_rows(norm_g[i], SUBLANES) for i in range(2)]
    rw_t = [router_w[i].T for i in range(2)]
    wg = exp_w_gate.astype(BF16)
    wu = exp_w_up.astype(BF16)
    wd = exp_w_down.astype(BF16)

    in_w = ssd_in_w[0]
    conv_dim = ssd_conv_w.shape[2]
    w_in = jnp.pad(in_w, ((0, 0), (0, LANES - 2 * SSD_HEADS))).astype(BF16)
    dt_bias = jnp.pad(ssd_dt_bias[0].reshape(1, 2 * SSD_HEADS), ((0, 0), (0, LANES - 2 * SSD_HEADS)))
    z_all, xs_all, bc_all, dt_all = _ssd_in_proj(
        ctx, x, mods01, gains[0], w_in, _pad_rows(ssd_conv_w[0], SUBLANES), ssd_conv_b[0].reshape(1, conv_dim),
        dt_bias)

    alog = _pad_rows(jnp.stack([jnp.pad(ssd_a_log[0, 0], (0, LANES - SSD_HEADS)),
                                jnp.pad(ssd_a_log[0, 1], (SSD_HEADS, LANES - 2 * SSD_HEADS))]), SUBLANES)
    head_of_col = jnp.arange(inner, dtype=jnp.int32) // SSD_HEADDIM
    rows128 = jnp.arange(LANES, dtype=jnp.int32)[:, None]
    expand = jnp.stack([(rows128 == head_of_col[None, :] + SSD_HEADS * dd) for dd in range(2)]).astype(BF16)
    dskip = jnp.repeat(ssd_d[0], SSD_HEADDIM).reshape(1, inner)
    y = _ssd_scan(xs_all, bc_all, dt_all, alog, expand, dskip, n_ctx_chunks, n_chunks)

    xa, h2, aff_t = _ssd_out(y, z_all, x, mods_lat[0], gains[0], ssd_norm_g[0].reshape(1, inner),
                             ssd_out_w[0].astype(BF16), rw_t[0], ctx.shape[1] // TOKEN_TILE)
    moe0 = _moe(h2, aff_t, wg[0], wu[0], wd[0])

    x1, u, v = _mlp_in(xa, moe0, mods_lat[0], gains[0], mods_lat[1], gains[1], mlp_in_w[0].astype(BF16),
                       mlp_v_g[0].reshape(1, -1))
    bsb = jnp.broadcast_to(mlp_bs[0][:, :, None], (MLP_GROUPS, MLP_CHUNK, LANES))
    o1 = _mlp_mix(u, v, mlp_ws[0].astype(BF16), bsb, mlp_out_w[0].astype(BF16))
    xb, h2b, aff_tb = _epilogue(o1, x1, mods_lat[1], gains[1], rw_t[1])
    moe1 = _moe(h2b, aff_tb, wg[1], wu[1], wd[1])
    return _final(xb, moe1, mods_lat[1], gains[1])
```

```python
import functools

import jax
import jax.numpy as jnp
from jax import lax
from jax.experimental import pallas as pl
from jax.experimental.pallas import tpu as pltpu

F32 = jnp.float32
BF16 = jnp.bfloat16
HIGHEST = lax.Precision.HIGHEST
EPS = 1e-6

LANES = 128
SUBLANES = 8
MIB = 1024 * 1024

N_MOD = 6
GRID_W = 64
SSD_HEADDIM = 64
SSD_HEADS = 32
SSD_GROUPS = 4
SSD_HPG = SSD_HEADS // SSD_GROUPS
SSD_STATE = 128
SSD_CONV = 5
SSD_CHUNK = 128
MLP_CHUNK = 128
MLP_GROUPS = 16
MLP_ROW_GROUPS = 8
N_EXPERTS = 16
CAPACITY_FACTOR = 2

CONV_HALO = SUBLANES
TOKEN_TILE = 256


def _cparams(n_axes, vmem_mib):
    return pltpu.CompilerParams(dimension_semantics=("arbitrary",) * n_axes,
                                vmem_limit_bytes=vmem_mib * MIB)


def _sigmoid(x):
    return 1.0 / (1.0 + jnp.exp(-x))


def _rms(x, g):
    return x * lax.rsqrt(jnp.mean(x * x, axis=-1, keepdims=True) + EPS) * g


def _dot(a, b):
    return jnp.dot(a, b, preferred_element_type=F32)


def _dot_nt(a, b, precision=None):
    return lax.dot_general(a, b, (((1,), (1,)), ((), ())), preferred_element_type=F32, precision=precision)


def _dot_tn(a, b):
    return lax.dot_general(a, b, (((0,), (0,)), ((), ())), preferred_element_type=F32)


def _mod_kernel(c_ref, w_ref, b_ref, o_ref):
    c = c_ref[...]
    s = c * _sigmoid(c)
    o_ref[0] = jnp.dot(s, w_ref[0], preferred_element_type=F32, precision=HIGHEST) + b_ref[0]


def _modulation(crows, mod_w, mod_b):
    depth, d, n = mod_w.shape
    rows = crows.shape[0]
    tn = 1536
    return pl.pallas_call(
        _mod_kernel,
        grid=(depth, n // tn),
        in_specs=[pl.BlockSpec((rows, d), lambda i, j: (0, 0)),
                  pl.BlockSpec((1, d, tn), lambda i, j: (i, 0, j)),
                  pl.BlockSpec((1, 1, tn), lambda i, j: (i, 0, j))],
        out_specs=pl.BlockSpec((1, rows, tn), lambda i, j: (i, 0, j)),
        out_shape=jax.ShapeDtypeStruct((depth, rows, n), F32),
        compiler_params=_cparams(2, 40),
        name="modulation",
    )(crows, mod_w, mod_b.reshape(depth, 1, n))


def _ssd_in_kernel(ctx_ref, x_ref, xp_ref, xn_ref, mod_ref, g_ref, w_ref, cw_ref, cb_ref, dtb_ref,
                   z_ref, xs_ref, bc_ref, dt_ref, cbuf_ref, *, n_tiles, inner, conv_dim):
    j = pl.program_id(1)
    tm = x_ref.shape[1]
    halo = xp_ref.shape[1]
    xc = jnp.where(j == 0, ctx_ref[0], x_ref[0])
    xa = jnp.concatenate([xp_ref[0], xc, xn_ref[0]], axis=0)
    h = _rms(xa, g_ref[0:1, :]) * (1.0 + mod_ref[0, 0, 1:2, :]) + mod_ref[0, 0, 0:1, :]
    r = _dot(h.astype(BF16), w_ref[...])
    z_ref[0] = r[halo:halo + tm, :inner].astype(BF16)
    row = lax.broadcasted_iota(jnp.int32, (tm + 2 * halo, 1), 0)
    drop_top = jnp.logical_and(row < halo, j <= 1)
    drop_bot = jnp.logical_and(row >= halo + tm, jnp.logical_or(j == 0, j == n_tiles - 1))
    keep = jnp.logical_not(jnp.logical_or(drop_top, drop_bot))
    cbuf_ref[...] = jnp.where(keep, r[:, inner:inner + conv_dim], 0.0)
    pad = (SSD_CONV - 1) // 2
    acc = cb_ref[...] + cw_ref[0:1, :] * cbuf_ref[pl.ds(halo - pad, tm), :]
    for k in range(1, SSD_CONV):
        acc = acc + cw_ref[k:k + 1, :] * cbuf_ref[pl.ds(halo - pad + k, tm), :]
    act = acc * _sigmoid(acc)
    xs_ref[0] = act[:, :inner].astype(BF16)
    bc_ref[0] = act[:, inner:].astype(BF16)
    dtr = r[halo:halo + tm, inner + conv_dim:] + dtb_ref[...]
    dt_ref[0] = jnp.maximum(dtr, 0.0) + jnp.log1p(jnp.exp(-jnp.abs(dtr)))


def _ssd_in_proj(ctx, x, mods01, g, w, conv_w, conv_b, dt_bias):
    b, l, d = x.shape
    lc = ctx.shape[1]
    tm = TOKEN_TILE
    assert lc == tm and l % tm == 0
    n_tiles = 1 + l // tm
    inner = SSD_HEADS * SSD_HEADDIM
    conv_dim = conv_w.shape[1]
    bc_dim = conv_dim - inner
    halo = CONV_HALO
    blocks_per_tile = tm // halo
    last_halo_block = l // halo - 1
    lt = lc + l
    kern = functools.partial(_ssd_in_kernel, n_tiles=n_tiles, inner=inner, conv_dim=conv_dim)
    return pl.pallas_call(
        kern,
        grid=(b, n_tiles),
        in_specs=[
            pl.BlockSpec((1, lc, d), lambda i, j: (i, 0, 0)),
            pl.BlockSpec((1, tm, d), lambda i, j: (i, jnp.maximum(j - 1, 0), 0)),
            pl.BlockSpec((1, halo, d), lambda i, j: (i, jnp.maximum((j - 1) * blocks_per_tile - 1, 0), 0)),
            pl.BlockSpec((1, halo, d), lambda i, j: (i, jnp.minimum(j * blocks_per_tile, last_halo_block), 0)),
            pl.BlockSpec((1, 1, SUBLANES, d), lambda i, j: (i, jnp.minimum(j, 1), 0, 0)),
            pl.BlockSpec((SUBLANES, d), lambda i, j: (0, 0)),
            pl.BlockSpec(w.shape, lambda i, j: (0, 0), pipeline_mode=pl.Buffered(1)),
            pl.BlockSpec((SUBLANES, conv_dim), lambda i, j: (0, 0)),
            pl.BlockSpec((1, conv_dim), lambda i, j: (0, 0)),
            pl.BlockSpec((1, LANES), lambda i, j: (0, 0)),
        ],
        out_specs=[
            pl.BlockSpec((1, tm, inner), lambda i, j: (i, j, 0)),
            pl.BlockSpec((1, tm, inner), lambda i, j: (i, j, 0)),
            pl.BlockSpec((1, tm, bc_dim), lambda i, j: (i, j, 0)),
            pl.BlockSpec((1, tm, LANES), lambda i, j: (i, j, 0)),
        ],
        out_shape=[
            jax.ShapeDtypeStruct((b, lt, inner), BF16),
            jax.ShapeDtypeStruct((b, lt, inner), BF16),
            jax.ShapeDtypeStruct((b, lt, bc_dim), BF16),
            jax.ShapeDtypeStruct((b, lt, LANES), F32),
        ],
        scratch_shapes=[pltpu.VMEM((tm + 2 * halo, conv_dim), F32)],
        compiler_params=_cparams(2, 56),
        name="ssd_in_proj",
    )(ctx, x, x, x, mods01, g, w, conv_w, conv_b, dt_bias)


def _expand_exact(row, e):
    r8 = jnp.broadcast_to(row, (SUBLANES, row.shape[1]))
    hi = r8.astype(BF16)
    r1 = r8 - hi.astype(F32)
    mid = r1.astype(BF16)
    lo = (r1 - mid.astype(F32)).astype(BF16)
    return (_dot(hi, e) + _dot(mid, e) + _dot(lo, e))[0:1, :]


def _ssd_step(rev, s, xs_ref, bc_ref, dt_ref, alog_ref, e_ref, dskip_ref, y_ref, st_ref, ybuf_ref, n_ctx_chunks,
              n_chunks):
    c = SSD_CHUNK
    if rev:
        ci = jnp.where(s < n_ctx_chunks, n_ctx_chunks - 1 - s, n_chunks + 2 * n_ctx_chunks - 1 - s)
    else:
        ci = s
    is_lat = ci >= n_ctx_chunks
    lc = jnp.maximum(ci - n_ctx_chunks, 0)
    dcol = SSD_HEADS * int(rev)

    bc = bc_ref[0]
    dt = dt_ref[0]
    a = dt * (-jnp.exp(alog_ref[int(rev):int(rev) + 1, :]))
    li = lax.broadcasted_iota(jnp.int32, (c, c), 0)
    si = lax.broadcasted_iota(jnp.int32, (c, c), 1)
    causal = (si >= li) if rev else (si <= li)
    tri = jnp.where(causal, 1.0, 0.0).astype(F32)
    cs = jnp.dot(tri, a, preferred_element_type=F32, precision=HIGHEST)
    tot = cs[0:1, :] if rev else cs[c - 1:c, :]
    e = e_ref[int(rev)]
    w = jnp.exp(tot - cs) * dt
    wx = _dot(w.astype(BF16), e)
    xw = (xs_ref[0].astype(F32) * wx).astype(BF16)
    gw = SSD_HPG * SSD_HEADDIM
    bc_off = SSD_GROUPS * SSD_STATE

    @pl.when(is_lat)
    def _():
        cs_t = cs.T
        dt_t = dt.T
        px = _dot(jnp.exp(cs).astype(BF16), e)
        lane = lax.broadcasted_iota(jnp.int32, (c, LANES), 1)
        for g in range(SSD_GROUPS):
            bg = bc[:, g * SSD_STATE:(g + 1) * SSD_STATE]
            cg = bc[:, bc_off + g * SSD_STATE:bc_off + (g + 1) * SSD_STATE]
            y_off = _dot(cg, st_ref[:, g * gw:(g + 1) * gw].astype(BF16)) * px[:, g * gw:(g + 1) * gw]
            cb = _dot_nt(cg, bg)
            for hp in range(SSD_HPG // 2):
                ms = []
                for hh in range(2):
                    hd = dcol + g * SSD_HPG + 2 * hp + hh
                    diff = cs[:, hd:hd + 1] - cs_t[hd:hd + 1, :]
                    dec = jnp.exp(jnp.where(causal, diff, -jnp.inf))
                    ms.append(cb * dec * dt_t[hd:hd + 1, :])
                mp = jnp.concatenate(ms, axis=1).astype(BF16)
                col0 = (g * SSD_HPG + 2 * hp) * SSD_HEADDIM
                xp = xs_ref[0, :, col0:col0 + LANES].astype(F32)
                rhs = jnp.concatenate([jnp.where(lane < SSD_HEADDIM, xp, 0.0),
                                       jnp.where(lane >= SSD_HEADDIM, xp, 0.0)], axis=0).astype(BF16)
                y_pair = _dot(mp, rhs) + y_off[:, 2 * hp * SSD_HEADDIM:2 * hp * SSD_HEADDIM + LANES]
                if rev:
                    tot_y = ybuf_ref[lc, :, col0:col0 + LANES] + y_pair + dskip_ref[:, col0:col0 + LANES] * xp
                    y_ref[0, :, col0:col0 + LANES] = tot_y.astype(BF16)
                else:
                    ybuf_ref[lc, :, col0:col0 + LANES] = y_pair

    decay = _expand_exact(jnp.exp(tot), e)
    for g in range(SSD_GROUPS):
        bg = bc[:, g * SSD_STATE:(g + 1) * SSD_STATE]
        upd = _dot_tn(bg, xw[:, g * gw:(g + 1) * gw])
        st_ref[:, g * gw:(g + 1) * gw] = st_ref[:, g * gw:(g + 1) * gw] * decay[:, g * gw:(g + 1) * gw] + upd


def _ssd_scan_kernel(xs_ref, bc_ref, dt_ref, alog_ref, e_ref, dskip_ref, y_ref, st_ref, ybuf_ref, *,
                     n_ctx_chunks, n_chunks):
    d = pl.program_id(1)
    s = pl.program_id(2)

    @pl.when(s == 0)
    def _():
        st_ref[...] = jnp.zeros_like(st_ref)

    for rev in (False, True):
        @pl.when(d == int(rev))
        def _(rev=rev):
            _ssd_step(rev, s, xs_ref, bc_ref, dt_ref, alog_ref, e_ref, dskip_ref, y_ref, st_ref, ybuf_ref,
                      n_ctx_chunks, n_chunks)


def _ssd_scan(xs_all, bc_all, dt_all, alog, expand, dskip, n_ctx_chunks, n_chunks):
    b, lt, inner = xs_all.shape
    c = SSD_CHUNK
    steps = n_ctx_chunks + n_chunks

    def chunk_idx(d, s):
        bwd = jnp.where(s < n_ctx_chunks, n_ctx_chunks - 1 - s, steps + n_ctx_chunks - 1 - s)
        return jnp.where(d == 0, s, bwd)

    def out_idx(d, s):
        return jnp.where(jnp.logical_and(d == 1, s >= n_ctx_chunks), steps - 1 - s, n_chunks - 1)

    kern = functools.partial(_ssd_scan_kernel, n_ctx_chunks=n_ctx_chunks, n_chunks=n_chunks)
    return pl.pallas_call(
        kern,
        grid=(b, 2, steps),
        in_specs=[
            pl.BlockSpec((1, c, inner), lambda i, d, s: (i, chunk_idx(d, s), 0)),
            pl.BlockSpec((1, c, bc_all.shape[2]), lambda i, d, s: (i, chunk_idx(d, s), 0)),
            pl.BlockSpec((1, c, LANES), lambda i, d, s: (i, chunk_idx(d, s), 0)),
            pl.BlockSpec((SUBLANES, LANES), lambda i, d, s: (0, 0)),
            pl.BlockSpec((2, LANES, inner), lambda i, d, s: (0, 0, 0)),
            pl.BlockSpec((1, inner), lambda i, d, s: (0, 0)),
        ],
        out_specs=pl.BlockSpec((1, c, inner), lambda i, d, s: (i, out_idx(d, s), 0)),
        out_shape=jax.ShapeDtypeStruct((b, n_chunks * c, inner), BF16),
        scratch_shapes=[pltpu.VMEM((SSD_STATE, inner), F32),
                        pltpu.VMEM((n_chunks, c, inner), F32)],
        compiler_params=_cparams(3, 48),
        name="ssd_scan",
    )(xs_all, bc_all, dt_all, alog, expand, dskip)


def _mixer_epilogue(o, x_ref, mod_ref, g_ref, rw_ref, xo_ref, h2_ref, aff_ref):
    xn = x_ref[0] + mod_ref[0, 2:3, :] * _rms(o, g_ref[1:2, :])
    xo_ref[0] = xn
    h2 = _rms(xn, g_ref[2:3, :]) * (1.0 + mod_ref[0, 4:5, :]) + mod_ref[0, 3:4, :]
    h2_ref[0] = h2.astype(BF16)
    lg = _dot_nt(rw_ref[...], h2, precision=HIGHEST)
    ex = jnp.exp(lg - jnp.max(lg, axis=0, keepdims=True))
    aff_ref[0] = ex / jnp.sum(ex, axis=0, keepdims=True)


def _ssd_out_kernel(y_ref, z_ref, x_ref, mod_ref, g_ref, ng_ref, w_ref, rw_ref, xo_ref, h2_ref, aff_ref):
    z = z_ref[0].astype(F32)
    yg = (y_ref[0].astype(F32) * (z * _sigmoid(z)))
    gdim = yg.shape[1] // SSD_GROUPS
    parts = [_rms(yg[:, g * gdim:(g + 1) * gdim], ng_ref[:, g * gdim:(g + 1) * gdim]).astype(BF16)
             for g in range(SSD_GROUPS)]
    o = _dot(jnp.concatenate(parts, axis=1), w_ref[...])
    _mixer_epilogue(o, x_ref, mod_ref, g_ref, rw_ref, xo_ref, h2_ref, aff_ref)


def _epilogue_outs(b, l, d, tm):
    out_specs = [pl.BlockSpec((1, tm, d), lambda i, j: (i, j, 0)),
                 pl.BlockSpec((1, tm, d), lambda i, j: (i, j, 0)),
                 pl.BlockSpec((1, N_EXPERTS, tm), lambda i, j: (i, 0, j))]
    out_shape = [jax.ShapeDtypeStruct((b, l, d), F32),
                 jax.ShapeDtypeStruct((b, l, d), BF16),
                 jax.ShapeDtypeStruct((b, N_EXPERTS, l), F32)]
    return out_specs, out_shape


def _ssd_out(y, z_all, x, mods, g, ng, w, rw_t, ctx_tiles):
    b, l, d = x.shape
    inner = y.shape[2]
    tm = TOKEN_TILE
    z_off = ctx_tiles
    out_specs, out_shape = _epilogue_outs(b, l, d, tm)
    return pl.pallas_call(
        _ssd_out_kernel,
        grid=(b, l // tm),
        in_specs=[
            pl.BlockSpec((1, tm, inner), lambda i, j: (i, j, 0)),
            pl.BlockSpec((1, tm, inner), lambda i, j: (i, j + z_off, 0)),
            pl.BlockSpec((1, tm, d), lambda i, j: (i, j, 0)),
            pl.BlockSpec((1, SUBLANES, d), lambda i, j: (i, 0, 0)),
            pl.BlockSpec((SUBLANES, d), lambda i, j: (0, 0)),
            pl.BlockSpec((1, inner), lambda i, j: (0, 0)),
            pl.BlockSpec(w.shape, lambda i, j: (0, 0)),
            pl.BlockSpec(rw_t.shape, lambda i, j: (0, 0)),
        ],
        out_specs=out_specs,
        out_shape=out_shape,
        compiler_params=_cparams(2, 48),
        name="ssd_out",
    )(y, z_all, x, mods, g, ng, w, rw_t)


def _epilogue_kernel(o_ref, x_ref, mod_ref, g_ref, rw_ref, xo_ref, h2_ref, aff_ref):
    _mixer_epilogue(o_ref[0], x_ref, mod_ref, g_ref, rw_ref, xo_ref, h2_ref, aff_ref)


def _epilogue(o, x, mods, g, rw_t):
    b, l, d = x.shape
    tm = 2 * TOKEN_TILE
    out_specs, out_shape = _epilogue_outs(b, l, d, tm)
    return pl.pallas_call(
        _epilogue_kernel,
        grid=(b, l // tm),
        in_specs=[
            pl.BlockSpec((1, tm, d), lambda i, j: (i, j, 0)),
            pl.BlockSpec((1, tm, d), lambda i, j: (i, j, 0)),
            pl.BlockSpec((1, SUBLANES, d), lambda i, j: (i, 0, 0)),
            pl.BlockSpec((SUBLANES, d), lambda i, j: (0, 0)),
            pl.BlockSpec(rw_t.shape, lambda i, j: (0, 0)),
        ],
        out_specs=out_specs,
        out_shape=out_shape,
        compiler_params=_cparams(2, 40),
        name="mixer_epilogue",
    )(o, x, mods, g, rw_t)


def _lane_prefix_exclusive(m01, upper):
    e, t = m01.shape
    carry = jnp.zeros((e, 1), F32)
    outs = []
    for k in range(t // LANES):
        tile = m01[:, k * LANES:(k + 1) * LANES]
        incl = _dot(tile.astype(BF16), upper)
        outs.append(incl - tile + carry)
        carry = carry + incl[:, LANES - 1:LANES]
    return jnp.concatenate(outs, axis=1)


def _route(aff, cap):
    e, t = aff.shape
    key = pltpu.bitcast(aff, jnp.int32)

    def body(i, thr):
        cand = jnp.bitwise_or(thr, jnp.left_shift(jnp.int32(1), 30 - i))
        cnt = jnp.sum(jnp.where(key >= cand, 1.0, 0.0), axis=1, keepdims=True)
        return jnp.where(cnt >= cap, cand, thr)

    thr = lax.fori_loop(0, 31, body, jnp.zeros((e, 1), jnp.int32))
    gt = jnp.where(key > thr, 1.0, 0.0)
    eq = jnp.where(key == thr, 1.0, 0.0)
    need = cap - jnp.sum(gt, axis=1, keepdims=True)
    r = lax.broadcasted_iota(jnp.int32, (LANES, LANES), 0)
    cidx = lax.broadcasted_iota(jnp.int32, (LANES, LANES), 1)
    upper = jnp.where(r <= cidx, 1.0, 0.0).astype(BF16)
    eq_rank = _lane_prefix_exclusive(eq, upper)
    sel = gt + eq * jnp.where(eq_rank < need, 1.0, 0.0)
    pos = _lane_prefix_exclusive(sel, upper)
    return jnp.where(sel > 0.5, pos, -1.0).astype(jnp.int32)


MOE_EXPERTS_PER_STEP = 4
MOE_FFN_TILE = 1024


def _pick_rows(slot_ref, e, cap, t):
    srow = slot_ref[0, pl.ds(e, 1), :]
    return lax.broadcasted_iota(jnp.int32, (cap, t), 0) == srow


def _moe_gather_kernel(h_ref, aff_ref, xin_ref, gate_ref, slot_ref, *, cap):
    eg = pl.program_id(1)

    @pl.when(eg == 0)
    def _():
        slot_ref[0] = _route(aff_ref[0], cap)

    t = h_ref.shape[1]
    ps = []
    for i in range(MOE_EXPERTS_PER_STEP):
        e = eg * MOE_EXPERTS_PER_STEP + i
        pick = _pick_rows(slot_ref, e, cap, t)
        arow = aff_ref[0, pl.ds(e, 1), :]
        gate = jnp.sum(jnp.where(pick, arow, 0.0), axis=1, keepdims=True)
        gate_ref[i] = jnp.broadcast_to(gate, (cap, LANES))
        ps.append(jnp.where(pick, 1.0, 0.0).astype(BF16))
    xin = _dot(jnp.concatenate(ps, axis=0), h_ref[0])
    for i in range(MOE_EXPERTS_PER_STEP):
        xin_ref[i] = xin[i * cap:(i + 1) * cap, :].astype(BF16)


def _moe_gather(h2, aff_t):
    b, t, d = h2.shape
    ne = aff_t.shape[1]
    cap = (CAPACITY_FACTOR * t) // ne
    g = MOE_EXPERTS_PER_STEP
    return pl.pallas_call(
        functools.partial(_moe_gather_kernel, cap=cap),
        grid=(b, ne // g),
        in_specs=[pl.BlockSpec((1, t, d), lambda i, e: (i, 0, 0)),
                  pl.BlockSpec((1, ne, t), lambda i, e: (i, 0, 0))],
        out_specs=[pl.BlockSpec((g, cap, d), lambda i, e: (e, i, 0)),
                   pl.BlockSpec((g, cap, LANES), lambda i, e: (e, i, 0)),
                   pl.BlockSpec((1, ne, t), lambda i, e: (i, 0, 0))],
        out_shape=[jax.ShapeDtypeStruct((ne, b * cap, d), BF16),
                   jax.ShapeDtypeStruct((ne, b * cap, LANES), F32),
                   jax.ShapeDtypeStruct((b, ne, t), jnp.int32)],
        compiler_params=_cparams(2, 48),
        name="moe_gather",
    )(h2, aff_t)


def _moe_ffn_kernel(x_ref, gate_ref, wg_ref, wu_ref, wd_ref, y_ref, acc_ref):
    half = pl.program_id(2)
    x = x_ref[0]
    hid = _dot(x, wg_ref[0, 0].astype(BF16))
    hid = (hid * _sigmoid(hid)) * _dot(x, wu_ref[0, 0].astype(BF16))
    part = _dot(hid.astype(BF16), wd_ref[0, 0].astype(BF16))

    @pl.when(half == 0)
    def _():
        acc_ref[...] = part

    @pl.when(half == 1)
    def _():
        y_ref[0] = ((acc_ref[...] + part) * gate_ref[0, :, 0:1]).astype(BF16)


def _moe_ffn(xin, gates, w_gate, w_up, w_down, layer):
    ne, m, d = xin.shape
    ff = w_gate.shape[3]
    tm = MOE_FFN_TILE
    hf = ff // 2
    return pl.pallas_call(
        _moe_ffn_kernel,
        grid=(ne, m // tm, 2),
        in_specs=[pl.BlockSpec((1, tm, d), lambda e, j, h: (e, j, 0)),
                  pl.BlockSpec((1, tm, LANES), lambda e, j, h: (e, j, 0)),
                  pl.BlockSpec((1, 1, d, hf), lambda e, j, h: (layer, e, 0, h)),
                  pl.BlockSpec((1, 1, d, hf), lambda e, j, h: (layer, e, 0, h)),
                  pl.BlockSpec((1, 1, hf, d), lambda e, j, h: (layer, e, h, 0))],
        out_specs=pl.BlockSpec((1, tm, d), lambda e, j, h: (e, j, 0)),
        out_shape=jax.ShapeDtypeStruct((ne, m, d), BF16),
        scratch_shapes=[pltpu.VMEM((tm, d), F32)],
        compiler_params=_cparams(3, 56),
        name="moe_ffn",
    )(xin, gates, w_gate, w_up, w_down)


def _moe_scatter_kernel(slot_ref, y_ref, o_ref, *, cap):
    eg = pl.program_id(1)
    t = o_ref.shape[1]
    ps = [jnp.where(_pick_rows(slot_ref, eg * MOE_EXPERTS_PER_STEP + i, cap, t), 1.0, 0.0).astype(BF16)
          for i in range(MOE_EXPERTS_PER_STEP)]
    p = jnp.concatenate(ps, axis=0)
    y = y_ref[...].reshape(MOE_EXPERTS_PER_STEP * cap, y_ref.shape[2])
    contrib = _dot_tn(p, y)

    @pl.when(eg == 0)
    def _():
        o_ref[0] = contrib

    @pl.when(eg > 0)
    def _():
        o_ref[0] += contrib


def _moe_scatter(slot, y, b):
    ne, m, d = y.shape
    t = slot.shape[2]
    cap = m // b
    g = MOE_EXPERTS_PER_STEP
    return pl.pallas_call(
        functools.partial(_moe_scatter_kernel, cap=cap),
        grid=(b, ne // g),
        in_specs=[pl.BlockSpec((1, ne, t), lambda i, e: (i, 0, 0)),
                  pl.BlockSpec((g, cap, d), lambda i, e: (e, i, 0))],
        out_specs=pl.BlockSpec((1, t, d), lambda i, e: (i, 0, 0)),
        out_shape=jax.ShapeDtypeStruct((b, t, d), F32),
        compiler_params=_cparams(2, 56),
        name="moe_scatter",
    )(slot, y)


def _moe(h2, aff_t, w_gate, w_up, w_down, layer):
    xin, gates, slot = _moe_gather(h2, aff_t)
    y = _moe_ffn(xin, gates, w_gate, w_up, w_down, layer)
    return _moe_scatter(slot, y, h2.shape[0])


def _moe_residual(x_ref, moe_ref, mod_ref, g_ref):
    return x_ref[0] + mod_ref[0, 5:6, :] * _rms(moe_ref[0], g_ref[3:4, :])


def _mlp_in_kernel(xa_ref, moe_ref, modp_ref, gp_ref, mod_ref, g_ref, w_ref, vg_ref, x1_ref, u_ref, v_ref):
    x1 = _moe_residual(xa_ref, moe_ref, modp_ref, gp_ref)
    x1_ref[0] = x1
    h = _rms(x1, g_ref[0:1, :]) * (1.0 + mod_ref[0, 1:2, :]) + mod_ref[0, 0:1, :]
    r = _dot(h.astype(BF16), w_ref[...])
    ge = 0.5 * r * (1.0 + jnp.tanh(0.7978845608028654 * (r + 0.044715 * (r * r * r))))
    half = ge.shape[1] // 2
    u_ref[0] = ge[:, :half].astype(BF16)
    v = ge[:, half:]
    vc = v - jnp.mean(v, axis=-1, keepdims=True)
    vn = vc * lax.rsqrt(jnp.mean(vc * vc, axis=-1, keepdims=True) + EPS) * vg_ref[...]
    v_ref[0] = vn.astype(BF16)


def _mlp_in(xa, moe, mods_prev, g_prev, mods, g, w, vg):
    b, l, d = xa.shape
    tm = TOKEN_TILE
    half = w.shape[1] // 2
    tok = lambda i, j: (i, j, 0)
    return pl.pallas_call(
        _mlp_in_kernel,
        grid=(b, l // tm),
        in_specs=[
            pl.BlockSpec((1, tm, d), tok),
            pl.BlockSpec((1, tm, d), tok),
            pl.BlockSpec((1, SUBLANES, d), lambda i, j: (i, 0, 0)),
            pl.BlockSpec((SUBLANES, d), lambda i, j: (0, 0)),
            pl.BlockSpec((1, SUBLANES, d), lambda i, j: (i, 0, 0)),
            pl.BlockSpec((SUBLANES, d), lambda i, j: (0, 0)),
            pl.BlockSpec(w.shape, lambda i, j: (0, 0), pipeline_mode=pl.Buffered(1)),
            pl.BlockSpec((1, half), lambda i, j: (0, 0)),
        ],
        out_specs=[pl.BlockSpec((1, tm, d), tok),
                   pl.BlockSpec((1, tm, half), tok),
                   pl.BlockSpec((1, tm, half), tok)],
        out_shape=[jax.ShapeDtypeStruct((b, l, d), F32),
                   jax.ShapeDtypeStruct((b, l, half), BF16),
                   jax.ShapeDtypeStruct((b, l, half), BF16)],
        compiler_params=_cparams(2, 48),
        name="mlp_in_proj",
    )(xa, moe, mods_prev, g_prev, mods, g, w, vg)


def _mlp_mix_kernel(u_ref, v_ref, ws_ref, bs_ref, w_ref, o_ref, vf_ref, sf_ref, us_ref, *, rows):
    gp = pl.program_id(1)
    t = u_ref.shape[1]
    n_chunks = t // MLP_CHUNK
    per_step = ws_ref.shape[0]
    cols_per_chunk = MLP_CHUNK // rows

    @pl.when(gp < MLP_ROW_GROUPS // per_step)
    def _():
        for i in range(per_step):
            cs = slice(i * LANES, (i + 1) * LANES)
            for c in range(n_chunks):
                rs = slice(c * MLP_CHUNK, (c + 1) * MLP_CHUNK)
                s = _dot(ws_ref[i], v_ref[0, rs, cs]) + bs_ref[i]
                us_ref[rs, cs] = (u_ref[0, rs, cs].astype(F32) * s).astype(BF16)

    @pl.when(gp >= MLP_ROW_GROUPS // per_step)
    def _():
        for i in range(per_step):
            cs = slice(i * LANES, (i + 1) * LANES)
            vf_ref[...] = v_ref[0, :, cs].astype(F32)
            for k in range(n_chunks):
                xk = jnp.concatenate(
                    [vf_ref[pl.ds(k * cols_per_chunk + wl, rows, stride=GRID_W), :] for wl in range(cols_per_chunk)],
                    axis=0)
                s = _dot(ws_ref[i], xk.astype(BF16)) + bs_ref[i]
                for wl in range(cols_per_chunk):
                    sf_ref[pl.ds(k * cols_per_chunk + wl, rows, stride=GRID_W), :] = s[wl * rows:(wl + 1) * rows, :]
            us_ref[:, cs] = (u_ref[0, :, cs].astype(F32) * sf_ref[...]).astype(BF16)

    part = _dot(us_ref[...], w_ref[...])

    @pl.when(gp == 0)
    def _():
        o_ref[0] = part

    @pl.when(gp > 0)
    def _():
        o_ref[0] += part


def _mlp_mix(u, v, ws, bsb, w):
    b, l, inner = u.shape
    d = w.shape[1]
    per_step = 2
    steps = MLP_GROUPS // per_step
    kc = per_step * LANES
    rows = l // GRID_W
    return pl.pallas_call(
        functools.partial(_mlp_mix_kernel, rows=rows),
        grid=(b, steps),
        in_specs=[
            pl.BlockSpec((1, l, kc), lambda i, g: (i, 0, g)),
            pl.BlockSpec((1, l, kc), lambda i, g: (i, 0, g)),
            pl.BlockSpec((per_step, MLP_CHUNK, MLP_CHUNK), lambda i, g: (g, 0, 0)),
            pl.BlockSpec((per_step, MLP_CHUNK, LANES), lambda i, g: (g, 0, 0)),
            pl.BlockSpec((kc, d), lambda i, g: (g, 0)),
        ],
        out_specs=pl.BlockSpec((1, l, d), lambda i, g: (i, 0, 0)),
        out_shape=jax.ShapeDtypeStruct((b, l, d), F32),
        scratch_shapes=[pltpu.VMEM((l, LANES), F32), pltpu.VMEM((l, LANES), F32), pltpu.VMEM((l, kc), BF16)],
        compiler_params=_cparams(2, 48),
        name="mlp_mix",
    )(u, v, ws, bsb, w)


def _final_kernel(x_ref, moe_ref, mod_ref, g_ref, o_ref):
    o_ref[0] = _moe_residual(x_ref, moe_ref, mod_ref, g_ref)


def _final(x, moe, mods, g):
    b, l, d = x.shape
    tm = 2 * TOKEN_TILE
    tok = lambda i, j: (i, j, 0)
    return pl.pallas_call(
        _final_kernel,
        grid=(b, l // tm),
        in_specs=[pl.BlockSpec((1, tm, d), tok), pl.BlockSpec((1, tm, d), tok),
                  pl.BlockSpec((1, SUBLANES, d), lambda i, j: (i, 0, 0)),
                  pl.BlockSpec((SUBLANES, d), lambda i, j: (0, 0))],
        out_specs=pl.BlockSpec((1, tm, d), tok),
        out_shape=jax.ShapeDtypeStruct((b, l, d), F32),
        compiler_params=_cparams(2, 32),
        name="moe_residual",
    )(x, moe, mods, g)


def _pad_rows(a, rows):
    return jnp.pad(a, ((0, rows - a.shape[0]),) + ((0, 0),) * (a.ndim - 1))


def kernel(x, c, ctx, c_ctx, mod_w, mod_b, norm_g, ssd_in_w, ssd_conv_w, ssd_conv_b, ssd_dt_bias, ssd_a_log,
           ssd_d, ssd_norm_g, ssd_out_w, mlp_in_w, mlp_v_g, mlp_ws, mlp_bs, mlp_out_w, router_w, exp_w_gate,
           exp_w_up, exp_w_down):
    b, l, d = x.shape
    inner = SSD_HEADS * SSD_HEADDIM
    n_chunks = l // SSD_CHUNK
    n_ctx_chunks = ctx.shape[1] // SSD_CHUNK

    crows = _pad_rows(jnp.concatenate([c, c_ctx[None, :]], axis=0), -(-(b + 1) // SUBLANES) * SUBLANES)
    mod = _modulation(crows, mod_w, mod_b)
    mods_lat = [jnp.pad(mod[i, :b].reshape(b, N_MOD, d), ((0, 0), (0, SUBLANES - N_MOD), (0, 0)))
                for i in range(2)]
    mods_ctx = jnp.pad(mod[0, b].reshape(1, N_MOD, d), ((0, 0), (0, SUBLANES - N_MOD), (0, 0)))
    mods01 = jnp.stack([jnp.broadcast_to(mods_ctx, (b, SUBLANES, d)), mods_lat[0]], axis=1)
    gains = [_pad_rows(norm_g[i], SUBLANES) for i in range(2)]
    rw_t = [router_w[i].T for i in range(2)]

    in_w = ssd_in_w[0]
    conv_dim = ssd_conv_w.shape[2]
    w_in = jnp.pad(in_w, ((0, 0), (0, LANES - 2 * SSD_HEADS))).astype(BF16)
    dt_bias = jnp.pad(ssd_dt_bias[0].reshape(1, 2 * SSD_HEADS), ((0, 0), (0, LANES - 2 * SSD_HEADS)))
    z_all, xs_all, bc_all, dt_all = _ssd_in_proj(
        ctx, x, mods01, gains[0], w_in, _pad_rows(ssd_conv_w[0], SUBLANES), ssd_conv_b[0].reshape(1, conv_dim),
        dt_bias)

    alog = _pad_rows(jnp.stack([jnp.pad(ssd_a_log[0, 0], (0, LANES - SSD_HEADS)),
                                jnp.pad(ssd_a_log[0, 1], (SSD_HEADS, LANES - 2 * SSD_HEADS))]), SUBLANES)
    head_of_col = jnp.arange(inner, dtype=jnp.int32) // SSD_HEADDIM
    rows128 = jnp.arange(LANES, dtype=jnp.int32)[:, None]
    expand = jnp.stack([(rows128 == head_of_col[None, :] + SSD_HEADS * dd) for dd in range(2)]).astype(BF16)
    dskip = jnp.repeat(ssd_d[0], SSD_HEADDIM).reshape(1, inner)
    y = _ssd_scan(xs_all, bc_all, dt_all, alog, expand, dskip, n_ctx_chunks, n_chunks)

    xa, h2, aff_t = _ssd_out(y, z_all, x, mods_lat[0], gains[0], ssd_norm_g[0].reshape(1, inner),
                             ssd_out_w[0].astype(BF16), rw_t[0], ctx.shape[1] // TOKEN_TILE)
    moe0 = _moe(h2, aff_t, exp_w_gate, exp_w_up, exp_w_down, 0)

    x1, u, v = _mlp_in(xa, moe0, mods_lat[0], gains[0], mods_lat[1], gains[1], mlp_in_w[0].astype(BF16),
                       mlp_v_g[0].reshape(1, -1))
    bsb = jnp.broadcast_to(mlp_bs[0][:, :, None], (MLP_GROUPS, MLP_CHUNK, LANES))
    o1 = _mlp_mix(u, v, mlp_ws[0].astype(BF16), bsb, mlp_out_w[0].astype(BF16))
    xb, h2b, aff_tb = _epilogue(o1, x1, mods_lat[1], gains[1], rw_t[1])
    moe1 = _moe(h2b, aff_tb, exp_w_gate, exp_w_up, exp_w_down, 1)
    return _final(xb, moe1, mods_lat[1], gains[1])
```

```python
import functools

import jax
import jax.numpy as jnp
from jax import lax
from jax.experimental import pallas as pl
from jax.experimental.pallas import tpu as pltpu

F32 = jnp.float32
BF16 = jnp.bfloat16
HIGHEST = lax.Precision.HIGHEST
EPS = 1e-6

LANES = 128
SUBLANES = 8
MIB = 1024 * 1024

N_MOD = 6
GRID_W = 64
SSD_HEADDIM = 64
SSD_HEADS = 32
SSD_GROUPS = 4
SSD_HPG = SSD_HEADS // SSD_GROUPS
SSD_STATE = 128
SSD_CONV = 5
SSD_CHUNK = 128
MLP_CHUNK = 128
MLP_GROUPS = 16
MLP_ROW_GROUPS = 8
N_EXPERTS = 16
CAPACITY_FACTOR = 2

CONV_HALO = SUBLANES
TOKEN_TILE = 256
CONV_COL_BLOCK = 1024


def _cparams(n_axes, vmem_mib):
    return pltpu.CompilerParams(dimension_semantics=("arbitrary",) * n_axes,
                                vmem_limit_bytes=vmem_mib * MIB)


def _sigmoid(x):
    return 0.5 * (1.0 + jnp.tanh(0.5 * x))


def _rms(x, g):
    return x * lax.rsqrt(jnp.mean(x * x, axis=-1, keepdims=True) + EPS) * g


def _dot(a, b):
    return jnp.dot(a, b, preferred_element_type=F32)


def _dot_nt(a, b, precision=None):
    return lax.dot_general(a, b, (((1,), (1,)), ((), ())), preferred_element_type=F32, precision=precision)


def _dot_tn(a, b):
    return lax.dot_general(a, b, (((0,), (0,)), ((), ())), preferred_element_type=F32)


def _mod_kernel(c_ref, w_ref, b_ref, o_ref):
    c = c_ref[...]
    s = c * _sigmoid(c)
    o_ref[0] = jnp.dot(s, w_ref[0], preferred_element_type=F32, precision=HIGHEST) + b_ref[0]


def _modulation(crows, mod_w, mod_b):
    depth, d, n = mod_w.shape
    rows = crows.shape[0]
    tn = 1536
    return pl.pallas_call(
        _mod_kernel,
        grid=(depth, n // tn),
        in_specs=[pl.BlockSpec((rows, d), lambda i, j: (0, 0)),
                  pl.BlockSpec((1, d, tn), lambda i, j: (i, 0, j)),
                  pl.BlockSpec((1, 1, tn), lambda i, j: (i, 0, j))],
        out_specs=pl.BlockSpec((1, rows, tn), lambda i, j: (i, 0, j)),
        out_shape=jax.ShapeDtypeStruct((depth, rows, n), F32),
        compiler_params=_cparams(2, 40),
        name="modulation",
    )(crows, mod_w, mod_b.reshape(depth, 1, n))


def _ssd_in_kernel(ctx_ref, x_ref, xp_ref, xn_ref, mod_ref, g_ref, w_ref, cw_ref, cb_ref, dtb_ref,
                   z_ref, xs_ref, bc_ref, dt_ref, *, n_tiles, inner, conv_dim):
    j = pl.program_id(1)
    tm = x_ref.shape[1]
    halo = xp_ref.shape[1]
    rows = tm + 2 * halo
    xc = jnp.where(j == 0, ctx_ref[0], x_ref[0])
    xa = jnp.concatenate([xp_ref[0], xc, xn_ref[0]], axis=0)
    h = (_rms(xa, g_ref[0:1, :]) * (1.0 + mod_ref[0, 0, 1:2, :]) + mod_ref[0, 0, 0:1, :]).astype(BF16)
    hc = h[halo:halo + tm, :]
    z_ref[0] = _dot(hc, w_ref[:, :inner]).astype(BF16)
    dtr = _dot(hc, w_ref[:, inner + conv_dim:]) + dtb_ref[...]
    dt_ref[0] = jnp.maximum(dtr, 0.0) + jnp.log1p(jnp.exp(-jnp.abs(dtr)))
    row = lax.broadcasted_iota(jnp.int32, (rows, 1), 0)
    drop_top = jnp.logical_and(row < halo, j <= 1)
    drop_bot = jnp.logical_and(row >= halo + tm, jnp.logical_or(j == 0, j == n_tiles - 1))
    keep = jnp.where(jnp.logical_or(drop_top, drop_bot), 0.0, 1.0)
    pad = (SSD_CONV - 1) // 2
    cblk = CONV_COL_BLOCK
    for cbi in range(conv_dim // cblk):
        c0 = cbi * cblk
        u = _dot(h, w_ref[:, inner + c0:inner + c0 + cblk]) * keep
        acc = cb_ref[:, c0:c0 + cblk] + cw_ref[pad:pad + 1, c0:c0 + cblk] * u[halo:halo + tm, :]
        for k in range(SSD_CONV):
            if k != pad:
                shifted = pltpu.roll(u, (pad - k) % rows, axis=0)
                acc = acc + cw_ref[k:k + 1, c0:c0 + cblk] * shifted[halo:halo + tm, :]
        act = (acc * _sigmoid(acc)).astype(BF16)
        if c0 < inner:
            xs_ref[0, :, c0:c0 + cblk] = act
        else:
            bc_ref[0, :, c0 - inner:c0 - inner + cblk] = act


def _ssd_in_proj(ctx, x, mods01, g, w, conv_w, conv_b, dt_bias):
    b, l, d = x.shape
    lc = ctx.shape[1]
    tm = TOKEN_TILE
    assert lc == tm and l % tm == 0
    n_tiles = 1 + l // tm
    inner = SSD_HEADS * SSD_HEADDIM
    conv_dim = conv_w.shape[1]
    bc_dim = conv_dim - inner
    halo = CONV_HALO
    blocks_per_tile = tm // halo
    last_halo_block = l // halo - 1
    lt = lc + l
    kern = functools.partial(_ssd_in_kernel, n_tiles=n_tiles, inner=inner, conv_dim=conv_dim)
    return pl.pallas_call(
        kern,
        grid=(b, n_tiles),
        in_specs=[
            pl.BlockSpec((1, lc, d), lambda i, j: (i, 0, 0)),
            pl.BlockSpec((1, tm, d), lambda i, j: (i, jnp.maximum(j - 1, 0), 0)),
            pl.BlockSpec((1, halo, d), lambda i, j: (i, jnp.maximum((j - 1) * blocks_per_tile - 1, 0), 0)),
            pl.BlockSpec((1, halo, d), lambda i, j: (i, jnp.minimum(j * blocks_per_tile, last_halo_block), 0)),
            pl.BlockSpec((1, 1, SUBLANES, d), lambda i, j: (i, jnp.minimum(j, 1), 0, 0)),
            pl.BlockSpec((SUBLANES, d), lambda i, j: (0, 0)),
            pl.BlockSpec(w.shape, lambda i, j: (0, 0), pipeline_mode=pl.Buffered(1)),
            pl.BlockSpec((SUBLANES, conv_dim), lambda i, j: (0, 0)),
            pl.BlockSpec((1, conv_dim), lambda i, j: (0, 0)),
            pl.BlockSpec((1, LANES), lambda i, j: (0, 0)),
        ],
        out_specs=[
            pl.BlockSpec((1, tm, inner), lambda i, j: (i, jnp.maximum(j - 1, 0), 0)),
            pl.BlockSpec((1, tm, inner), lambda i, j: (i, j, 0)),
            pl.BlockSpec((1, tm, bc_dim), lambda i, j: (i, j, 0)),
            pl.BlockSpec((1, tm, LANES), lambda i, j: (i, j, 0)),
        ],
        out_shape=[
            jax.ShapeDtypeStruct((b, l, inner), BF16),
            jax.ShapeDtypeStruct((b, lt, inner), BF16),
            jax.ShapeDtypeStruct((b, lt, bc_dim), BF16),
            jax.ShapeDtypeStruct((b, lt, LANES), F32),
        ],
        compiler_params=_cparams(2, 56),
        name="ssd_in_proj",
    )(ctx, x, x, x, mods01, g, w, conv_w, conv_b, dt_bias)


SSD_CHUNKS_PER_STEP = 2
SSD_DECAY_ROWS = 16


def _split3(x):
    hi = x.astype(BF16)
    r1 = x - hi.astype(F32)
    mid = r1.astype(BF16)
    lo = (r1 - mid.astype(F32)).astype(BF16)
    return hi, mid, lo


def _ssd_step(rev, is_lat, lblk, xs_ref, bc_ref, dt_ref, alog_ref, e_ref, dskip_ref, y_ref, st_ref, ybuf_ref):
    c = SSD_CHUNK
    nck = SSD_CHUNKS_PER_STEP
    dcol = SSD_HEADS * int(rev)
    gw = SSD_HPG * SSD_HEADDIM
    bc_off = SSD_GROUPS * SSD_STATE
    a_neg = -jnp.exp(alog_ref[int(rev):int(rev) + 1, :])
    li = lax.broadcasted_iota(jnp.int32, (c, c), 0)
    si = lax.broadcasted_iota(jnp.int32, (c, c), 1)
    causal = (si >= li) if rev else (si <= li)
    tri = jnp.where(causal, 1.0, 0.0).astype(BF16)
    e = e_ref[int(rev)]
    head = lax.broadcasted_iota(jnp.int32, (c, LANES), 1)
    even_head = jnp.bitwise_and(head, 1) == 0

    dts = [dt_ref[0, k * c:(k + 1) * c, :] for k in range(nck)]
    pieces = []
    for k in range(nck):
        pieces.extend(_split3(dts[k] * a_neg))
    run = _dot(tri, jnp.concatenate(pieces, axis=1))
    css = [run[:, (3 * k) * LANES:(3 * k + 1) * LANES] + run[:, (3 * k + 1) * LANES:(3 * k + 2) * LANES]
           + run[:, (3 * k + 2) * LANES:(3 * k + 3) * LANES] for k in range(nck)]
    tots = [cs[0:1, :] if rev else cs[c - 1:c, :] for cs in css]

    blocks = []
    for k in range(nck):
        blocks.append((jnp.exp(tots[k] - css[k]) * dts[k]).astype(BF16))
        blocks.append(jnp.exp(css[k]).astype(BF16))
        blocks.append(jnp.where(even_head, dts[k], 0.0).astype(BF16))
        blocks.append(jnp.where(even_head, 0.0, dts[k]).astype(BF16))
    drow = lax.broadcasted_iota(jnp.int32, (SSD_DECAY_ROWS, LANES), 0)
    for k in range(nck):
        hi, mid, lo = _split3(jnp.broadcast_to(jnp.exp(tots[k]), (SSD_DECAY_ROWS, LANES)))
        rows3 = jnp.where(drow == 0, hi.astype(F32),
                          jnp.where(drow == 1, mid.astype(F32), jnp.where(drow == 2, lo.astype(F32), 0.0)))
        blocks.append(rows3.astype(BF16))
    ex = _dot(jnp.concatenate(blocks, axis=0), e)
    dec0 = 4 * nck * c

    for k in (range(nck - 1, -1, -1) if rev else range(nck)):
        base = 4 * k * c
        xs = xs_ref[0, k * c:(k + 1) * c, :].astype(F32)
        bc = bc_ref[0, k * c:(k + 1) * c, :]
        xw = (xs * ex[base:base + c, :]).astype(BF16)

        @pl.when(is_lat)
        def _(k=k, base=base, xs=xs, bc=bc):
            lc = lblk * nck + k
            cs = css[k]
            cs_t = cs.T
            px = ex[base + c:base + 2 * c, :]
            xdt_even = (xs * ex[base + 2 * c:base + 3 * c, :]).astype(BF16)
            xdt_odd = (xs * ex[base + 3 * c:base + 4 * c, :]).astype(BF16)
            for g in range(SSD_GROUPS):
                bg = bc[:, g * SSD_STATE:(g + 1) * SSD_STATE]
                cg = bc[:, bc_off + g * SSD_STATE:bc_off + (g + 1) * SSD_STATE]
                y_off = _dot(cg, st_ref[:, g * gw:(g + 1) * gw].astype(BF16)) * px[:, g * gw:(g + 1) * gw]
                cbm = jnp.where(causal, _dot_nt(cg, bg), 0.0)
                for hp in range(SSD_HPG // 2):
                    ms = []
                    for hh in range(2):
                        hd = dcol + g * SSD_HPG + 2 * hp + hh
                        diff = cs[:, hd:hd + 1] - cs_t[hd:hd + 1, :]
                        ms.append(cbm * jnp.exp(jnp.minimum(diff, 0.0)))
                    mp = jnp.concatenate(ms, axis=1).astype(BF16)
                    col0 = (g * SSD_HPG + 2 * hp) * SSD_HEADDIM
                    rhs = jnp.concatenate([xdt_even[:, col0:col0 + LANES], xdt_odd[:, col0:col0 + LANES]], axis=0)
                    y_pair = _dot(mp, rhs) + y_off[:, 2 * hp * SSD_HEADDIM:2 * hp * SSD_HEADDIM + LANES]
                    if rev:
                        tot_y = (ybuf_ref[lc, :, col0:col0 + LANES] + y_pair
                                 + dskip_ref[:, col0:col0 + LANES] * xs[:, col0:col0 + LANES])
                        y_ref[0, k * c:(k + 1) * c, col0:col0 + LANES] = tot_y.astype(BF16)
                    else:
                        ybuf_ref[lc, :, col0:col0 + LANES] = y_pair

        drows = ex[dec0 + k * SSD_DECAY_ROWS:dec0 + (k + 1) * SSD_DECAY_ROWS, :]
        decay = drows[0:1, :] + drows[1:2, :] + drows[2:3, :]
        for g in range(SSD_GROUPS):
            bg = bc[:, g * SSD_STATE:(g + 1) * SSD_STATE]
            upd = _dot_tn(bg, xw[:, g * gw:(g + 1) * gw])
            st_ref[:, g * gw:(g + 1) * gw] = st_ref[:, g * gw:(g + 1) * gw] * decay[:, g * gw:(g + 1) * gw] + upd


def _ssd_scan_kernel(xs_ref, bc_ref, dt_ref, alog_ref, e_ref, dskip_ref, y_ref, st_ref, ybuf_ref, *,
                     n_ctx_blocks, n_blocks):
    d = pl.program_id(1)
    s = pl.program_id(2)

    @pl.when(s == 0)
    def _():
        st_ref[...] = jnp.zeros_like(st_ref)

    for rev in (False, True):
        @pl.when(d == int(rev))
        def _(rev=rev):
            if rev:
                blk = jnp.where(s < n_ctx_blocks, n_ctx_blocks - 1 - s, n_blocks + 2 * n_ctx_blocks - 1 - s)
            else:
                blk = s
            _ssd_step(rev, blk >= n_ctx_blocks, jnp.maximum(blk - n_ctx_blocks, 0), xs_ref, bc_ref, dt_ref,
                      alog_ref, e_ref, dskip_ref, y_ref, st_ref, ybuf_ref)


def _ssd_scan(xs_all, bc_all, dt_all, alog, expand, dskip, n_ctx_chunks, n_chunks):
    b, lt, inner = xs_all.shape
    nck = SSD_CHUNKS_PER_STEP
    rows = nck * SSD_CHUNK
    assert n_ctx_chunks % nck == 0 and n_chunks % nck == 0
    n_ctx_blocks = n_ctx_chunks // nck
    n_blocks = n_chunks // nck
    steps = n_ctx_blocks + n_blocks

    def block_idx(d, s):
        bwd = jnp.where(s < n_ctx_blocks, n_ctx_blocks - 1 - s, steps + n_ctx_blocks - 1 - s)
        return jnp.where(d == 0, s, bwd)

    def out_idx(d, s):
        return jnp.where(jnp.logical_and(d == 1, s >= n_ctx_blocks), steps - 1 - s, n_blocks - 1)

    kern = functools.partial(_ssd_scan_kernel, n_ctx_blocks=n_ctx_blocks, n_blocks=n_blocks)
    return pl.pallas_call(
        kern,
        grid=(b, 2, steps),
        in_specs=[
            pl.BlockSpec((1, rows, inner), lambda i, d, s: (i, block_idx(d, s), 0)),
            pl.BlockSpec((1, rows, bc_all.shape[2]), lambda i, d, s: (i, block_idx(d, s), 0)),
            pl.BlockSpec((1, rows, LANES), lambda i, d, s: (i, block_idx(d, s), 0)),
            pl.BlockSpec((SUBLANES, LANES), lambda i, d, s: (0, 0)),
            pl.BlockSpec((2, LANES, inner), lambda i, d, s: (0, 0, 0)),
            pl.BlockSpec((1, inner), lambda i, d, s: (0, 0)),
        ],
        out_specs=pl.BlockSpec((1, rows, inner), lambda i, d, s: (i, out_idx(d, s), 0)),
        out_shape=jax.ShapeDtypeStruct((b, n_chunks * SSD_CHUNK, inner), BF16),
        scratch_shapes=[pltpu.VMEM((SSD_STATE, inner), F32),
                        pltpu.VMEM((n_chunks, SSD_CHUNK, inner), F32)],
        compiler_params=_cparams(3, 56),
        name="ssd_scan",
    )(xs_all, bc_all, dt_all, alog, expand, dskip)


def _mixer_epilogue(o, x, mod_ref, g_ref, rw_ref):
    xn = x + mod_ref[0, 2:3, :] * _rms(o, g_ref[1:2, :])
    h2 = _rms(xn, g_ref[2:3, :]) * (1.0 + mod_ref[0, 4:5, :]) + mod_ref[0, 3:4, :]
    h_hi = h2.astype(BF16)
    h_lo = (h2 - h_hi.astype(F32)).astype(BF16)
    ne = rw_ref.shape[0] // 2
    both = _dot_nt(rw_ref[...], h_hi)
    lg = both[:ne, :] + both[ne:, :] + _dot_nt(rw_ref[0:ne, :], h_lo)
    ex = jnp.exp(lg - jnp.max(lg, axis=0, keepdims=True))
    return xn, h_hi, ex / jnp.sum(ex, axis=0, keepdims=True)


def _ssd_out_kernel(y_ref, z_ref, x_ref, mod_ref, g_ref, ng_ref, w_ref, rw_ref, xo_ref, h2_ref, aff_ref):
    sub = TOKEN_TILE
    gdim = y_ref.shape[2] // SSD_GROUPS
    for r in range(y_ref.shape[1] // sub):
        rs = slice(r * sub, (r + 1) * sub)
        z = z_ref[0, rs, :].astype(F32)
        yg = y_ref[0, rs, :].astype(F32) * (z * _sigmoid(z))
        parts = [_rms(yg[:, g * gdim:(g + 1) * gdim], ng_ref[:, g * gdim:(g + 1) * gdim]).astype(BF16)
                 for g in range(SSD_GROUPS)]
        o = _dot(jnp.concatenate(parts, axis=1), w_ref[...])
        xn, hb, aff = _mixer_epilogue(o, x_ref[0, rs, :], mod_ref, g_ref, rw_ref)
        xo_ref[0, rs, :] = xn
        h2_ref[0, rs, :] = hb
        aff_ref[0, :, rs] = aff


def _epilogue_outs(b, l, d, tm):
    out_specs = [pl.BlockSpec((1, tm, d), lambda i, j: (i, j, 0)),
                 pl.BlockSpec((1, tm, d), lambda i, j: (i, j, 0)),
                 pl.BlockSpec((1, N_EXPERTS, tm), lambda i, j: (i, 0, j))]
    out_shape = [jax.ShapeDtypeStruct((b, l, d), F32),
                 jax.ShapeDtypeStruct((b, l, d), BF16),
                 jax.ShapeDtypeStruct((b, N_EXPERTS, l), F32)]
    return out_specs, out_shape


def _ssd_out(y, z, x, mods, g, ng, w, rw2):
    b, l, d = x.shape
    inner = y.shape[2]
    tm = 2 * TOKEN_TILE
    out_specs, out_shape = _epilogue_outs(b, l, d, tm)
    return pl.pallas_call(
        _ssd_out_kernel,
        grid=(b, l // tm),
        in_specs=[
            pl.BlockSpec((1, tm, inner), lambda i, j: (i, j, 0)),
            pl.BlockSpec((1, tm, inner), lambda i, j: (i, j, 0)),
            pl.BlockSpec((1, tm, d), lambda i, j: (i, j, 0)),
            pl.BlockSpec((1, SUBLANES, d), lambda i, j: (i, 0, 0)),
            pl.BlockSpec((SUBLANES, d), lambda i, j: (0, 0)),
            pl.BlockSpec((1, inner), lambda i, j: (0, 0)),
            pl.BlockSpec(w.shape, lambda i, j: (0, 0)),
            pl.BlockSpec(rw2.shape, lambda i, j: (0, 0)),
        ],
        out_specs=out_specs,
        out_shape=out_shape,
        compiler_params=_cparams(2, 48),
        name="ssd_out",
    )(y, z, x, mods, g, ng, w, rw2)


def _epilogue_kernel(o_ref, x_ref, mod_ref, g_ref, rw_ref, xo_ref, h2_ref, aff_ref):
    sub = TOKEN_TILE
    for r in range(o_ref.shape[1] // sub):
        rs = slice(r * sub, (r + 1) * sub)
        xn, hb, aff = _mixer_epilogue(o_ref[0, rs, :], x_ref[0, rs, :], mod_ref, g_ref, rw_ref)
        xo_ref[0, rs, :] = xn
        h2_ref[0, rs, :] = hb
        aff_ref[0, :, rs] = aff


def _epilogue(o, x, mods, g, rw_t):
    b, l, d = x.shape
    tm = 2 * TOKEN_TILE
    out_specs, out_shape = _epilogue_outs(b, l, d, tm)
    return pl.pallas_call(
        _epilogue_kernel,
        grid=(b, l // tm),
        in_specs=[
            pl.BlockSpec((1, tm, d), lambda i, j: (i, j, 0)),
            pl.BlockSpec((1, tm, d), lambda i, j: (i, j, 0)),
            pl.BlockSpec((1, SUBLANES, d), lambda i, j: (i, 0, 0)),
            pl.BlockSpec((SUBLANES, d), lambda i, j: (0, 0)),
            pl.BlockSpec(rw_t.shape, lambda i, j: (0, 0)),
        ],
        out_specs=out_specs,
        out_shape=out_shape,
        compiler_params=_cparams(2, 40),
        name="mixer_epilogue",
    )(o, x, mods, g, rw_t)


def _lane_prefix_exclusive(m01, upper):
    e, t = m01.shape
    carry = jnp.zeros((e, 1), F32)
    outs = []
    for k in range(t // LANES):
        tile = m01[:, k * LANES:(k + 1) * LANES]
        incl = _dot(tile.astype(BF16), upper)
        outs.append(incl - tile + carry)
        carry = carry + incl[:, LANES - 1:LANES]
    return jnp.concatenate(outs, axis=1)


def _route(aff, cap):
    e, t = aff.shape
    key = pltpu.bitcast(aff, jnp.int32)

    def body(i, thr):
        cand = jnp.bitwise_or(thr, jnp.left_shift(jnp.int32(1), 30 - i))
        cnt = jnp.sum(jnp.where(key >= cand, 1.0, 0.0), axis=1, keepdims=True)
        return jnp.where(cnt >= cap, cand, thr)

    thr = lax.fori_loop(0, 31, body, jnp.zeros((e, 1), jnp.int32))
    gt = jnp.where(key > thr, 1.0, 0.0)
    eq = jnp.where(key == thr, 1.0, 0.0)
    need = cap - jnp.sum(gt, axis=1, keepdims=True)
    r = lax.broadcasted_iota(jnp.int32, (LANES, LANES), 0)
    cidx = lax.broadcasted_iota(jnp.int32, (LANES, LANES), 1)
    upper = jnp.where(r <= cidx, 1.0, 0.0).astype(BF16)
    eq_rank = _lane_prefix_exclusive(eq, upper)
    sel = gt + eq * jnp.where(eq_rank < need, 1.0, 0.0)
    pos = _lane_prefix_exclusive(sel, upper)
    return jnp.where(sel > 0.5, pos, -1.0).astype(jnp.int32)


MOE_EXPERTS_PER_STEP = 4
MOE_FFN_TILE = 1024


def _pick_rows(slot_ref, e, cap, t):
    srow = slot_ref[0, pl.ds(e, 1), :]
    return lax.broadcasted_iota(jnp.int32, (cap, t), 0) == srow


def _moe_gather_kernel(h_ref, aff_ref, xin_ref, gate_ref, slot_ref, *, cap):
    eg = pl.program_id(1)

    @pl.when(eg == 0)
    def _():
        slot_ref[0] = _route(aff_ref[0], cap)

    t = h_ref.shape[1]
    ps = []
    for i in range(MOE_EXPERTS_PER_STEP):
        e = eg * MOE_EXPERTS_PER_STEP + i
        pick = _pick_rows(slot_ref, e, cap, t)
        arow = aff_ref[0, pl.ds(e, 1), :]
        gate = jnp.sum(jnp.where(pick, arow, 0.0), axis=1, keepdims=True)
        gate_ref[i] = jnp.broadcast_to(gate, (cap, LANES))
        ps.append(jnp.where(pick, 1.0, 0.0).astype(BF16))
    xin = _dot(jnp.concatenate(ps, axis=0), h_ref[0])
    for i in range(MOE_EXPERTS_PER_STEP):
        xin_ref[i] = xin[i * cap:(i + 1) * cap, :].astype(BF16)


def _moe_gather(h2, aff_t):
    b, t, d = h2.shape
    ne = aff_t.shape[1]
    cap = (CAPACITY_FACTOR * t) // ne
    g = MOE_EXPERTS_PER_STEP
    return pl.pallas_call(
        functools.partial(_moe_gather_kernel, cap=cap),
        grid=(b, ne // g),
        in_specs=[pl.BlockSpec((1, t, d), lambda i, e: (i, 0, 0)),
                  pl.BlockSpec((1, ne, t), lambda i, e: (i, 0, 0))],
        out_specs=[pl.BlockSpec((g, cap, d), lambda i, e: (e, i, 0)),
                   pl.BlockSpec((g, cap, LANES), lambda i, e: (e, i, 0)),
                   pl.BlockSpec((1, ne, t), lambda i, e: (i, 0, 0))],
        out_shape=[jax.ShapeDtypeStruct((ne, b * cap, d), BF16),
                   jax.ShapeDtypeStruct((ne, b * cap, LANES), F32),
                   jax.ShapeDtypeStruct((b, ne, t), jnp.int32)],
        compiler_params=_cparams(2, 48),
        name="moe_gather",
    )(h2, aff_t)


def _moe_ffn_kernel(x_ref, gate_ref, wg_ref, wu_ref, wd_ref, y_ref, acc_ref):
    half = pl.program_id(2)
    x = x_ref[0]
    hid = _dot(x, wg_ref[0, 0].astype(BF16))
    hid = (hid * _sigmoid(hid)) * _dot(x, wu_ref[0, 0].astype(BF16))
    part = _dot(hid.astype(BF16), wd_ref[0, 0].astype(BF16))

    @pl.when(half == 0)
    def _():
        acc_ref[...] = part

    @pl.when(half == 1)
    def _():
        y_ref[0] = ((acc_ref[...] + part) * gate_ref[0, :, 0:1]).astype(BF16)


def _moe_ffn(xin, gates, w_gate, w_up, w_down, layer):
    ne, m, d = xin.shape
    ff = w_gate.shape[3]
    tm = MOE_FFN_TILE
    hf = ff // 2
    return pl.pallas_call(
        _moe_ffn_kernel,
        grid=(ne, m // tm, 2),
        in_specs=[pl.BlockSpec((1, tm, d), lambda e, j, h: (e, j, 0)),
                  pl.BlockSpec((1, tm, LANES), lambda e, j, h: (e, j, 0)),
                  pl.BlockSpec((1, 1, d, hf), lambda e, j, h: (layer, e, 0, h)),
                  pl.BlockSpec((1, 1, d, hf), lambda e, j, h: (layer, e, 0, h)),
                  pl.BlockSpec((1, 1, hf, d), lambda e, j, h: (layer, e, h, 0))],
        out_specs=pl.BlockSpec((1, tm, d), lambda e, j, h: (e, j, 0)),
        out_shape=jax.ShapeDtypeStruct((ne, m, d), BF16),
        scratch_shapes=[pltpu.VMEM((tm, d), F32)],
        compiler_params=_cparams(3, 56),
        name="moe_ffn",
    )(xin, gates, w_gate, w_up, w_down)


def _moe_scatter_kernel(slot_ref, y_ref, o_ref, *, cap):
    eg = pl.program_id(1)
    t = o_ref.shape[1]
    ps = [jnp.where(_pick_rows(slot_ref, eg * MOE_EXPERTS_PER_STEP + i, cap, t), 1.0, 0.0).astype(BF16)
          for i in range(MOE_EXPERTS_PER_STEP)]
    p = jnp.concatenate(ps, axis=0)
    y = y_ref[...].reshape(MOE_EXPERTS_PER_STEP * cap, y_ref.shape[2])
    contrib = _dot_tn(p, y)

    @pl.when(eg == 0)
    def _():
        o_ref[0] = contrib

    @pl.when(eg > 0)
    def _():
        o_ref[0] += contrib


def _moe_scatter(slot, y, b):
    ne, m, d = y.shape
    t = slot.shape[2]
    cap = m // b
    g = MOE_EXPERTS_PER_STEP
    return pl.pallas_call(
        functools.partial(_moe_scatter_kernel, cap=cap),
        grid=(b, ne // g),
        in_specs=[pl.BlockSpec((1, ne, t), lambda i, e: (i, 0, 0)),
                  pl.BlockSpec((g, cap, d), lambda i, e: (e, i, 0))],
        out_specs=pl.BlockSpec((1, t, d), lambda i, e: (i, 0, 0)),
        out_shape=jax.ShapeDtypeStruct((b, t, d), F32),
        compiler_params=_cparams(2, 56),
        name="moe_scatter",
    )(slot, y)


def _moe(h2, aff_t, w_gate, w_up, w_down, layer):
    xin, gates, slot = _moe_gather(h2, aff_t)
    y = _moe_ffn(xin, gates, w_gate, w_up, w_down, layer)
    return _moe_scatter(slot, y, h2.shape[0])


def _moe_residual(x_ref, moe_ref, mod_ref, g_ref):
    return x_ref[0] + mod_ref[0, 5:6, :] * _rms(moe_ref[0], g_ref[3:4, :])


def _mlp_in_kernel(xa_ref, moe_ref, modp_ref, gp_ref, mod_ref, g_ref, w_ref, vg_ref, x1_ref, u_ref, v_ref):
    x1 = _moe_residual(xa_ref, moe_ref, modp_ref, gp_ref)
    x1_ref[0] = x1
    h = _rms(x1, g_ref[0:1, :]) * (1.0 + mod_ref[0, 1:2, :]) + mod_ref[0, 0:1, :]
    r = _dot(h.astype(BF16), w_ref[...])
    ge = 0.5 * r * (1.0 + jnp.tanh(0.7978845608028654 * (r + 0.044715 * (r * r * r))))
    half = ge.shape[1] // 2
    u_ref[0] = ge[:, :half].astype(BF16)
    v = ge[:, half:]
    vc = v - jnp.mean(v, axis=-1, keepdims=True)
    vn = vc * lax.rsqrt(jnp.mean(vc * vc, axis=-1, keepdims=True) + EPS) * vg_ref[...]
    v_ref[0] = vn.astype(BF16)


def _mlp_in(xa, moe, mods_prev, g_prev, mods, g, w, vg):
    b, l, d = xa.shape
    tm = TOKEN_TILE
    half = w.shape[1] // 2
    tok = lambda i, j: (i, j, 0)
    return pl.pallas_call(
        _mlp_in_kernel,
        grid=(b, l // tm),
        in_specs=[
            pl.BlockSpec((1, tm, d), tok),
            pl.BlockSpec((1, tm, d), tok),
            pl.BlockSpec((1, SUBLANES, d), lambda i, j: (i, 0, 0)),
            pl.BlockSpec((SUBLANES, d), lambda i, j: (0, 0)),
            pl.BlockSpec((1, SUBLANES, d), lambda i, j: (i, 0, 0)),
            pl.BlockSpec((SUBLANES, d), lambda i, j: (0, 0)),
            pl.BlockSpec(w.shape, lambda i, j: (0, 0), pipeline_mode=pl.Buffered(1)),
            pl.BlockSpec((1, half), lambda i, j: (0, 0)),
        ],
        out_specs=[pl.BlockSpec((1, tm, d), tok),
                   pl.BlockSpec((1, tm, half), tok),
                   pl.BlockSpec((1, tm, half), tok)],
        out_shape=[jax.ShapeDtypeStruct((b, l, d), F32),
                   jax.ShapeDtypeStruct((b, l, half), BF16),
                   jax.ShapeDtypeStruct((b, l, half), BF16)],
        compiler_params=_cparams(2, 48),
        name="mlp_in_proj",
    )(xa, moe, mods_prev, g_prev, mods, g, w, vg)


def _mlp_mix_kernel(u_ref, v_ref, ws_ref, bs_ref, w_ref, o_ref, vf_ref, sf_ref, us_ref, *, rows):
    gp = pl.program_id(1)
    t = u_ref.shape[1]
    n_chunks = t // MLP_CHUNK
    per_step = ws_ref.shape[0]
    cols_per_chunk = MLP_CHUNK // rows

    @pl.when(gp < MLP_ROW_GROUPS // per_step)
    def _():
        for i in range(per_step):
            cs = slice(i * LANES, (i + 1) * LANES)
            for c in range(n_chunks):
                rs = slice(c * MLP_CHUNK, (c + 1) * MLP_CHUNK)
                s = _dot(ws_ref[i], v_ref[0, rs, cs]) + bs_ref[i]
                us_ref[rs, cs] = (u_ref[0, rs, cs].astype(F32) * s).astype(BF16)

    @pl.when(gp >= MLP_ROW_GROUPS // per_step)
    def _():
        for i in range(per_step):
            cs = slice(i * LANES, (i + 1) * LANES)
            vf_ref[...] = v_ref[0, :, cs].astype(F32)
            for k in range(n_chunks):
                xk = jnp.concatenate(
                    [vf_ref[pl.ds(k * cols_per_chunk + wl, rows, stride=GRID_W), :] for wl in range(cols_per_chunk)],
                    axis=0)
                s = _dot(ws_ref[i], xk.astype(BF16)) + bs_ref[i]
                for wl in range(cols_per_chunk):
                    sf_ref[pl.ds(k * cols_per_chunk + wl, rows, stride=GRID_W), :] = s[wl * rows:(wl + 1) * rows, :]
            us_ref[:, cs] = (u_ref[0, :, cs].astype(F32) * sf_ref[...]).astype(BF16)

    part = _dot(us_ref[...], w_ref[...])

    @pl.when(gp == 0)
    def _():
        o_ref[0] = part

    @pl.when(gp > 0)
    def _():
        o_ref[0] += part


def _mlp_mix(u, v, ws, bsb, w):
    b, l, inner = u.shape
    d = w.shape[1]
    per_step = 2
    steps = MLP_GROUPS // per_step
    kc = per_step * LANES
    rows = l // GRID_W
    return pl.pallas_call(
        functools.partial(_mlp_mix_kernel, rows=rows),
        grid=(b, steps),
        in_specs=[
            pl.BlockSpec((1, l, kc), lambda i, g: (i, 0, g)),
            pl.BlockSpec((1, l, kc), lambda i, g: (i, 0, g)),
            pl.BlockSpec((per_step, MLP_CHUNK, MLP_CHUNK), lambda i, g: (g, 0, 0)),
            pl.BlockSpec((per_step, MLP_CHUNK, LANES), lambda i, g: (g, 0, 0)),
            pl.BlockSpec((kc, d), lambda i, g: (g, 0)),
        ],
        out_specs=pl.BlockSpec((1, l, d), lambda i, g: (i, 0, 0)),
        out_shape=jax.ShapeDtypeStruct((b, l, d), F32),
        scratch_shapes=[pltpu.VMEM((l, LANES), F32), pltpu.VMEM((l, LANES), F32), pltpu.VMEM((l, kc), BF16)],
        compiler_params=_cparams(2, 48),
        name="mlp_mix",
    )(u, v, ws, bsb, w)


def _final_kernel(x_ref, moe_ref, mod_ref, g_ref, o_ref):
    o_ref[0] = _moe_residual(x_ref, moe_ref, mod_ref, g_ref)


def _final(x, moe, mods, g):
    b, l, d = x.shape
    tm = 2 * TOKEN_TILE
    tok = lambda i, j: (i, j, 0)
    return pl.pallas_call(
        _final_kernel,
        grid=(b, l // tm),
        in_specs=[pl.BlockSpec((1, tm, d), tok), pl.BlockSpec((1, tm, d), tok),
                  pl.BlockSpec((1, SUBLANES, d), lambda i, j: (i, 0, 0)),
                  pl.BlockSpec((SUBLANES, d), lambda i, j: (0, 0))],
        out_specs=pl.BlockSpec((1, tm, d), tok),
        out_shape=jax.ShapeDtypeStruct((b, l, d), F32),
        compiler_params=_cparams(2, 32),
        name="moe_residual",
    )(x, moe, mods, g)


def _pad_rows(a, rows):
    return jnp.pad(a, ((0, rows - a.shape[0]),) + ((0, 0),) * (a.ndim - 1))


def kernel(x, c, ctx, c_ctx, mod_w, mod_b, norm_g, ssd_in_w, ssd_conv_w, ssd_conv_b, ssd_dt_bias, ssd_a_log,
           ssd_d, ssd_norm_g, ssd_out_w, mlp_in_w, mlp_v_g, mlp_ws, mlp_bs, mlp_out_w, router_w, exp_w_gate,
           exp_w_up, exp_w_down):
    b, l, d = x.shape
    inner = SSD_HEADS * SSD_HEADDIM
    n_chunks = l // SSD_CHUNK
    n_ctx_chunks = ctx.shape[1] // SSD_CHUNK

    crows = _pad_rows(jnp.concatenate([c, c_ctx[None, :]], axis=0), -(-(b + 1) // SUBLANES) * SUBLANES)
    mod = _modulation(crows, mod_w, mod_b)
    mods_lat = [jnp.pad(mod[i, :b].reshape(b, N_MOD, d), ((0, 0), (0, SUBLANES - N_MOD), (0, 0)))
                for i in range(2)]
    mods_ctx = jnp.pad(mod[0, b].reshape(1, N_MOD, d), ((0, 0), (0, SUBLANES - N_MOD), (0, 0)))
    mods01 = jnp.stack([jnp.broadcast_to(mods_ctx, (b, SUBLANES, d)), mods_lat[0]], axis=1)
    gains = [_pad_rows(norm_g[i], SUBLANES) for i in range(2)]
    rw2 = []
    for i in range(2):
        rw_t = router_w[i].T
        rw_hi = rw_t.astype(BF16)
        rw2.append(jnp.concatenate([rw_hi, (rw_t - rw_hi.astype(F32)).astype(BF16)], axis=0))

    in_w = ssd_in_w[0]
    conv_dim = ssd_conv_w.shape[2]
    w_in = jnp.pad(in_w, ((0, 0), (0, LANES - 2 * SSD_HEADS))).astype(BF16)
    dt_bias = jnp.pad(ssd_dt_bias[0].reshape(1, 2 * SSD_HEADS), ((0, 0), (0, LANES - 2 * SSD_HEADS)))
    z, xs_all, bc_all, dt_all = _ssd_in_proj(
        ctx, x, mods01, gains[0], w_in, _pad_rows(ssd_conv_w[0], SUBLANES), ssd_conv_b[0].reshape(1, conv_dim),
        dt_bias)

    alog = _pad_rows(jnp.stack([jnp.pad(ssd_a_log[0, 0], (0, LANES - SSD_HEADS)),
                                jnp.pad(ssd_a_log[0, 1], (SSD_HEADS, LANES - 2 * SSD_HEADS))]), SUBLANES)
    head_of_col = jnp.arange(inner, dtype=jnp.int32) // SSD_HEADDIM
    rows128 = jnp.arange(LANES, dtype=jnp.int32)[:, None]
    expand = jnp.stack([(rows128 == head_of_col[None, :] + SSD_HEADS * dd) for dd in range(2)]).astype(BF16)
    dskip = jnp.repeat(ssd_d[0], SSD_HEADDIM).reshape(1, inner)
    y = _ssd_scan(xs_all, bc_all, dt_all, alog, expand, dskip, n_ctx_chunks, n_chunks)

    xa, h2, aff_t = _ssd_out(y, z, x, mods_lat[0], gains[0], ssd_norm_g[0].reshape(1, inner),
                             ssd_out_w[0].astype(BF16), rw2[0])
    moe0 = _moe(h2, aff_t, exp_w_gate, exp_w_up, exp_w_down, 0)

    x1, u, v = _mlp_in(xa, moe0, mods_lat[0], gains[0], mods_lat[1], gains[1], mlp_in_w[0].astype(BF16),
                       mlp_v_g[0].reshape(1, -1))
    bsb = jnp.broadcast_to(mlp_bs[0][:, :, None], (MLP_GROUPS, MLP_CHUNK, LANES))
    o1 = _mlp_mix(u, v, mlp_ws[0].astype(BF16), bsb, mlp_out_w[0].astype(BF16))
    xb, h2b, aff_tb = _epilogue(o1, x1, mods_lat[1], gains[1], rw2[1])
    moe1 = _moe(h2b, aff_tb, exp_w_gate, exp_w_up, exp_w_down, 1)
    return _final(xb, moe1, mods_lat[1], gains[1])
```

```python
import functools

import jax
import jax.numpy as jnp
from jax import lax
from jax.experimental import pallas as pl
from jax.experimental.pallas import tpu as pltpu

F32 = jnp.float32
BF16 = jnp.bfloat16
HIGHEST = lax.Precision.HIGHEST
EPS = 1e-6

LANES = 128
SUBLANES = 8
MIB = 1024 * 1024

N_MOD = 6
GRID_W = 64
SSD_HEADDIM = 64
SSD_HEADS = 32
SSD_GROUPS = 4
SSD_HPG = SSD_HEADS // SSD_GROUPS
SSD_STATE = 128
SSD_CONV = 5
SSD_CHUNK = 128
MLP_CHUNK = 128
MLP_GROUPS = 16
MLP_ROW_GROUPS = 8
N_EXPERTS = 16
CAPACITY_FACTOR = 2

CONV_HALO = SUBLANES
TOKEN_TILE = 256
CONV_COL_BLOCK = 1024


def _cparams(n_axes, vmem_mib):
    return pltpu.CompilerParams(dimension_semantics=("arbitrary",) * n_axes,
                                vmem_limit_bytes=vmem_mib * MIB)


def _sigmoid(x):
    return 0.5 * (1.0 + jnp.tanh(0.5 * x))


def _rms(x, g):
    return x * lax.rsqrt(jnp.mean(x * x, axis=-1, keepdims=True) + EPS) * g


def _dot(a, b):
    return jnp.dot(a, b, preferred_element_type=F32)


def _dot_nt(a, b, precision=None):
    return lax.dot_general(a, b, (((1,), (1,)), ((), ())), preferred_element_type=F32, precision=precision)


def _dot_tn(a, b):
    return lax.dot_general(a, b, (((0,), (0,)), ((), ())), preferred_element_type=F32)


def _mod_kernel(c_ref, w_ref, b_ref, o_ref):
    c = c_ref[...]
    s = c * _sigmoid(c)
    o_ref[0] = jnp.dot(s, w_ref[0], preferred_element_type=F32, precision=HIGHEST) + b_ref[0]


def _modulation(crows, mod_w, mod_b):
    depth, d, n = mod_w.shape
    rows = crows.shape[0]
    tn = 1536
    return pl.pallas_call(
        _mod_kernel,
        grid=(depth, n // tn),
        in_specs=[pl.BlockSpec((rows, d), lambda i, j: (0, 0)),
                  pl.BlockSpec((1, d, tn), lambda i, j: (i, 0, j)),
                  pl.BlockSpec((1, 1, tn), lambda i, j: (i, 0, j))],
        out_specs=pl.BlockSpec((1, rows, tn), lambda i, j: (i, 0, j)),
        out_shape=jax.ShapeDtypeStruct((depth, rows, n), F32),
        compiler_params=_cparams(2, 40),
        name="modulation",
    )(crows, mod_w, mod_b.reshape(depth, 1, n))


def _ssd_in_kernel(ctx_ref, x_ref, xp_ref, xn_ref, mod_ref, g_ref, w_ref, cw_ref, cb_ref, dtb_ref,
                   z_ref, xs_ref, bc_ref, dt_ref, *, n_tiles, inner, conv_dim):
    j = pl.program_id(1)
    tm = x_ref.shape[1]
    halo = xp_ref.shape[1]
    rows = tm + 2 * halo
    xc = jnp.where(j == 0, ctx_ref[0], x_ref[0])
    xa = jnp.concatenate([xp_ref[0], xc, xn_ref[0]], axis=0)
    h = (_rms(xa, g_ref[0:1, :]) * (1.0 + mod_ref[0, 0, 1:2, :]) + mod_ref[0, 0, 0:1, :]).astype(BF16)
    hc = h[halo:halo + tm, :]
    z_ref[0] = _dot(hc, w_ref[:, :inner]).astype(BF16)
    dtr = _dot(hc, w_ref[:, inner + conv_dim:]) + dtb_ref[...]
    dt_ref[0] = jnp.maximum(dtr, 0.0) + jnp.log1p(jnp.exp(-jnp.abs(dtr)))
    row = lax.broadcasted_iota(jnp.int32, (rows, 1), 0)
    drop_top = jnp.logical_and(row < halo, j <= 1)
    drop_bot = jnp.logical_and(row >= halo + tm, jnp.logical_or(j == 0, j == n_tiles - 1))
    keep = jnp.where(jnp.logical_or(drop_top, drop_bot), 0.0, 1.0)
    pad = (SSD_CONV - 1) // 2
    cblk = CONV_COL_BLOCK
    for cbi in range(conv_dim // cblk):
        c0 = cbi * cblk
        u = _dot(h, w_ref[:, inner + c0:inner + c0 + cblk]) * keep
        acc = cb_ref[:, c0:c0 + cblk] + cw_ref[pad:pad + 1, c0:c0 + cblk] * u[halo:halo + tm, :]
        for k in range(SSD_CONV):
            if k != pad:
                shifted = pltpu.roll(u, (pad - k) % rows, axis=0)
                acc = acc + cw_ref[k:k + 1, c0:c0 + cblk] * shifted[halo:halo + tm, :]
        act = (acc * _sigmoid(acc)).astype(BF16)
        if c0 < inner:
            xs_ref[0, :, c0:c0 + cblk] = act
        else:
            bc_ref[0, :, c0 - inner:c0 - inner + cblk] = act


def _ssd_in_proj(ctx, x, mods01, g, w, conv_w, conv_b, dt_bias):
    b, l, d = x.shape
    lc = ctx.shape[1]
    tm = TOKEN_TILE
    assert lc == tm and l % tm == 0
    n_tiles = 1 + l // tm
    inner = SSD_HEADS * SSD_HEADDIM
    conv_dim = conv_w.shape[1]
    bc_dim = conv_dim - inner
    halo = CONV_HALO
    blocks_per_tile = tm // halo
    last_halo_block = l // halo - 1
    lt = lc + l
    kern = functools.partial(_ssd_in_kernel, n_tiles=n_tiles, inner=inner, conv_dim=conv_dim)
    return pl.pallas_call(
        kern,
        grid=(b, n_tiles),
        in_specs=[
            pl.BlockSpec((1, lc, d), lambda i, j: (i, 0, 0)),
            pl.BlockSpec((1, tm, d), lambda i, j: (i, jnp.maximum(j - 1, 0), 0)),
            pl.BlockSpec((1, halo, d), lambda i, j: (i, jnp.maximum((j - 1) * blocks_per_tile - 1, 0), 0)),
            pl.BlockSpec((1, halo, d), lambda i, j: (i, jnp.minimum(j * blocks_per_tile, last_halo_block), 0)),
            pl.BlockSpec((1, 1, SUBLANES, d), lambda i, j: (i, jnp.minimum(j, 1), 0, 0)),
            pl.BlockSpec((SUBLANES, d), lambda i, j: (0, 0)),
            pl.BlockSpec(w.shape, lambda i, j: (0, 0), pipeline_mode=pl.Buffered(1)),
            pl.BlockSpec((SUBLANES, conv_dim), lambda i, j: (0, 0)),
            pl.BlockSpec((1, conv_dim), lambda i, j: (0, 0)),
            pl.BlockSpec((1, LANES), lambda i, j: (0, 0)),
        ],
        out_specs=[
            pl.BlockSpec((1, tm, inner), lambda i, j: (i, jnp.maximum(j - 1, 0), 0)),
            pl.BlockSpec((1, tm, inner), lambda i, j: (i, j, 0)),
            pl.BlockSpec((1, tm, bc_dim), lambda i, j: (i, j, 0)),
            pl.BlockSpec((1, tm, LANES), lambda i, j: (i, j, 0)),
        ],
        out_shape=[
            jax.ShapeDtypeStruct((b, l, inner), BF16),
            jax.ShapeDtypeStruct((b, lt, inner), BF16),
            jax.ShapeDtypeStruct((b, lt, bc_dim), BF16),
            jax.ShapeDtypeStruct((b, lt, LANES), F32),
        ],
        compiler_params=_cparams(2, 56),
        name="ssd_in_proj",
    )(ctx, x, x, x, mods01, g, w, conv_w, conv_b, dt_bias)


SSD_CHUNKS_PER_STEP = 2
SSD_DECAY_ROWS = 16


def _split3(x):
    hi = x.astype(BF16)
    r1 = x - hi.astype(F32)
    mid = r1.astype(BF16)
    lo = (r1 - mid.astype(F32)).astype(BF16)
    return hi, mid, lo


def _ssd_step(rev, is_lat, lblk, xs_ref, bc_ref, dt_ref, alog_ref, e_ref, dskip_ref, y_ref, st_ref, ybuf_ref):
    c = SSD_CHUNK
    nck = SSD_CHUNKS_PER_STEP
    dcol = SSD_HEADS * int(rev)
    gw = SSD_HPG * SSD_HEADDIM
    bc_off = SSD_GROUPS * SSD_STATE
    a_neg = -jnp.exp(alog_ref[int(rev):int(rev) + 1, :])
    li = lax.broadcasted_iota(jnp.int32, (c, c), 0)
    si = lax.broadcasted_iota(jnp.int32, (c, c), 1)
    causal = (si >= li) if rev else (si <= li)
    tri = jnp.where(causal, 1.0, 0.0).astype(BF16)
    e = e_ref[int(rev)]
    head = lax.broadcasted_iota(jnp.int32, (c, LANES), 1)

    dts = [dt_ref[0, k * c:(k + 1) * c, :] for k in range(nck)]
    pieces = []
    for k in range(nck):
        pieces.extend(_split3(dts[k] * a_neg))
    run = _dot(tri, jnp.concatenate(pieces, axis=1))
    css = [run[:, (3 * k) * LANES:(3 * k + 1) * LANES] + run[:, (3 * k + 1) * LANES:(3 * k + 2) * LANES]
           + run[:, (3 * k + 2) * LANES:(3 * k + 3) * LANES] for k in range(nck)]
    tots = [cs[0:1, :] if rev else cs[c - 1:c, :] for cs in css]

    blocks = []
    for k in range(nck):
        blocks.append((jnp.exp(tots[k] - css[k]) * dts[k]).astype(BF16))
        blocks.append(jnp.exp(css[k]).astype(BF16))
    drow = lax.broadcasted_iota(jnp.int32, (SSD_DECAY_ROWS, LANES), 0)
    for k in range(nck):
        hi, mid, lo = _split3(jnp.broadcast_to(jnp.exp(tots[k]), (SSD_DECAY_ROWS, LANES)))
        rows3 = jnp.where(drow == 0, hi.astype(F32),
                          jnp.where(drow == 1, mid.astype(F32), jnp.where(drow == 2, lo.astype(F32), 0.0)))
        blocks.append(rows3.astype(BF16))
    ex = _dot(jnp.concatenate(blocks, axis=0), e).astype(BF16)
    dec0 = 2 * nck * c
    even_cols = jnp.where(head < SSD_HEADDIM, 1.0, 0.0).astype(BF16)
    odd_cols = jnp.where(head < SSD_HEADDIM, 0.0, 1.0).astype(BF16)

    for k in (range(nck - 1, -1, -1) if rev else range(nck)):
        base = 2 * k * c
        xs = xs_ref[0, k * c:(k + 1) * c, :]
        bc = bc_ref[0, k * c:(k + 1) * c, :]
        xw = xs * ex[base:base + c, :]

        @pl.when(is_lat)
        def _(k=k, base=base, xs=xs, bc=bc):
            lc = lblk * nck + k
            cs = css[k]
            cs_t = cs.T
            dt_t = dts[k].T
            for g in range(SSD_GROUPS):
                bg = bc[:, g * SSD_STATE:(g + 1) * SSD_STATE]
                cg = bc[:, bc_off + g * SSD_STATE:bc_off + (g + 1) * SSD_STATE]
                px = ex[base + c:base + 2 * c, g * gw:(g + 1) * gw].astype(F32)
                y_off = _dot(cg, st_ref[:, g * gw:(g + 1) * gw].astype(BF16)) * px
                cbm = jnp.where(causal, _dot_nt(cg, bg), 0.0)
                for hp in range(SSD_HPG // 2):
                    ms = []
                    for hh in range(2):
                        hd = dcol + g * SSD_HPG + 2 * hp + hh
                        diff = cs[:, hd:hd + 1] - cs_t[hd:hd + 1, :]
                        ms.append(cbm * jnp.exp(jnp.minimum(diff, 0.0)) * dt_t[hd:hd + 1, :])
                    mp = jnp.concatenate(ms, axis=1).astype(BF16)
                    col0 = (g * SSD_HPG + 2 * hp) * SSD_HEADDIM
                    xp = xs[:, col0:col0 + LANES]
                    rhs = jnp.concatenate([xp * even_cols, xp * odd_cols], axis=0)
                    y_pair = _dot(mp, rhs) + y_off[:, 2 * hp * SSD_HEADDIM:2 * hp * SSD_HEADDIM + LANES]
                    if rev:
                        tot_y = (ybuf_ref[lc, :, col0:col0 + LANES] + y_pair
                                 + dskip_ref[:, col0:col0 + LANES] * xp.astype(F32))
                        y_ref[0, k * c:(k + 1) * c, col0:col0 + LANES] = tot_y.astype(BF16)
                    else:
                        ybuf_ref[lc, :, col0:col0 + LANES] = y_pair

        drows = ex[dec0 + k * SSD_DECAY_ROWS:dec0 + (k + 1) * SSD_DECAY_ROWS, :].astype(F32)
        decay = drows[0:1, :] + drows[1:2, :] + drows[2:3, :]
        for g in range(SSD_GROUPS):
            bg = bc[:, g * SSD_STATE:(g + 1) * SSD_STATE]
            upd = _dot_tn(bg, xw[:, g * gw:(g + 1) * gw])
            st_ref[:, g * gw:(g + 1) * gw] = st_ref[:, g * gw:(g + 1) * gw] * decay[:, g * gw:(g + 1) * gw] + upd


def _ssd_scan_kernel(xs_ref, bc_ref, dt_ref, alog_ref, e_ref, dskip_ref, y_ref, st_ref, ybuf_ref, *,
                     n_ctx_blocks, n_blocks):
    d = pl.program_id(1)
    s = pl.program_id(2)

    @pl.when(s == 0)
    def _():
        st_ref[...] = jnp.zeros_like(st_ref)

    for rev in (False, True):
        @pl.when(d == int(rev))
        def _(rev=rev):
            if rev:
                blk = jnp.where(s < n_ctx_blocks, n_ctx_blocks - 1 - s, n_blocks + 2 * n_ctx_blocks - 1 - s)
            else:
                blk = s
            _ssd_step(rev, blk >= n_ctx_blocks, jnp.maximum(blk - n_ctx_blocks, 0), xs_ref, bc_ref, dt_ref,
                      alog_ref, e_ref, dskip_ref, y_ref, st_ref, ybuf_ref)


def _ssd_scan(xs_all, bc_all, dt_all, alog, expand, dskip, n_ctx_chunks, n_chunks):
    b, lt, inner = xs_all.shape
    nck = SSD_CHUNKS_PER_STEP
    rows = nck * SSD_CHUNK
    assert n_ctx_chunks % nck == 0 and n_chunks % nck == 0
    n_ctx_blocks = n_ctx_chunks // nck
    n_blocks = n_chunks // nck
    steps = n_ctx_blocks + n_blocks

    def block_idx(d, s):
        bwd = jnp.where(s < n_ctx_blocks, n_ctx_blocks - 1 - s, steps + n_ctx_blocks - 1 - s)
        return jnp.where(d == 0, s, bwd)

    def out_idx(d, s):
        return jnp.where(jnp.logical_and(d == 1, s >= n_ctx_blocks), steps - 1 - s, n_blocks - 1)

    kern = functools.partial(_ssd_scan_kernel, n_ctx_blocks=n_ctx_blocks, n_blocks=n_blocks)
    return pl.pallas_call(
        kern,
        grid=(b, 2, steps),
        in_specs=[
            pl.BlockSpec((1, rows, inner), lambda i, d, s: (i, block_idx(d, s), 0)),
            pl.BlockSpec((1, rows, bc_all.shape[2]), lambda i, d, s: (i, block_idx(d, s), 0)),
            pl.BlockSpec((1, rows, LANES), lambda i, d, s: (i, block_idx(d, s), 0)),
            pl.BlockSpec((SUBLANES, LANES), lambda i, d, s: (0, 0)),
            pl.BlockSpec((2, LANES, inner), lambda i, d, s: (0, 0, 0)),
            pl.BlockSpec((1, inner), lambda i, d, s: (0, 0)),
        ],
        out_specs=pl.BlockSpec((1, rows, inner), lambda i, d, s: (i, out_idx(d, s), 0)),
        out_shape=jax.ShapeDtypeStruct((b, n_chunks * SSD_CHUNK, inner), BF16),
        scratch_shapes=[pltpu.VMEM((SSD_STATE, inner), F32),
                        pltpu.VMEM((n_chunks, SSD_CHUNK, inner), F32)],
        compiler_params=_cparams(3, 56),
        name="ssd_scan",
    )(xs_all, bc_all, dt_all, alog, expand, dskip)


def _mixer_epilogue(o, x, mod_ref, g_ref, rw_ref):
    xn = x + mod_ref[0, 2:3, :] * _rms(o, g_ref[1:2, :])
    h2 = _rms(xn, g_ref[2:3, :]) * (1.0 + mod_ref[0, 4:5, :]) + mod_ref[0, 3:4, :]
    h_hi = h2.astype(BF16)
    h_lo = (h2 - h_hi.astype(F32)).astype(BF16)
    ne = rw_ref.shape[0] // 2
    both = _dot_nt(rw_ref[...], h_hi)
    lg = both[:ne, :] + both[ne:, :] + _dot_nt(rw_ref[0:ne, :], h_lo)
    ex = jnp.exp(lg - jnp.max(lg, axis=0, keepdims=True))
    return xn, h_hi, ex / jnp.sum(ex, axis=0, keepdims=True)


def _ssd_out_kernel(y_ref, z_ref, x_ref, mod_ref, g_ref, ng_ref, w_ref, rw_ref, xo_ref, h2_ref, aff_ref):
    sub = TOKEN_TILE
    gdim = y_ref.shape[2] // SSD_GROUPS
    for r in range(y_ref.shape[1] // sub):
        rs = slice(r * sub, (r + 1) * sub)
        z = z_ref[0, rs, :].astype(F32)
        yg = y_ref[0, rs, :].astype(F32) * (z * _sigmoid(z))
        parts = [_rms(yg[:, g * gdim:(g + 1) * gdim], ng_ref[:, g * gdim:(g + 1) * gdim]).astype(BF16)
                 for g in range(SSD_GROUPS)]
        o = _dot(jnp.concatenate(parts, axis=1), w_ref[...])
        xn, hb, aff = _mixer_epilogue(o, x_ref[0, rs, :], mod_ref, g_ref, rw_ref)
        xo_ref[0, rs, :] = xn
        h2_ref[0, rs, :] = hb
        aff_ref[0, :, rs] = aff


def _epilogue_outs(b, l, d, tm):
    out_specs = [pl.BlockSpec((1, tm, d), lambda i, j: (i, j, 0)),
                 pl.BlockSpec((1, tm, d), lambda i, j: (i, j, 0)),
                 pl.BlockSpec((1, N_EXPERTS, tm), lambda i, j: (i, 0, j))]
    out_shape = [jax.ShapeDtypeStruct((b, l, d), F32),
                 jax.ShapeDtypeStruct((b, l, d), BF16),
                 jax.ShapeDtypeStruct((b, N_EXPERTS, l), F32)]
    return out_specs, out_shape


def _ssd_out(y, z, x, mods, g, ng, w, rw2):
    b, l, d = x.shape
    inner = y.shape[2]
    tm = 2 * TOKEN_TILE
    out_specs, out_shape = _epilogue_outs(b, l, d, tm)
    return pl.pallas_call(
        _ssd_out_kernel,
        grid=(b, l // tm),
        in_specs=[
            pl.BlockSpec((1, tm, inner), lambda i, j: (i, j, 0)),
            pl.BlockSpec((1, tm, inner), lambda i, j: (i, j, 0)),
            pl.BlockSpec((1, tm, d), lambda i, j: (i, j, 0)),
            pl.BlockSpec((1, SUBLANES, d), lambda i, j: (i, 0, 0)),
            pl.BlockSpec((SUBLANES, d), lambda i, j: (0, 0)),
            pl.BlockSpec((1, inner), lambda i, j: (0, 0)),
            pl.BlockSpec(w.shape, lambda i, j: (0, 0)),
            pl.BlockSpec(rw2.shape, lambda i, j: (0, 0)),
        ],
        out_specs=out_specs,
        out_shape=out_shape,
        compiler_params=_cparams(2, 48),
        name="ssd_out",
    )(y, z, x, mods, g, ng, w, rw2)


def _epilogue_kernel(o_ref, x_ref, mod_ref, g_ref, rw_ref, xo_ref, h2_ref, aff_ref):
    sub = TOKEN_TILE
    for r in range(o_ref.shape[1] // sub):
        rs = slice(r * sub, (r + 1) * sub)
        xn, hb, aff = _mixer_epilogue(o_ref[0, rs, :], x_ref[0, rs, :], mod_ref, g_ref, rw_ref)
        xo_ref[0, rs, :] = xn
        h2_ref[0, rs, :] = hb
        aff_ref[0, :, rs] = aff


def _epilogue(o, x, mods, g, rw_t):
    b, l, d = x.shape
    tm = 2 * TOKEN_TILE
    out_specs, out_shape = _epilogue_outs(b, l, d, tm)
    return pl.pallas_call(
        _epilogue_kernel,
        grid=(b, l // tm),
        in_specs=[
            pl.BlockSpec((1, tm, d), lambda i, j: (i, j, 0)),
            pl.BlockSpec((1, tm, d), lambda i, j: (i, j, 0)),
            pl.BlockSpec((1, SUBLANES, d), lambda i, j: (i, 0, 0)),
            pl.BlockSpec((SUBLANES, d), lambda i, j: (0, 0)),
            pl.BlockSpec(rw_t.shape, lambda i, j: (0, 0)),
        ],
        out_specs=out_specs,
        out_shape=out_shape,
        compiler_params=_cparams(2, 40),
        name="mixer_epilogue",
    )(o, x, mods, g, rw_t)


def _lane_prefix_exclusive(m01, upper):
    e, t = m01.shape
    carry = jnp.zeros((e, 1), F32)
    outs = []
    for k in range(t // LANES):
        tile = m01[:, k * LANES:(k + 1) * LANES]
        incl = _dot(tile.astype(BF16), upper)
        outs.append(incl - tile + carry)
        carry = carry + incl[:, LANES - 1:LANES]
    return jnp.concatenate(outs, axis=1)


def _route(aff, cap):
    e, t = aff.shape
    key = pltpu.bitcast(aff, jnp.int32)

    def body(i, thr):
        cand = jnp.bitwise_or(thr, jnp.left_shift(jnp.int32(1), 30 - i))
        cnt = jnp.sum(jnp.where(key >= cand, 1.0, 0.0), axis=1, keepdims=True)
        return jnp.where(cnt >= cap, cand, thr)

    thr = lax.fori_loop(0, 31, body, jnp.zeros((e, 1), jnp.int32))
    gt = jnp.where(key > thr, 1.0, 0.0)
    eq = jnp.where(key == thr, 1.0, 0.0)
    need = cap - jnp.sum(gt, axis=1, keepdims=True)
    r = lax.broadcasted_iota(jnp.int32, (LANES, LANES), 0)
    cidx = lax.broadcasted_iota(jnp.int32, (LANES, LANES), 1)
    upper = jnp.where(r <= cidx, 1.0, 0.0).astype(BF16)
    eq_rank = _lane_prefix_exclusive(eq, upper)
    sel = gt + eq * jnp.where(eq_rank < need, 1.0, 0.0)
    pos = _lane_prefix_exclusive(sel, upper)
    return jnp.where(sel > 0.5, pos, -1.0).astype(jnp.int32)


MOE_EXPERTS_PER_STEP = 4
MOE_FFN_TILE = 1024


def _pick_rows(slot_ref, e, cap, t):
    srow = slot_ref[0, pl.ds(e, 1), :]
    return lax.broadcasted_iota(jnp.int32, (cap, t), 0) == srow


def _moe_gather_kernel(h_ref, aff_ref, xin_ref, gate_ref, slot_ref, *, cap):
    eg = pl.program_id(1)

    @pl.when(eg == 0)
    def _():
        slot_ref[0] = _route(aff_ref[0], cap)

    t = h_ref.shape[1]
    ps = []
    for i in range(MOE_EXPERTS_PER_STEP):
        e = eg * MOE_EXPERTS_PER_STEP + i
        pick = _pick_rows(slot_ref, e, cap, t)
        arow = aff_ref[0, pl.ds(e, 1), :]
        gate = jnp.sum(jnp.where(pick, arow, 0.0), axis=1, keepdims=True)
        gate_ref[i] = jnp.broadcast_to(gate, (cap, LANES))
        ps.append(jnp.where(pick, 1.0, 0.0).astype(BF16))
    xin = _dot(jnp.concatenate(ps, axis=0), h_ref[0])
    for i in range(MOE_EXPERTS_PER_STEP):
        xin_ref[i] = xin[i * cap:(i + 1) * cap, :].astype(BF16)


def _moe_gather(h2, aff_t):
    b, t, d = h2.shape
    ne = aff_t.shape[1]
    cap = (CAPACITY_FACTOR * t) // ne
    g = MOE_EXPERTS_PER_STEP
    return pl.pallas_call(
        functools.partial(_moe_gather_kernel, cap=cap),
        grid=(b, ne // g),
        in_specs=[pl.BlockSpec((1, t, d), lambda i, e: (i, 0, 0)),
                  pl.BlockSpec((1, ne, t), lambda i, e: (i, 0, 0))],
        out_specs=[pl.BlockSpec((g, cap, d), lambda i, e: (e, i, 0)),
                   pl.BlockSpec((g, cap, LANES), lambda i, e: (e, i, 0)),
                   pl.BlockSpec((1, ne, t), lambda i, e: (i, 0, 0))],
        out_shape=[jax.ShapeDtypeStruct((ne, b * cap, d), BF16),
                   jax.ShapeDtypeStruct((ne, b * cap, LANES), F32),
                   jax.ShapeDtypeStruct((b, ne, t), jnp.int32)],
        compiler_params=_cparams(2, 48),
        name="moe_gather",
    )(h2, aff_t)


def _moe_ffn_kernel(x_ref, gate_ref, wg_ref, wu_ref, wd_ref, y_ref, acc_ref):
    half = pl.program_id(2)
    x = x_ref[0]
    hid = _dot(x, wg_ref[0, 0].astype(BF16))
    hid = (hid * _sigmoid(hid)) * _dot(x, wu_ref[0, 0].astype(BF16))
    part = _dot(hid.astype(BF16), wd_ref[0, 0].astype(BF16))

    @pl.when(half == 0)
    def _():
        acc_ref[...] = part

    @pl.when(half == 1)
    def _():
        y_ref[0] = ((acc_ref[...] + part) * gate_ref[0, :, 0:1]).astype(BF16)


def _moe_ffn(xin, gates, w_gate, w_up, w_down, layer):
    ne, m, d = xin.shape
    ff = w_gate.shape[3]
    tm = MOE_FFN_TILE
    hf = ff // 2
    return pl.pallas_call(
        _moe_ffn_kernel,
        grid=(ne, m // tm, 2),
        in_specs=[pl.BlockSpec((1, tm, d), lambda e, j, h: (e, j, 0)),
                  pl.BlockSpec((1, tm, LANES), lambda e, j, h: (e, j, 0)),
                  pl.BlockSpec((1, 1, d, hf), lambda e, j, h: (layer, e, 0, h)),
                  pl.BlockSpec((1, 1, d, hf), lambda e, j, h: (layer, e, 0, h)),
                  pl.BlockSpec((1, 1, hf, d), lambda e, j, h: (layer, e, h, 0))],
        out_specs=pl.BlockSpec((1, tm, d), lambda e, j, h: (e, j, 0)),
        out_shape=jax.ShapeDtypeStruct((ne, m, d), BF16),
        scratch_shapes=[pltpu.VMEM((tm, d), F32)],
        compiler_params=_cparams(3, 56),
        name="moe_ffn",
    )(xin, gates, w_gate, w_up, w_down)


def _moe_scatter_kernel(slot_ref, y_ref, x_ref, mod_ref, g_ref, o_ref, *, cap):
    eg = pl.program_id(1)
    t = o_ref.shape[1]
    ps = [jnp.where(_pick_rows(slot_ref, eg * MOE_EXPERTS_PER_STEP + i, cap, t), 1.0, 0.0).astype(BF16)
          for i in range(MOE_EXPERTS_PER_STEP)]
    p = jnp.concatenate(ps, axis=0)
    y = y_ref[...].reshape(MOE_EXPERTS_PER_STEP * cap, y_ref.shape[2])
    cw = 2 * LANES
    n_cb = y.shape[1] // cw

    @pl.when(eg == 0)
    def _():
        for cb in range(n_cb):
            o_ref[0, :, cb * cw:(cb + 1) * cw] = _dot_tn(p, y[:, cb * cw:(cb + 1) * cw])

    @pl.when(eg > 0)
    def _():
        for cb in range(n_cb):
            o_ref[0, :, cb * cw:(cb + 1) * cw] += _dot_tn(p, y[:, cb * cw:(cb + 1) * cw])

    @pl.when(eg == pl.num_programs(1) - 1)
    def _():
        sub = TOKEN_TILE
        for r in range(t // sub):
            rs = slice(r * sub, (r + 1) * sub)
            o_ref[0, rs, :] = x_ref[0, rs, :] + mod_ref[0, 5:6, :] * _rms(o_ref[0, rs, :], g_ref[3:4, :])


def _moe_scatter(slot, y, x, mods, g):
    ne, m, d = y.shape
    b, _, t = slot.shape
    cap = m // b
    grp = MOE_EXPERTS_PER_STEP
    return pl.pallas_call(
        functools.partial(_moe_scatter_kernel, cap=cap),
        grid=(b, ne // grp),
        in_specs=[pl.BlockSpec((1, ne, t), lambda i, e: (i, 0, 0)),
                  pl.BlockSpec((grp, cap, d), lambda i, e: (e, i, 0)),
                  pl.BlockSpec((1, t, d), lambda i, e: (i, 0, 0), pipeline_mode=pl.Buffered(1)),
                  pl.BlockSpec((1, SUBLANES, d), lambda i, e: (i, 0, 0)),
                  pl.BlockSpec((SUBLANES, d), lambda i, e: (0, 0))],
        out_specs=pl.BlockSpec((1, t, d), lambda i, e: (i, 0, 0)),
        out_shape=jax.ShapeDtypeStruct((b, t, d), F32),
        compiler_params=_cparams(2, 56),
        name="moe_scatter",
    )(slot, y, x, mods, g)


def _moe(h2, aff_t, w_gate, w_up, w_down, layer, x, mods, g):
    xin, gates, slot = _moe_gather(h2, aff_t)
    y = _moe_ffn(xin, gates, w_gate, w_up, w_down, layer)
    return _moe_scatter(slot, y, x, mods, g)


def _mlp_in_kernel(x_ref, mod_ref, g_ref, w_ref, vg_ref, u_ref, v_ref):
    h = _rms(x_ref[0], g_ref[0:1, :]) * (1.0 + mod_ref[0, 1:2, :]) + mod_ref[0, 0:1, :]
    r = _dot(h.astype(BF16), w_ref[...])
    ge = 0.5 * r * (1.0 + jnp.tanh(0.7978845608028654 * (r + 0.044715 * (r * r * r))))
    half = ge.shape[1] // 2
    u_ref[0] = ge[:, :half].astype(BF16)
    v = ge[:, half:]
    vc = v - jnp.mean(v, axis=-1, keepdims=True)
    vn = vc * lax.rsqrt(jnp.mean(vc * vc, axis=-1, keepdims=True) + EPS) * vg_ref[...]
    v_ref[0] = vn.astype(BF16)


def _mlp_in(x, mods, g, w, vg):
    b, l, d = x.shape
    tm = TOKEN_TILE
    half = w.shape[1] // 2
    tok = lambda i, j: (i, j, 0)
    return pl.pallas_call(
        _mlp_in_kernel,
        grid=(b, l // tm),
        in_specs=[
            pl.BlockSpec((1, tm, d), tok),
            pl.BlockSpec((1, SUBLANES, d), lambda i, j: (i, 0, 0)),
            pl.BlockSpec((SUBLANES, d), lambda i, j: (0, 0)),
            pl.BlockSpec(w.shape, lambda i, j: (0, 0), pipeline_mode=pl.Buffered(1)),
            pl.BlockSpec((1, half), lambda i, j: (0, 0)),
        ],
        out_specs=[pl.BlockSpec((1, tm, half), tok),
                   pl.BlockSpec((1, tm, half), tok)],
        out_shape=[jax.ShapeDtypeStruct((b, l, half), BF16),
                   jax.ShapeDtypeStruct((b, l, half), BF16)],
        compiler_params=_cparams(2, 48),
        name="mlp_in_proj",
    )(x, mods, g, w, vg)


MLP_GROUPS_PER_STEP = 4
MLP_COL_PITCH = GRID_W + SUBLANES


def _mlp_mix_kernel(u_ref, v_ref, ws_ref, bs_ref, w_ref, o_ref, vf_ref, sf_ref, us_ref, *, rows):
    gp = pl.program_id(1)
    t = u_ref.shape[1]
    n_chunks = t // MLP_CHUNK
    per_step = ws_ref.shape[0]
    cols_per_chunk = MLP_CHUNK // rows
    pitch = MLP_COL_PITCH

    @pl.when(gp < MLP_ROW_GROUPS // per_step)
    def _():
        for i in range(per_step):
            cs = slice(i * LANES, (i + 1) * LANES)
            for c in range(n_chunks):
                rs = slice(c * MLP_CHUNK, (c + 1) * MLP_CHUNK)
                s = _dot(ws_ref[i], v_ref[0, rs, cs]) + bs_ref[i]
                us_ref[rs, cs] = (u_ref[0, rs, cs].astype(F32) * s).astype(BF16)

    @pl.when(gp >= MLP_ROW_GROUPS // per_step)
    def _():
        for i in range(per_step):
            cs = slice(i * LANES, (i + 1) * LANES)
            for r in range(rows):
                vf_ref[r * pitch:r * pitch + GRID_W, :] = v_ref[0, r * GRID_W:(r + 1) * GRID_W, cs].astype(F32)
            for k in range(n_chunks):
                xk = jnp.concatenate(
                    [vf_ref[pl.ds(k * cols_per_chunk + wl, rows, stride=pitch), :] for wl in range(cols_per_chunk)],
                    axis=0)
                s = _dot(ws_ref[i], xk.astype(BF16)) + bs_ref[i]
                for wl in range(cols_per_chunk):
                    sf_ref[pl.ds(k * cols_per_chunk + wl, rows, stride=pitch), :] = s[wl * rows:(wl + 1) * rows, :]
            for r in range(rows):
                rr = slice(r * GRID_W, (r + 1) * GRID_W)
                us_ref[rr, cs] = (u_ref[0, rr, cs].astype(F32) * sf_ref[r * pitch:r * pitch + GRID_W, :]).astype(BF16)

    part = _dot(us_ref[...], w_ref[...])

    @pl.when(gp == 0)
    def _():
        o_ref[0] = part

    @pl.when(gp > 0)
    def _():
        o_ref[0] += part


def _mlp_mix(u, v, ws, bsb, w):
    b, l, inner = u.shape
    d = w.shape[1]
    per_step = MLP_GROUPS_PER_STEP
    steps = MLP_GROUPS // per_step
    kc = per_step * LANES
    rows = l // GRID_W
    return pl.pallas_call(
        functools.partial(_mlp_mix_kernel, rows=rows),
        grid=(b, steps),
        in_specs=[
            pl.BlockSpec((1, l, kc), lambda i, g: (i, 0, g)),
            pl.BlockSpec((1, l, kc), lambda i, g: (i, 0, g)),
            pl.BlockSpec((per_step, MLP_CHUNK, MLP_CHUNK), lambda i, g: (g, 0, 0)),
            pl.BlockSpec((per_step, MLP_CHUNK, LANES), lambda i, g: (g, 0, 0)),
            pl.BlockSpec((kc, d), lambda i, g: (g, 0)),
        ],
        out_specs=pl.BlockSpec((1, l, d), lambda i, g: (i, 0, 0)),
        out_shape=jax.ShapeDtypeStruct((b, l, d), F32),
        scratch_shapes=[pltpu.VMEM((rows * MLP_COL_PITCH, LANES), F32),
                        pltpu.VMEM((rows * MLP_COL_PITCH, LANES), F32),
                        pltpu.VMEM((l, kc), BF16)],
        compiler_params=_cparams(2, 48),
        name="mlp_mix",
    )(u, v, ws, bsb, w)


def _pad_rows(a, rows):
    return jnp.pad(a, ((0, rows - a.shape[0]),) + ((0, 0),) * (a.ndim - 1))


def kernel(x, c, ctx, c_ctx, mod_w, mod_b, norm_g, ssd_in_w, ssd_conv_w, ssd_conv_b, ssd_dt_bias, ssd_a_log,
           ssd_d, ssd_norm_g, ssd_out_w, mlp_in_w, mlp_v_g, mlp_ws, mlp_bs, mlp_out_w, router_w, exp_w_gate,
           exp_w_up, exp_w_down):
    b, l, d = x.shape
    inner = SSD_HEADS * SSD_HEADDIM
    n_chunks = l // SSD_CHUNK
    n_ctx_chunks = ctx.shape[1] // SSD_CHUNK

    crows = _pad_rows(jnp.concatenate([c, c_ctx[None, :]], axis=0), -(-(b + 1) // SUBLANES) * SUBLANES)
    mod = _modulation(crows, mod_w, mod_b)
    mods_lat = [jnp.pad(mod[i, :b].reshape(b, N_MOD, d), ((0, 0), (0, SUBLANES - N_MOD), (0, 0)))
                for i in range(2)]
    mods_ctx = jnp.pad(mod[0, b].reshape(1, N_MOD, d), ((0, 0), (0, SUBLANES - N_MOD), (0, 0)))
    mods01 = jnp.stack([jnp.broadcast_to(mods_ctx, (b, SUBLANES, d)), mods_lat[0]], axis=1)
    gains = [_pad_rows(norm_g[i], SUBLANES) for i in range(2)]
    rw2 = []
    for i in range(2):
        rw_t = router_w[i].T
        rw_hi = rw_t.astype(BF16)
        rw2.append(jnp.concatenate([rw_hi, (rw_t - rw_hi.astype(F32)).astype(BF16)], axis=0))

    in_w = ssd_in_w[0]
    conv_dim = ssd_conv_w.shape[2]
    w_in = jnp.pad(in_w, ((0, 0), (0, LANES - 2 * SSD_HEADS))).astype(BF16)
    dt_bias = jnp.pad(ssd_dt_bias[0].reshape(1, 2 * SSD_HEADS), ((0, 0), (0, LANES - 2 * SSD_HEADS)))
    z, xs_all, bc_all, dt_all = _ssd_in_proj(
        ctx, x, mods01, gains[0], w_in, _pad_rows(ssd_conv_w[0], SUBLANES), ssd_conv_b[0].reshape(1, conv_dim),
        dt_bias)

    alog = _pad_rows(jnp.stack([jnp.pad(ssd_a_log[0, 0], (0, LANES - SSD_HEADS)),
                                jnp.pad(ssd_a_log[0, 1], (SSD_HEADS, LANES - 2 * SSD_HEADS))]), SUBLANES)
    head_of_col = jnp.arange(inner, dtype=jnp.int32) // SSD_HEADDIM
    rows128 = jnp.arange(LANES, dtype=jnp.int32)[:, None]
    expand = jnp.stack([(rows128 == head_of_col[None, :] + SSD_HEADS * dd) for dd in range(2)]).astype(BF16)
    dskip = jnp.repeat(ssd_d[0], SSD_HEADDIM).reshape(1, inner)
    y = _ssd_scan(xs_all, bc_all, dt_all, alog, expand, dskip, n_ctx_chunks, n_chunks)

    xa, h2, aff_t = _ssd_out(y, z, x, mods_lat[0], gains[0], ssd_norm_g[0].reshape(1, inner),
                             ssd_out_w[0].astype(BF16), rw2[0])
    x1 = _moe(h2, aff_t, exp_w_gate, exp_w_up, exp_w_down, 0, xa, mods_lat[0], gains[0])

    u, v = _mlp_in(x1, mods_lat[1], gains[1], mlp_in_w[0].astype(BF16), mlp_v_g[0].reshape(1, -1))
    bsb = jnp.broadcast_to(mlp_bs[0][:, :, None], (MLP_GROUPS, MLP_CHUNK, LANES))
    o1 = _mlp_mix(u, v, mlp_ws[0].astype(BF16), bsb, mlp_out_w[0].astype(BF16))
    xb, h2b, aff_tb = _epilogue(o1, x1, mods_lat[1], gains[1], rw2[1])
    return _moe(h2b, aff_tb, exp_w_gate, exp_w_up, exp_w_down, 1, xb, mods_lat[1], gains[1])
```

```python
import functools

import jax
import jax.numpy as jnp
from jax import lax
from jax.experimental import pallas as pl
from jax.experimental.pallas import tpu as pltpu

F32 = jnp.float32
BF16 = jnp.bfloat16
HIGHEST = lax.Precision.HIGHEST
EPS = 1e-6

LANES = 128
SUBLANES = 8
MIB = 1024 * 1024

N_MOD = 6
GRID_W = 64
SSD_HEADDIM = 64
SSD_HEADS = 32
SSD_GROUPS = 4
SSD_HPG = SSD_HEADS // SSD_GROUPS
SSD_STATE = 128
SSD_CONV = 5
SSD_CHUNK = 128
MLP_CHUNK = 128
MLP_GROUPS = 16
MLP_ROW_GROUPS = 8
N_EXPERTS = 16
CAPACITY_FACTOR = 2

CONV_HALO = SUBLANES
TOKEN_TILE = 256
CONV_COL_BLOCK = 1024


def _cparams(n_axes, vmem_mib):
    return pltpu.CompilerParams(dimension_semantics=("arbitrary",) * n_axes,
                                vmem_limit_bytes=vmem_mib * MIB)


def _sigmoid(x):
    return 0.5 * (1.0 + jnp.tanh(0.5 * x))


def _rms(x, g):
    return x * lax.rsqrt(jnp.mean(x * x, axis=-1, keepdims=True) + EPS) * g


def _dot(a, b):
    return jnp.dot(a, b, preferred_element_type=F32)


def _dot_nt(a, b, precision=None):
    return lax.dot_general(a, b, (((1,), (1,)), ((), ())), preferred_element_type=F32, precision=precision)


def _dot_tn(a, b):
    return lax.dot_general(a, b, (((0,), (0,)), ((), ())), preferred_element_type=F32)


def _mod_kernel(c_ref, w_ref, b_ref, o_ref):
    c = c_ref[...]
    s = c * _sigmoid(c)
    o_ref[0] = jnp.dot(s, w_ref[0], preferred_element_type=F32, precision=HIGHEST) + b_ref[0]


def _modulation(crows, mod_w, mod_b):
    depth, d, n = mod_w.shape
    rows = crows.shape[0]
    tn = 1536
    return pl.pallas_call(
        _mod_kernel,
        grid=(depth, n // tn),
        in_specs=[pl.BlockSpec((rows, d), lambda i, j: (0, 0)),
                  pl.BlockSpec((1, d, tn), lambda i, j: (i, 0, j)),
                  pl.BlockSpec((1, 1, tn), lambda i, j: (i, 0, j))],
        out_specs=pl.BlockSpec((1, rows, tn), lambda i, j: (i, 0, j)),
        out_shape=jax.ShapeDtypeStruct((depth, rows, n), F32),
        compiler_params=_cparams(2, 40),
        name="modulation",
    )(crows, mod_w, mod_b.reshape(depth, 1, n))


def _ssd_in_kernel(ctx_ref, x_ref, xp_ref, xn_ref, mod_ref, g_ref, w_ref, cw_ref, cb_ref, dtb_ref, sh_ref,
                   z_ref, xs_ref, bc_ref, dt_ref, *, n_tiles, inner, conv_dim):
    j = pl.program_id(1)
    tm = x_ref.shape[1]
    halo = xp_ref.shape[1]
    xc = jnp.where(j == 0, ctx_ref[0], x_ref[0])
    xa = jnp.concatenate([xp_ref[0], xc, xn_ref[0]], axis=0)
    h = (_rms(xa, g_ref[0:1, :]) * (1.0 + mod_ref[0, 0, 1:2, :]) + mod_ref[0, 0, 0:1, :]).astype(BF16)
    hc = h[halo:halo + tm, :]
    z_ref[0] = _dot(hc, w_ref[:, :inner]).astype(BF16)
    dtr = _dot(hc, w_ref[:, inner + conv_dim:]) + dtb_ref[...]
    dt_ref[0] = jnp.maximum(dtr, 0.0) + jnp.log1p(jnp.exp(-jnp.abs(dtr)))
    keep_top = jnp.where(j <= 1, 0.0, 1.0)
    keep_bot = jnp.where(jnp.logical_or(j == 0, j == n_tiles - 1), 0.0, 1.0)
    hrow = lax.broadcasted_iota(jnp.int32, (halo, 1), 0)
    pad = (SSD_CONV - 1) // 2
    taps = [k for k in range(SSD_CONV) if k != pad]
    cblk = CONV_COL_BLOCK
    for cbi in range(conv_dim // cblk):
        c0 = cbi * cblk
        u = _dot(h, w_ref[:, inner + c0:inner + c0 + cblk])
        uc = u[halo:halo + tm, :]
        top = u[:halo, :] * keep_top
        bot = u[halo + tm:, :] * keep_bot
        shifted = _dot(sh_ref[...], uc.astype(BF16))
        acc = cb_ref[:, c0:c0 + cblk] + cw_ref[pad:pad + 1, c0:c0 + cblk] * uc
        corr_top = jnp.zeros((halo, cblk), F32)
        corr_bot = jnp.zeros((halo, cblk), F32)
        for jj, k in enumerate(taps):
            wk = cw_ref[k:k + 1, c0:c0 + cblk]
            acc = acc + wk * shifted[jj * tm:(jj + 1) * tm, :]
            d = k - pad
            if d < 0:
                corr_top = corr_top + wk * jnp.where(hrow < -d, pltpu.roll(top, -d, axis=0), 0.0)
            else:
                corr_bot = corr_bot + wk * jnp.where(hrow >= halo - d, pltpu.roll(bot, halo - d, axis=0), 0.0)
        acc = jnp.concatenate([acc[:halo, :] + corr_top, acc[halo:tm - halo, :], acc[tm - halo:, :] + corr_bot],
                              axis=0)
        act = (acc * _sigmoid(acc)).astype(BF16)
        if c0 < inner:
            xs_ref[0, :, c0:c0 + cblk] = act
        else:
            bc_ref[0, :, c0 - inner:c0 - inner + cblk] = act


def _ssd_in_proj(ctx, x, mods01, g, w, conv_w, conv_b, dt_bias):
    b, l, d = x.shape
    lc = ctx.shape[1]
    tm = TOKEN_TILE
    assert lc == tm and l % tm == 0
    n_tiles = 1 + l // tm
    inner = SSD_HEADS * SSD_HEADDIM
    conv_dim = conv_w.shape[1]
    bc_dim = conv_dim - inner
    halo = CONV_HALO
    blocks_per_tile = tm // halo
    last_halo_block = l // halo - 1
    lt = lc + l
    pad = (SSD_CONV - 1) // 2
    t_idx = jnp.arange(tm, dtype=jnp.int32)
    shifts = jnp.concatenate([t_idx[None, :] == t_idx[:, None] + (k - pad) for k in range(SSD_CONV) if k != pad],
                             axis=0).astype(BF16)
    kern = functools.partial(_ssd_in_kernel, n_tiles=n_tiles, inner=inner, conv_dim=conv_dim)
    return pl.pallas_call(
        kern,
        grid=(b, n_tiles),
        in_specs=[
            pl.BlockSpec((1, lc, d), lambda i, j: (i, 0, 0)),
            pl.BlockSpec((1, tm, d), lambda i, j: (i, jnp.maximum(j - 1, 0), 0)),
            pl.BlockSpec((1, halo, d), lambda i, j: (i, jnp.maximum((j - 1) * blocks_per_tile - 1, 0), 0)),
            pl.BlockSpec((1, halo, d), lambda i, j: (i, jnp.minimum(j * blocks_per_tile, last_halo_block), 0)),
            pl.BlockSpec((1, 1, SUBLANES, d), lambda i, j: (i, jnp.minimum(j, 1), 0, 0)),
            pl.BlockSpec((SUBLANES, d), lambda i, j: (0, 0)),
            pl.BlockSpec(w.shape, lambda i, j: (0, 0), pipeline_mode=pl.Buffered(1)),
            pl.BlockSpec((SUBLANES, conv_dim), lambda i, j: (0, 0)),
            pl.BlockSpec((1, conv_dim), lambda i, j: (0, 0)),
            pl.BlockSpec((1, LANES), lambda i, j: (0, 0)),
            pl.BlockSpec(shifts.shape, lambda i, j: (0, 0)),
        ],
        out_specs=[
            pl.BlockSpec((1, tm, inner), lambda i, j: (i, jnp.maximum(j - 1, 0), 0)),
            pl.BlockSpec((1, tm, inner), lambda i, j: (i, j, 0)),
            pl.BlockSpec((1, tm, bc_dim), lambda i, j: (i, j, 0)),
            pl.BlockSpec((1, tm, LANES), lambda i, j: (i, j, 0)),
        ],
        out_shape=[
            jax.ShapeDtypeStruct((b, l, inner), BF16),
            jax.ShapeDtypeStruct((b, lt, inner), BF16),
            jax.ShapeDtypeStruct((b, lt, bc_dim), BF16),
            jax.ShapeDtypeStruct((b, lt, LANES), F32),
        ],
        compiler_params=_cparams(2, 56),
        name="ssd_in_proj",
    )(ctx, x, x, x, mods01, g, w, conv_w, conv_b, dt_bias, shifts)


SSD_CHUNKS_PER_STEP = 2
SSD_DECAY_ROWS = 16


def _split3(x):
    hi = x.astype(BF16)
    r1 = x - hi.astype(F32)
    mid = r1.astype(BF16)
    lo = (r1 - mid.astype(F32)).astype(BF16)
    return hi, mid, lo


def _ssd_step(rev, is_lat, lblk, xs_ref, bc_ref, dt_ref, alog_ref, e_ref, dskip_ref, y_ref, st_ref, ybuf_ref):
    c = SSD_CHUNK
    nck = SSD_CHUNKS_PER_STEP
    dcol = SSD_HEADS * int(rev)
    gw = SSD_HPG * SSD_HEADDIM
    bc_off = SSD_GROUPS * SSD_STATE
    a_neg = -jnp.exp(alog_ref[int(rev):int(rev) + 1, :])
    li = lax.broadcasted_iota(jnp.int32, (c, c), 0)
    si = lax.broadcasted_iota(jnp.int32, (c, c), 1)
    causal = (si >= li) if rev else (si <= li)
    tri = jnp.where(causal, 1.0, 0.0).astype(BF16)
    e = e_ref[int(rev)]
    head = lax.broadcasted_iota(jnp.int32, (c, LANES), 1)

    dts = [dt_ref[0, k * c:(k + 1) * c, :] for k in range(nck)]
    pieces = []
    for k in range(nck):
        pieces.extend(_split3(dts[k] * a_neg))
    run = _dot(tri, jnp.concatenate(pieces, axis=1))
    css = [run[:, (3 * k) * LANES:(3 * k + 1) * LANES] + run[:, (3 * k + 1) * LANES:(3 * k + 2) * LANES]
           + run[:, (3 * k + 2) * LANES:(3 * k + 3) * LANES] for k in range(nck)]
    tots = [cs[0:1, :] if rev else cs[c - 1:c, :] for cs in css]

    blocks = []
    for k in range(nck):
        blocks.append((jnp.exp(tots[k] - css[k]) * dts[k]).astype(BF16))
        blocks.append(jnp.exp(css[k]).astype(BF16))
    drow = lax.broadcasted_iota(jnp.int32, (SSD_DECAY_ROWS, LANES), 0)
    for k in range(nck):
        hi, mid, lo = _split3(jnp.broadcast_to(jnp.exp(tots[k]), (SSD_DECAY_ROWS, LANES)))
        rows3 = jnp.where(drow == 0, hi.astype(F32),
                          jnp.where(drow == 1, mid.astype(F32), jnp.where(drow == 2, lo.astype(F32), 0.0)))
        blocks.append(rows3.astype(BF16))
    ex = _dot(jnp.concatenate(blocks, axis=0), e).astype(BF16)
    dec0 = 2 * nck * c
    even_cols = jnp.where(head < SSD_HEADDIM, 1.0, 0.0).astype(BF16)
    odd_cols = jnp.where(head < SSD_HEADDIM, 0.0, 1.0).astype(BF16)

    for k in (range(nck - 1, -1, -1) if rev else range(nck)):
        base = 2 * k * c
        xs = xs_ref[0, k * c:(k + 1) * c, :]
        bc = bc_ref[0, k * c:(k + 1) * c, :]
        xw = xs * ex[base:base + c, :]

        @pl.when(is_lat)
        def _(k=k, base=base, xs=xs, bc=bc):
            lc = lblk * nck + k
            cs = css[k]
            cs_t = cs.T
            dt_t = dts[k].T
            for g in range(SSD_GROUPS):
                bg = bc[:, g * SSD_STATE:(g + 1) * SSD_STATE]
                cg = bc[:, bc_off + g * SSD_STATE:bc_off + (g + 1) * SSD_STATE]
                px = ex[base + c:base + 2 * c, g * gw:(g + 1) * gw].astype(F32)
                y_off = _dot(cg, st_ref[:, g * gw:(g + 1) * gw].astype(BF16)) * px
                cbm = jnp.where(causal, _dot_nt(cg, bg), 0.0)
                for hp in range(SSD_HPG // 2):
                    ms = []
                    for hh in range(2):
                        hd = dcol + g * SSD_HPG + 2 * hp + hh
                        diff = cs[:, hd:hd + 1] - cs_t[hd:hd + 1, :]
                        ms.append(cbm * jnp.exp(jnp.minimum(diff, 0.0)) * dt_t[hd:hd + 1, :])
                    mp = jnp.concatenate(ms, axis=1).astype(BF16)
                    col0 = (g * SSD_HPG + 2 * hp) * SSD_HEADDIM
                    xp = xs[:, col0:col0 + LANES]
                    rhs = jnp.concatenate([xp * even_cols, xp * odd_cols], axis=0)
                    y_pair = _dot(mp, rhs) + y_off[:, 2 * hp * SSD_HEADDIM:2 * hp * SSD_HEADDIM + LANES]
                    if rev:
                        tot_y = (ybuf_ref[lc, :, col0:col0 + LANES] + y_pair
                                 + dskip_ref[:, col0:col0 + LANES] * xp.astype(F32))
                        y_ref[0, k * c:(k + 1) * c, col0:col0 + LANES] = tot_y.astype(BF16)
                    else:
                        ybuf_ref[lc, :, col0:col0 + LANES] = y_pair

        drows = ex[dec0 + k * SSD_DECAY_ROWS:dec0 + (k + 1) * SSD_DECAY_ROWS, :].astype(F32)
        decay = drows[0:1, :] + drows[1:2, :] + drows[2:3, :]
        for g in range(SSD_GROUPS):
            bg = bc[:, g * SSD_STATE:(g + 1) * SSD_STATE]
            upd = _dot_tn(bg, xw[:, g * gw:(g + 1) * gw])
            st_ref[:, g * gw:(g + 1) * gw] = st_ref[:, g * gw:(g + 1) * gw] * decay[:, g * gw:(g + 1) * gw] + upd


def _ssd_scan_kernel(xs_ref, bc_ref, dt_ref, alog_ref, e_ref, dskip_ref, y_ref, st_ref, ybuf_ref, *,
                     n_ctx_blocks, n_blocks):
    d = pl.program_id(1)
    s = pl.program_id(2)

    @pl.when(s == 0)
    def _():
        st_ref[...] = jnp.zeros_like(st_ref)

    for rev in (False, True):
        @pl.when(d == int(rev))
        def _(rev=rev):
            if rev:
                blk = jnp.where(s < n_ctx_blocks, n_ctx_blocks - 1 - s, n_blocks + 2 * n_ctx_blocks - 1 - s)
            else:
                blk = s
            _ssd_step(rev, blk >= n_ctx_blocks, jnp.maximum(blk - n_ctx_blocks, 0), xs_ref, bc_ref, dt_ref,
                      alog_ref, e_ref, dskip_ref, y_ref, st_ref, ybuf_ref)


def _ssd_scan(xs_all, bc_all, dt_all, alog, expand, dskip, n_ctx_chunks, n_chunks):
    b, lt, inner = xs_all.shape
    nck = SSD_CHUNKS_PER_STEP
    rows = nck * SSD_CHUNK
    assert n_ctx_chunks % nck == 0 and n_chunks % nck == 0
    n_ctx_blocks = n_ctx_chunks // nck
    n_blocks = n_chunks // nck
    steps = n_ctx_blocks + n_blocks

    def block_idx(d, s):
        bwd = jnp.where(s < n_ctx_blocks, n_ctx_blocks - 1 - s, steps + n_ctx_blocks - 1 - s)
        return jnp.where(d == 0, s, bwd)

    def out_idx(d, s):
        return jnp.where(jnp.logical_and(d == 1, s >= n_ctx_blocks), steps - 1 - s, n_blocks - 1)

    kern = functools.partial(_ssd_scan_kernel, n_ctx_blocks=n_ctx_blocks, n_blocks=n_blocks)
    return pl.pallas_call(
        kern,
        grid=(b, 2, steps),
        in_specs=[
            pl.BlockSpec((1, rows, inner), lambda i, d, s: (i, block_idx(d, s), 0)),
            pl.BlockSpec((1, rows, bc_all.shape[2]), lambda i, d, s: (i, block_idx(d, s), 0)),
            pl.BlockSpec((1, rows, LANES), lambda i, d, s: (i, block_idx(d, s), 0)),
            pl.BlockSpec((SUBLANES, LANES), lambda i, d, s: (0, 0)),
            pl.BlockSpec((2, LANES, inner), lambda i, d, s: (0, 0, 0)),
            pl.BlockSpec((1, inner), lambda i, d, s: (0, 0)),
        ],
        out_specs=pl.BlockSpec((1, rows, inner), lambda i, d, s: (i, out_idx(d, s), 0)),
        out_shape=jax.ShapeDtypeStruct((b, n_chunks * SSD_CHUNK, inner), BF16),
        scratch_shapes=[pltpu.VMEM((SSD_STATE, inner), F32),
                        pltpu.VMEM((n_chunks, SSD_CHUNK, inner), F32)],
        compiler_params=_cparams(3, 56),
        name="ssd_scan",
    )(xs_all, bc_all, dt_all, alog, expand, dskip)


def _mixer_epilogue(o, x, mod_ref, g_ref, rw_ref):
    xn = x + mod_ref[0, 2:3, :] * _rms(o, g_ref[1:2, :])
    h2 = _rms(xn, g_ref[2:3, :]) * (1.0 + mod_ref[0, 4:5, :]) + mod_ref[0, 3:4, :]
    h_hi = h2.astype(BF16)
    h_lo = (h2 - h_hi.astype(F32)).astype(BF16)
    ne = rw_ref.shape[0] // 2
    both = _dot_nt(rw_ref[...], h_hi)
    lg = both[:ne, :] + both[ne:, :] + _dot_nt(rw_ref[0:ne, :], h_lo)
    ex = jnp.exp(lg - jnp.max(lg, axis=0, keepdims=True))
    return xn, h_hi, ex / jnp.sum(ex, axis=0, keepdims=True)


def _ssd_out_kernel(y_ref, z_ref, x_ref, mod_ref, g_ref, ng_ref, w_ref, rw_ref, xo_ref, h2_ref, aff_ref):
    sub = TOKEN_TILE
    gdim = y_ref.shape[2] // SSD_GROUPS
    for r in range(y_ref.shape[1] // sub):
        rs = slice(r * sub, (r + 1) * sub)
        z = z_ref[0, rs, :].astype(F32)
        yg = y_ref[0, rs, :].astype(F32) * (z * _sigmoid(z))
        parts = [_rms(yg[:, g * gdim:(g + 1) * gdim], ng_ref[:, g * gdim:(g + 1) * gdim]).astype(BF16)
                 for g in range(SSD_GROUPS)]
        o = _dot(jnp.concatenate(parts, axis=1), w_ref[...])
        xn, hb, aff = _mixer_epilogue(o, x_ref[0, rs, :], mod_ref, g_ref, rw_ref)
        xo_ref[0, rs, :] = xn
        h2_ref[0, rs, :] = hb
        aff_ref[0, :, rs] = aff


def _epilogue_outs(b, l, d, tm):
    out_specs = [pl.BlockSpec((1, tm, d), lambda i, j: (i, j, 0)),
                 pl.BlockSpec((1, tm, d), lambda i, j: (i, j, 0)),
                 pl.BlockSpec((1, N_EXPERTS, tm), lambda i, j: (i, 0, j))]
    out_shape = [jax.ShapeDtypeStruct((b, l, d), F32),
                 jax.ShapeDtypeStruct((b, l, d), BF16),
                 jax.ShapeDtypeStruct((b, N_EXPERTS, l), F32)]
    return out_specs, out_shape


def _ssd_out(y, z, x, mods, g, ng, w, rw2):
    b, l, d = x.shape
    inner = y.shape[2]
    tm = 2 * TOKEN_TILE
    out_specs, out_shape = _epilogue_outs(b, l, d, tm)
    return pl.pallas_call(
        _ssd_out_kernel,
        grid=(b, l // tm),
        in_specs=[
            pl.BlockSpec((1, tm, inner), lambda i, j: (i, j, 0)),
            pl.BlockSpec((1, tm, inner), lambda i, j: (i, j, 0)),
            pl.BlockSpec((1, tm, d), lambda i, j: (i, j, 0)),
            pl.BlockSpec((1, SUBLANES, d), lambda i, j: (i, 0, 0)),
            pl.BlockSpec((SUBLANES, d), lambda i, j: (0, 0)),
            pl.BlockSpec((1, inner), lambda i, j: (0, 0)),
            pl.BlockSpec(w.shape, lambda i, j: (0, 0)),
            pl.BlockSpec(rw2.shape, lambda i, j: (0, 0)),
        ],
        out_specs=out_specs,
        out_shape=out_shape,
        compiler_params=_cparams(2, 48),
        name="ssd_out",
    )(y, z, x, mods, g, ng, w, rw2)


def _epilogue_kernel(o_ref, x_ref, mod_ref, g_ref, rw_ref, xo_ref, h2_ref, aff_ref):
    sub = TOKEN_TILE
    for r in range(o_ref.shape[1] // sub):
        rs = slice(r * sub, (r + 1) * sub)
        xn, hb, aff = _mixer_epilogue(o_ref[0, rs, :], x_ref[0, rs, :], mod_ref, g_ref, rw_ref)
        xo_ref[0, rs, :] = xn
        h2_ref[0, rs, :] = hb
        aff_ref[0, :, rs] = aff


def _epilogue(o, x, mods, g, rw_t):
    b, l, d = x.shape
    tm = 2 * TOKEN_TILE
    out_specs, out_shape = _epilogue_outs(b, l, d, tm)
    return pl.pallas_call(
        _epilogue_kernel,
        grid=(b, l // tm),
        in_specs=[
            pl.BlockSpec((1, tm, d), lambda i, j: (i, j, 0)),
            pl.BlockSpec((1, tm, d), lambda i, j: (i, j, 0)),
            pl.BlockSpec((1, SUBLANES, d), lambda i, j: (i, 0, 0)),
            pl.BlockSpec((SUBLANES, d), lambda i, j: (0, 0)),
            pl.BlockSpec(rw_t.shape, lambda i, j: (0, 0)),
        ],
        out_specs=out_specs,
        out_shape=out_shape,
        compiler_params=_cparams(2, 40),
        name="mixer_epilogue",
    )(o, x, mods, g, rw_t)


def _lane_prefix_exclusive(m01, upper):
    e, t = m01.shape
    carry = jnp.zeros((e, 1), F32)
    outs = []
    for k in range(t // LANES):
        tile = m01[:, k * LANES:(k + 1) * LANES]
        incl = _dot(tile.astype(BF16), upper)
        outs.append(incl - tile + carry)
        carry = carry + incl[:, LANES - 1:LANES]
    return jnp.concatenate(outs, axis=1)


def _route(aff, cap):
    e, t = aff.shape
    key = pltpu.bitcast(aff, jnp.int32)

    def body(i, thr):
        cand = jnp.bitwise_or(thr, jnp.left_shift(jnp.int32(1), 30 - i))
        cnt = jnp.sum(jnp.where(key >= cand, 1.0, 0.0), axis=1, keepdims=True)
        return jnp.where(cnt >= cap, cand, thr)

    thr = lax.fori_loop(0, 31, body, jnp.zeros((e, 1), jnp.int32))
    gt = jnp.where(key > thr, 1.0, 0.0)
    eq = jnp.where(key == thr, 1.0, 0.0)
    need = cap - jnp.sum(gt, axis=1, keepdims=True)
    r = lax.broadcasted_iota(jnp.int32, (LANES, LANES), 0)
    cidx = lax.broadcasted_iota(jnp.int32, (LANES, LANES), 1)
    upper = jnp.where(r <= cidx, 1.0, 0.0).astype(BF16)
    eq_rank = _lane_prefix_exclusive(eq, upper)
    sel = gt + eq * jnp.where(eq_rank < need, 1.0, 0.0)
    pos = _lane_prefix_exclusive(sel, upper)
    return jnp.where(sel > 0.5, pos, -1.0).astype(jnp.int32)


MOE_EXPERTS_PER_STEP = 4
MOE_FFN_TILE = 1024


def _pick_rows(slot_ref, e, cap, t):
    srow = slot_ref[0, pl.ds(e, 1), :]
    return lax.broadcasted_iota(jnp.int32, (cap, t), 0) == srow


def _moe_gather_kernel(h_ref, aff_ref, xin_ref, gate_ref, slot_ref, *, cap):
    eg = pl.program_id(1)

    @pl.when(eg == 0)
    def _():
        slot_ref[0] = _route(aff_ref[0], cap)

    t = h_ref.shape[1]
    ps = []
    for i in range(MOE_EXPERTS_PER_STEP):
        e = eg * MOE_EXPERTS_PER_STEP + i
        pick = _pick_rows(slot_ref, e, cap, t)
        arow = aff_ref[0, pl.ds(e, 1), :]
        gate = jnp.sum(jnp.where(pick, arow, 0.0), axis=1, keepdims=True)
        gate_ref[i] = jnp.broadcast_to(gate, (cap, LANES))
        ps.append(jnp.where(pick, 1.0, 0.0).astype(BF16))
    xin = _dot(jnp.concatenate(ps, axis=0), h_ref[0])
    for i in range(MOE_EXPERTS_PER_STEP):
        xin_ref[i] = xin[i * cap:(i + 1) * cap, :].astype(BF16)


def _moe_gather(h2, aff_t):
    b, t, d = h2.shape
    ne = aff_t.shape[1]
    cap = (CAPACITY_FACTOR * t) // ne
    g = MOE_EXPERTS_PER_STEP
    return pl.pallas_call(
        functools.partial(_moe_gather_kernel, cap=cap),
        grid=(b, ne // g),
        in_specs=[pl.BlockSpec((1, t, d), lambda i, e: (i, 0, 0)),
                  pl.BlockSpec((1, ne, t), lambda i, e: (i, 0, 0))],
        out_specs=[pl.BlockSpec((g, cap, d), lambda i, e: (e, i, 0)),
                   pl.BlockSpec((g, cap, LANES), lambda i, e: (e, i, 0)),
                   pl.BlockSpec((1, ne, t), lambda i, e: (i, 0, 0))],
        out_shape=[jax.ShapeDtypeStruct((ne, b * cap, d), BF16),
                   jax.ShapeDtypeStruct((ne, b * cap, LANES), F32),
                   jax.ShapeDtypeStruct((b, ne, t), jnp.int32)],
        compiler_params=_cparams(2, 48),
        name="moe_gather",
    )(h2, aff_t)


def _moe_ffn_kernel(x_ref, gate_ref, wg_ref, wu_ref, wd_ref, y_ref, acc_ref):
    half = pl.program_id(2)
    x = x_ref[0]
    hid = _dot(x, wg_ref[0, 0].astype(BF16))
    hid = (hid * _sigmoid(hid)) * _dot(x, wu_ref[0, 0].astype(BF16))
    part = _dot(hid.astype(BF16), wd_ref[0, 0].astype(BF16))

    @pl.when(half == 0)
    def _():
        acc_ref[...] = part

    @pl.when(half == 1)
    def _():
        y_ref[0] = ((acc_ref[...] + part) * gate_ref[0, :, 0:1]).astype(BF16)


def _moe_ffn(xin, gates, w_gate, w_up, w_down, layer):
    ne, m, d = xin.shape
    ff = w_gate.shape[3]
    tm = MOE_FFN_TILE
    hf = ff // 2
    return pl.pallas_call(
        _moe_ffn_kernel,
        grid=(ne, m // tm, 2),
        in_specs=[pl.BlockSpec((1, tm, d), lambda e, j, h: (e, j, 0)),
                  pl.BlockSpec((1, tm, LANES), lambda e, j, h: (e, j, 0)),
                  pl.BlockSpec((1, 1, d, hf), lambda e, j, h: (layer, e, 0, h)),
                  pl.BlockSpec((1, 1, d, hf), lambda e, j, h: (layer, e, 0, h)),
                  pl.BlockSpec((1, 1, hf, d), lambda e, j, h: (layer, e, h, 0))],
        out_specs=pl.BlockSpec((1, tm, d), lambda e, j, h: (e, j, 0)),
        out_shape=jax.ShapeDtypeStruct((ne, m, d), BF16),
        scratch_shapes=[pltpu.VMEM((tm, d), F32)],
        compiler_params=_cparams(3, 56),
        name="moe_ffn",
    )(xin, gates, w_gate, w_up, w_down)


def _moe_scatter_kernel(slot_ref, y_ref, x_ref, mod_ref, g_ref, o_ref, *, cap):
    eg = pl.program_id(1)
    t = o_ref.shape[1]
    ps = [jnp.where(_pick_rows(slot_ref, eg * MOE_EXPERTS_PER_STEP + i, cap, t), 1.0, 0.0).astype(BF16)
          for i in range(MOE_EXPERTS_PER_STEP)]
    p = jnp.concatenate(ps, axis=0)
    y = y_ref[...].reshape(MOE_EXPERTS_PER_STEP * cap, y_ref.shape[2])
    contrib = _dot_tn(p, y)

    @pl.when(eg == 0)
    def _():
        o_ref[0] = contrib

    @pl.when(eg > 0)
    def _():
        o_ref[0] += contrib

    @pl.when(eg == pl.num_programs(1) - 1)
    def _():
        sub = TOKEN_TILE
        for r in range(t // sub):
            rs = slice(r * sub, (r + 1) * sub)
            o_ref[0, rs, :] = x_ref[0, rs, :] + mod_ref[0, 5:6, :] * _rms(o_ref[0, rs, :], g_ref[3:4, :])


def _moe_scatter(slot, y, x, mods, g):
    ne, m, d = y.shape
    b, _, t = slot.shape
    cap = m // b
    grp = MOE_EXPERTS_PER_STEP
    return pl.pallas_call(
        functools.partial(_moe_scatter_kernel, cap=cap),
        grid=(b, ne // grp),
        in_specs=[pl.BlockSpec((1, ne, t), lambda i, e: (i, 0, 0)),
                  pl.BlockSpec((grp, cap, d), lambda i, e: (e, i, 0)),
                  pl.BlockSpec((1, t, d), lambda i, e: (i, 0, 0), pipeline_mode=pl.Buffered(1)),
                  pl.BlockSpec((1, SUBLANES, d), lambda i, e: (i, 0, 0)),
                  pl.BlockSpec((SUBLANES, d), lambda i, e: (0, 0))],
        out_specs=pl.BlockSpec((1, t, d), lambda i, e: (i, 0, 0)),
        out_shape=jax.ShapeDtypeStruct((b, t, d), F32),
        compiler_params=_cparams(2, 56),
        name="moe_scatter",
    )(slot, y, x, mods, g)


def _moe(h2, aff_t, w_gate, w_up, w_down, layer, x, mods, g):
    xin, gates, slot = _moe_gather(h2, aff_t)
    y = _moe_ffn(xin, gates, w_gate, w_up, w_down, layer)
    return _moe_scatter(slot, y, x, mods, g)


GELU_C1 = 0.7978845608028654
GELU_C2 = GELU_C1 * 0.044715


def _mlp_in_kernel(x_ref, mod_ref, g_ref, w_ref, vg_ref, u_ref, v_ref):
    h = _rms(x_ref[0], g_ref[0:1, :]) * (1.0 + mod_ref[0, 1:2, :]) + mod_ref[0, 0:1, :]
    r = _dot(h.astype(BF16), w_ref[...])
    hr = 0.5 * r
    ge = hr + hr * jnp.tanh(r * (GELU_C1 + GELU_C2 * (r * r)))
    half = ge.shape[1] // 2
    u_ref[0] = ge[:, :half].astype(BF16)
    v = ge[:, half:]
    vc = v - jnp.mean(v, axis=-1, keepdims=True)
    vn = vc * lax.rsqrt(jnp.mean(vc * vc, axis=-1, keepdims=True) + EPS) * vg_ref[...]
    v_ref[0] = vn.astype(BF16)


def _mlp_in(x, mods, g, w, vg):
    b, l, d = x.shape
    tm = TOKEN_TILE
    half = w.shape[1] // 2
    tok = lambda i, j: (i, j, 0)
    return pl.pallas_call(
        _mlp_in_kernel,
        grid=(b, l // tm),
        in_specs=[
            pl.BlockSpec((1, tm, d), tok),
            pl.BlockSpec((1, SUBLANES, d), lambda i, j: (i, 0, 0)),
            pl.BlockSpec((SUBLANES, d), lambda i, j: (0, 0)),
            pl.BlockSpec(w.shape, lambda i, j: (0, 0), pipeline_mode=pl.Buffered(1)),
            pl.BlockSpec((1, half), lambda i, j: (0, 0)),
        ],
        out_specs=[pl.BlockSpec((1, tm, half), tok),
                   pl.BlockSpec((1, tm, half), tok)],
        out_shape=[jax.ShapeDtypeStruct((b, l, half), BF16),
                   jax.ShapeDtypeStruct((b, l, half), BF16)],
        compiler_params=_cparams(2, 48),
        name="mlp_in_proj",
    )(x, mods, g, w, vg)


MLP_GROUPS_PER_STEP = 4
MLP_COL_PITCH = GRID_W + SUBLANES


def _mlp_mix_kernel(u_ref, v_ref, ws_ref, bs_ref, w_ref, o_ref, vf_ref, sf_ref, us_ref, *, rows):
    gp = pl.program_id(1)
    t = u_ref.shape[1]
    n_chunks = t // MLP_CHUNK
    per_step = ws_ref.shape[0]
    cols_per_chunk = MLP_CHUNK // rows
    pitch = MLP_COL_PITCH

    @pl.when(gp < MLP_ROW_GROUPS // per_step)
    def _():
        for i in range(per_step):
            cs = slice(i * LANES, (i + 1) * LANES)
            for c in range(n_chunks):
                rs = slice(c * MLP_CHUNK, (c + 1) * MLP_CHUNK)
                s = _dot(ws_ref[i], v_ref[0, rs, cs]) + bs_ref[i]
                us_ref[rs, cs] = (u_ref[0, rs, cs].astype(F32) * s).astype(BF16)

    @pl.when(gp >= MLP_ROW_GROUPS // per_step)
    def _():
        for i in range(per_step):
            cs = slice(i * LANES, (i + 1) * LANES)
            for r in range(rows):
                vf_ref[r * pitch:r * pitch + GRID_W, :] = v_ref[0, r * GRID_W:(r + 1) * GRID_W, cs].astype(F32)
            for k in range(n_chunks):
                xk = jnp.concatenate(
                    [vf_ref[pl.ds(k * cols_per_chunk + wl, rows, stride=pitch), :] for wl in range(cols_per_chunk)],
                    axis=0)
                s = _dot(ws_ref[i], xk.astype(BF16)) + bs_ref[i]
                for wl in range(cols_per_chunk):
                    sf_ref[pl.ds(k * cols_per_chunk + wl, rows, stride=pitch), :] = s[wl * rows:(wl + 1) * rows, :]
            for r in range(rows):
                rr = slice(r * GRID_W, (r + 1) * GRID_W)
                us_ref[rr, cs] = (u_ref[0, rr, cs].astype(F32) * sf_ref[r * pitch:r * pitch + GRID_W, :]).astype(BF16)

    part = _dot(us_ref[...], w_ref[...])

    @pl.when(gp == 0)
    def _():
        o_ref[0] = part

    @pl.when(gp > 0)
    def _():
        o_ref[0] += part


def _mlp_mix(u, v, ws, bsb, w):
    b, l, inner = u.shape
    d = w.shape[1]
    per_step = MLP_GROUPS_PER_STEP
    steps = MLP_GROUPS // per_step
    kc = per_step * LANES
    rows = l // GRID_W
    return pl.pallas_call(
        functools.partial(_mlp_mix_kernel, rows=rows),
        grid=(b, steps),
        in_specs=[
            pl.BlockSpec((1, l, kc), lambda i, g: (i, 0, g)),
            pl.BlockSpec((1, l, kc), lambda i, g: (i, 0, g)),
            pl.BlockSpec((per_step, MLP_CHUNK, MLP_CHUNK), lambda i, g: (g, 0, 0)),
            pl.BlockSpec((per_step, MLP_CHUNK, LANES), lambda i, g: (g, 0, 0)),
            pl.BlockSpec((kc, d), lambda i, g: (g, 0)),
        ],
        out_specs=pl.BlockSpec((1, l, d), lambda i, g: (i, 0, 0)),
        out_shape=jax.ShapeDtypeStruct((b, l, d), F32),
        scratch_shapes=[pltpu.VMEM((rows * MLP_COL_PITCH, LANES), F32),
                        pltpu.VMEM((rows * MLP_COL_PITCH, LANES), F32),
                        pltpu.VMEM((l, kc), BF16)],
        compiler_params=_cparams(2, 48),
        name="mlp_mix",
    )(u, v, ws, bsb, w)


def _pad_rows(a, rows):
    return jnp.pad(a, ((0, rows - a.shape[0]),) + ((0, 0),) * (a.ndim - 1))


def kernel(x, c, ctx, c_ctx, mod_w, mod_b, norm_g, ssd_in_w, ssd_conv_w, ssd_conv_b, ssd_dt_bias, ssd_a_log,
           ssd_d, ssd_norm_g, ssd_out_w, mlp_in_w, mlp_v_g, mlp_ws, mlp_bs, mlp_out_w, router_w, exp_w_gate,
           exp_w_up, exp_w_down):
    b, l, d = x.shape
    inner = SSD_HEADS * SSD_HEADDIM
    n_chunks = l // SSD_CHUNK
    n_ctx_chunks = ctx.shape[1] // SSD_CHUNK

    crows = _pad_rows(jnp.concatenate([c, c_ctx[None, :]], axis=0), -(-(b + 1) // SUBLANES) * SUBLANES)
    mod = _modulation(crows, mod_w, mod_b)
    mods_lat = [jnp.pad(mod[i, :b].reshape(b, N_MOD, d), ((0, 0), (0, SUBLANES - N_MOD), (0, 0)))
                for i in range(2)]
    mods_ctx = jnp.pad(mod[0, b].reshape(1, N_MOD, d), ((0, 0), (0, SUBLANES - N_MOD), (0, 0)))
    mods01 = jnp.stack([jnp.broadcast_to(mods_ctx, (b, SUBLANES, d)), mods_lat[0]], axis=1)
    gains = [_pad_rows(norm_g[i], SUBLANES) for i in range(2)]
    rw2 = []
    for i in range(2):
        rw_t = router_w[i].T
        rw_hi = rw_t.astype(BF16)
        rw2.append(jnp.concatenate([rw_hi, (rw_t - rw_hi.astype(F32)).astype(BF16)], axis=0))

    in_w = ssd_in_w[0]
    conv_dim = ssd_conv_w.shape[2]
    w_in = jnp.pad(in_w, ((0, 0), (0, LANES - 2 * SSD_HEADS))).astype(BF16)
    dt_bias = jnp.pad(ssd_dt_bias[0].reshape(1, 2 * SSD_HEADS), ((0, 0), (0, LANES - 2 * SSD_HEADS)))
    z, xs_all, bc_all, dt_all = _ssd_in_proj(
        ctx, x, mods01, gains[0], w_in, _pad_rows(ssd_conv_w[0], SUBLANES), ssd_conv_b[0].reshape(1, conv_dim),
        dt_bias)

    alog = _pad_rows(jnp.stack([jnp.pad(ssd_a_log[0, 0], (0, LANES - SSD_HEADS)),
                                jnp.pad(ssd_a_log[0, 1], (SSD_HEADS, LANES - 2 * SSD_HEADS))]), SUBLANES)
    head_of_col = jnp.arange(inner, dtype=jnp.int32) // SSD_HEADDIM
    rows128 = jnp.arange(LANES, dtype=jnp.int32)[:, None]
    expand = jnp.stack([(rows128 == head_of_col[None, :] + SSD_HEADS * dd) for dd in range(2)]).astype(BF16)
    dskip = jnp.repeat(ssd_d[0], SSD_HEADDIM).reshape(1, inner)
    y = _ssd_scan(xs_all, bc_all, dt_all, alog, expand, dskip, n_ctx_chunks, n_chunks)

    xa, h2, aff_t = _ssd_out(y, z, x, mods_lat[0], gains[0], ssd_norm_g[0].reshape(1, inner),
                             ssd_out_w[0].astype(BF16), rw2[0])
    x1 = _moe(h2, aff_t, exp_w_gate, exp_w_up, exp_w_down, 0, xa, mods_lat[0], gains[0])

    u, v = _mlp_in(x1, mods_lat[1], gains[1], mlp_in_w[0].astype(BF16), mlp_v_g[0].reshape(1, -1))
    bsb = jnp.broadcast_to(mlp_bs[0][:, :, None], (MLP_GROUPS, MLP_CHUNK, LANES))
    o1 = _mlp_mix(u, v, mlp_ws[0].astype(BF16), bsb, mlp_out_w[0].astype(BF16))
    xb, h2b, aff_tb = _epilogue(o1, x1, mods_lat[1], gains[1], rw2[1])
    return _moe(h2b, aff_tb, exp_w_gate, exp_w_up, exp_w_down, 1, xb, mods_lat[1], gains[1])
```

```python
import functools

import jax
import jax.numpy as jnp
from jax import lax
from jax.experimental import pallas as pl
from jax.experimental.pallas import tpu as pltpu

F32 = jnp.float32
BF16 = jnp.bfloat16
HIGHEST = lax.Precision.HIGHEST
EPS = 1e-6

LANES = 128
SUBLANES = 8
MIB = 1024 * 1024

N_MOD = 6
GRID_W = 64
SSD_HEADDIM = 64
SSD_HEADS = 32
SSD_GROUPS = 4
SSD_HPG = SSD_HEADS // SSD_GROUPS
SSD_STATE = 128
SSD_CONV = 5
SSD_CHUNK = 128
MLP_CHUNK = 128
MLP_GROUPS = 16
MLP_ROW_GROUPS = 8
N_EXPERTS = 16
CAPACITY_FACTOR = 2

CONV_HALO = SUBLANES
TOKEN_TILE = 256
CONV_COL_BLOCK = 1024


def _cparams(n_axes, vmem_mib):
    return pltpu.CompilerParams(dimension_semantics=("arbitrary",) * n_axes,
                                vmem_limit_bytes=vmem_mib * MIB)


def _silu(x):
    h = 0.5 * x
    return h + h * jnp.tanh(h)


def _rms(x, g):
    return x * lax.rsqrt(jnp.mean(x * x, axis=-1, keepdims=True) + EPS) * g


def _dot(a, b):
    return jnp.dot(a, b, preferred_element_type=F32)


def _dot_nt(a, b, precision=None):
    return lax.dot_general(a, b, (((1,), (1,)), ((), ())), preferred_element_type=F32, precision=precision)


def _dot_tn(a, b):
    return lax.dot_general(a, b, (((0,), (0,)), ((), ())), preferred_element_type=F32)


def _mod_kernel(c_ref, w_ref, b_ref, o_ref):
    c = c_ref[...]
    s = _silu(c)
    o_ref[0] = jnp.dot(s, w_ref[0], preferred_element_type=F32, precision=HIGHEST) + b_ref[0]


def _modulation(crows, mod_w, mod_b):
    depth, d, n = mod_w.shape
    rows = crows.shape[0]
    tn = 1536
    return pl.pallas_call(
        _mod_kernel,
        grid=(depth, n // tn),
        in_specs=[pl.BlockSpec((rows, d), lambda i, j: (0, 0)),
                  pl.BlockSpec((1, d, tn), lambda i, j: (i, 0, j)),
                  pl.BlockSpec((1, 1, tn), lambda i, j: (i, 0, j))],
        out_specs=pl.BlockSpec((1, rows, tn), lambda i, j: (i, 0, j)),
        out_shape=jax.ShapeDtypeStruct((depth, rows, n), F32),
        compiler_params=_cparams(2, 40),
        name="modulation",
    )(crows, mod_w, mod_b.reshape(depth, 1, n))


def _ssd_in_kernel(ctx_ref, x_ref, xp_ref, xn_ref, mod_ref, g_ref, w_ref, cw_ref, cb_ref, dtb_ref, sh_ref,
                   z_ref, xs_ref, bc_ref, dt_ref, *, n_tiles, inner, conv_dim):
    j = pl.program_id(1)
    tm = x_ref.shape[1]
    halo = xp_ref.shape[1]
    xc = jnp.where(j == 0, ctx_ref[0], x_ref[0])
    xa = jnp.concatenate([xp_ref[0], xc, xn_ref[0]], axis=0)
    h = (_rms(xa, g_ref[0:1, :]) * (1.0 + mod_ref[0, 0, 1:2, :]) + mod_ref[0, 0, 0:1, :]).astype(BF16)
    hc = h[halo:halo + tm, :]
    z_ref[0] = _dot(hc, w_ref[:, :inner]).astype(BF16)
    dtr = _dot(hc, w_ref[:, inner + conv_dim:]) + dtb_ref[...]
    dt_ref[0] = jnp.maximum(dtr, 0.0) + jnp.log1p(jnp.exp(-jnp.abs(dtr)))
    keep_top = jnp.where(j <= 1, 0.0, 1.0)
    keep_bot = jnp.where(jnp.logical_or(j == 0, j == n_tiles - 1), 0.0, 1.0)
    hrow = lax.broadcasted_iota(jnp.int32, (halo, 1), 0)
    pad = (SSD_CONV - 1) // 2
    taps = [k for k in range(SSD_CONV) if k != pad]
    cblk = CONV_COL_BLOCK
    for cbi in range(conv_dim // cblk):
        c0 = cbi * cblk
        u = _dot(h, w_ref[:, inner + c0:inner + c0 + cblk])
        uc = u[halo:halo + tm, :]
        top = u[:halo, :] * keep_top
        bot = u[halo + tm:, :] * keep_bot
        shifted = _dot(sh_ref[...], uc.astype(BF16))
        acc = cb_ref[:, c0:c0 + cblk] + cw_ref[pad:pad + 1, c0:c0 + cblk] * uc
        corr_top = jnp.zeros((halo, cblk), F32)
        corr_bot = jnp.zeros((halo, cblk), F32)
        for jj, k in enumerate(taps):
            wk = cw_ref[k:k + 1, c0:c0 + cblk]
            acc = acc + wk * shifted[jj * tm:(jj + 1) * tm, :]
            d = k - pad
            if d < 0:
                corr_top = corr_top + wk * jnp.where(hrow < -d, pltpu.roll(top, -d, axis=0), 0.0)
            else:
                corr_bot = corr_bot + wk * jnp.where(hrow >= halo - d, pltpu.roll(bot, halo - d, axis=0), 0.0)
        acc = jnp.concatenate([acc[:halo, :] + corr_top, acc[halo:tm - halo, :], acc[tm - halo:, :] + corr_bot],
                              axis=0)
        act = _silu(acc).astype(BF16)
        if c0 < inner:
            xs_ref[0, :, c0:c0 + cblk] = act
        else:
            bc_ref[0, :, c0 - inner:c0 - inner + cblk] = act


def _ssd_in_proj(ctx, x, mods01, g, w, conv_w, conv_b, dt_bias):
    b, l, d = x.shape
    lc = ctx.shape[1]
    tm = TOKEN_TILE
    assert lc == tm and l % tm == 0
    n_tiles = 1 + l // tm
    inner = SSD_HEADS * SSD_HEADDIM
    conv_dim = conv_w.shape[1]
    bc_dim = conv_dim - inner
    halo = CONV_HALO
    blocks_per_tile = tm // halo
    last_halo_block = l // halo - 1
    lt = lc + l
    pad = (SSD_CONV - 1) // 2
    t_idx = jnp.arange(tm, dtype=jnp.int32)
    shifts = jnp.concatenate([t_idx[None, :] == t_idx[:, None] + (k - pad) for k in range(SSD_CONV) if k != pad],
                             axis=0).astype(BF16)
    kern = functools.partial(_ssd_in_kernel, n_tiles=n_tiles, inner=inner, conv_dim=conv_dim)
    return pl.pallas_call(
        kern,
        grid=(b, n_tiles),
        in_specs=[
            pl.BlockSpec((1, lc, d), lambda i, j: (i, 0, 0)),
            pl.BlockSpec((1, tm, d), lambda i, j: (i, jnp.maximum(j - 1, 0), 0)),
            pl.BlockSpec((1, halo, d), lambda i, j: (i, jnp.maximum((j - 1) * blocks_per_tile - 1, 0), 0)),
            pl.BlockSpec((1, halo, d), lambda i, j: (i, jnp.minimum(j * blocks_per_tile, last_halo_block), 0)),
            pl.BlockSpec((1, 1, SUBLANES, d), lambda i, j: (i, jnp.minimum(j, 1), 0, 0)),
            pl.BlockSpec((SUBLANES, d), lambda i, j: (0, 0)),
            pl.BlockSpec(w.shape, lambda i, j: (0, 0), pipeline_mode=pl.Buffered(1)),
            pl.BlockSpec((SUBLANES, conv_dim), lambda i, j: (0, 0)),
            pl.BlockSpec((1, conv_dim), lambda i, j: (0, 0)),
            pl.BlockSpec((1, LANES), lambda i, j: (0, 0)),
            pl.BlockSpec(shifts.shape, lambda i, j: (0, 0)),
        ],
        out_specs=[
            pl.BlockSpec((1, tm, inner), lambda i, j: (i, jnp.maximum(j - 1, 0), 0)),
            pl.BlockSpec((1, tm, inner), lambda i, j: (i, j, 0)),
            pl.BlockSpec((1, tm, bc_dim), lambda i, j: (i, j, 0)),
            pl.BlockSpec((1, tm, LANES), lambda i, j: (i, j, 0)),
        ],
        out_shape=[
            jax.ShapeDtypeStruct((b, l, inner), BF16),
            jax.ShapeDtypeStruct((b, lt, inner), BF16),
            jax.ShapeDtypeStruct((b, lt, bc_dim), BF16),
            jax.ShapeDtypeStruct((b, lt, LANES), F32),
        ],
        compiler_params=_cparams(2, 56),
        name="ssd_in_proj",
    )(ctx, x, x, x, mods01, g, w, conv_w, conv_b, dt_bias, shifts)


SSD_CHUNKS_PER_STEP = 2
SSD_DECAY_ROWS = 16


def _split3(x):
    hi = x.astype(BF16)
    r1 = x - hi.astype(F32)
    mid = r1.astype(BF16)
    lo = (r1 - mid.astype(F32)).astype(BF16)
    return hi, mid, lo


def _ssd_step(rev, is_lat, lblk, xs_ref, bc_ref, dt_ref, alog_ref, e_ref, dskip_ref, y_ref, st_ref, ybuf_ref):
    c = SSD_CHUNK
    nck = SSD_CHUNKS_PER_STEP
    dcol = SSD_HEADS * int(rev)
    gw = SSD_HPG * SSD_HEADDIM
    bc_off = SSD_GROUPS * SSD_STATE
    a_neg = -jnp.exp(alog_ref[int(rev):int(rev) + 1, :])
    li = lax.broadcasted_iota(jnp.int32, (c, c), 0)
    si = lax.broadcasted_iota(jnp.int32, (c, c), 1)
    causal = (si >= li) if rev else (si <= li)
    tri = jnp.where(causal, 1.0, 0.0).astype(BF16)
    e = e_ref[int(rev)]
    head = lax.broadcasted_iota(jnp.int32, (c, LANES), 1)

    dts = [dt_ref[0, k * c:(k + 1) * c, :] for k in range(nck)]
    pieces = []
    for k in range(nck):
        pieces.extend(_split3(dts[k] * a_neg))
    run = _dot(tri, jnp.concatenate(pieces, axis=1))
    css = [run[:, (3 * k) * LANES:(3 * k + 1) * LANES] + run[:, (3 * k + 1) * LANES:(3 * k + 2) * LANES]
           + run[:, (3 * k + 2) * LANES:(3 * k + 3) * LANES] for k in range(nck)]
    tots = [cs[0:1, :] if rev else cs[c - 1:c, :] for cs in css]

    blocks = []
    for k in range(nck):
        blocks.append((jnp.exp(tots[k] - css[k]) * dts[k]).astype(BF16))
        blocks.append(jnp.exp(css[k]).astype(BF16))
    drow = lax.broadcasted_iota(jnp.int32, (SSD_DECAY_ROWS, LANES), 0)
    for k in range(nck):
        hi, mid, lo = _split3(jnp.broadcast_to(jnp.exp(tots[k]), (SSD_DECAY_ROWS, LANES)))
        rows3 = jnp.where(drow == 0, hi.astype(F32),
                          jnp.where(drow == 1, mid.astype(F32), jnp.where(drow == 2, lo.astype(F32), 0.0)))
        blocks.append(rows3.astype(BF16))
    ex = _dot(jnp.concatenate(blocks, axis=0), e).astype(BF16)
    dec0 = 2 * nck * c
    even_cols = jnp.where(head < SSD_HEADDIM, 1.0, 0.0).astype(BF16)
    odd_cols = jnp.where(head < SSD_HEADDIM, 0.0, 1.0).astype(BF16)

    for k in (range(nck - 1, -1, -1) if rev else range(nck)):
        base = 2 * k * c
        xs = xs_ref[0, k * c:(k + 1) * c, :]
        bc = bc_ref[0, k * c:(k + 1) * c, :]
        xw = xs * ex[base:base + c, :]

        @pl.when(is_lat)
        def _(k=k, base=base, xs=xs, bc=bc):
            lc = lblk * nck + k
            cs = css[k]
            rs_t = (cs - jnp.log(dts[k])).T
            for g in range(SSD_GROUPS):
                bg = bc[:, g * SSD_STATE:(g + 1) * SSD_STATE]
                cg = bc[:, bc_off + g * SSD_STATE:bc_off + (g + 1) * SSD_STATE]
                px = ex[base + c:base + 2 * c, g * gw:(g + 1) * gw].astype(F32)
                y_off = _dot(cg, st_ref[:, g * gw:(g + 1) * gw].astype(BF16)) * px
                cb = _dot_nt(cg, bg)
                for hp in range(SSD_HPG // 2):
                    ms = []
                    for hh in range(2):
                        hd = dcol + g * SSD_HPG + 2 * hp + hh
                        diff = cs[:, hd:hd + 1] - rs_t[hd:hd + 1, :]
                        ms.append(cb * jnp.exp(jnp.where(causal, diff, -jnp.inf)))
                    mp = jnp.concatenate(ms, axis=1).astype(BF16)
                    col0 = (g * SSD_HPG + 2 * hp) * SSD_HEADDIM
                    xp = xs[:, col0:col0 + LANES]
                    rhs = jnp.concatenate([xp * even_cols, xp * odd_cols], axis=0)
                    y_pair = _dot(mp, rhs) + y_off[:, 2 * hp * SSD_HEADDIM:2 * hp * SSD_HEADDIM + LANES]
                    if rev:
                        tot_y = (ybuf_ref[lc, :, col0:col0 + LANES] + y_pair
                                 + dskip_ref[:, col0:col0 + LANES] * xp.astype(F32))
                        y_ref[0, k * c:(k + 1) * c, col0:col0 + LANES] = tot_y.astype(BF16)
                    else:
                        ybuf_ref[lc, :, col0:col0 + LANES] = y_pair

        drows = ex[dec0 + k * SSD_DECAY_ROWS:dec0 + (k + 1) * SSD_DECAY_ROWS, :].astype(F32)
        decay = drows[0:1, :] + drows[1:2, :] + drows[2:3, :]
        for g in range(SSD_GROUPS):
            bg = bc[:, g * SSD_STATE:(g + 1) * SSD_STATE]
            upd = _dot_tn(bg, xw[:, g * gw:(g + 1) * gw])
            st_ref[:, g * gw:(g + 1) * gw] = st_ref[:, g * gw:(g + 1) * gw] * decay[:, g * gw:(g + 1) * gw] + upd


def _ssd_scan_kernel(xs_ref, bc_ref, dt_ref, alog_ref, e_ref, dskip_ref, y_ref, st_ref, ybuf_ref, *,
                     n_ctx_blocks, n_blocks):
    d = pl.program_id(1)
    s = pl.program_id(2)

    @pl.when(s == 0)
    def _():
        st_ref[...] = jnp.zeros_like(st_ref)

    for rev in (False, True):
        @pl.when(d == int(rev))
        def _(rev=rev):
            if rev:
                blk = jnp.where(s < n_ctx_blocks, n_ctx_blocks - 1 - s, n_blocks + 2 * n_ctx_blocks - 1 - s)
            else:
                blk = s
            _ssd_step(rev, blk >= n_ctx_blocks, jnp.maximum(blk - n_ctx_blocks, 0), xs_ref, bc_ref, dt_ref,
                      alog_ref, e_ref, dskip_ref, y_ref, st_ref, ybuf_ref)


def _ssd_scan(xs_all, bc_all, dt_all, alog, expand, dskip, n_ctx_chunks, n_chunks):
    b, lt, inner = xs_all.shape
    nck = SSD_CHUNKS_PER_STEP
    rows = nck * SSD_CHUNK
    assert n_ctx_chunks % nck == 0 and n_chunks % nck == 0
    n_ctx_blocks = n_ctx_chunks // nck
    n_blocks = n_chunks // nck
    steps = n_ctx_blocks + n_blocks

    def block_idx(d, s):
        bwd = jnp.where(s < n_ctx_blocks, n_ctx_blocks - 1 - s, steps + n_ctx_blocks - 1 - s)
        return jnp.where(d == 0, s, bwd)

    def out_idx(d, s):
        return jnp.where(jnp.logical_and(d == 1, s >= n_ctx_blocks), steps - 1 - s, n_blocks - 1)

    kern = functools.partial(_ssd_scan_kernel, n_ctx_blocks=n_ctx_blocks, n_blocks=n_blocks)
    return pl.pallas_call(
        kern,
        grid=(b, 2, steps),
        in_specs=[
            pl.BlockSpec((1, rows, inner), lambda i, d, s: (i, block_idx(d, s), 0)),
            pl.BlockSpec((1, rows, bc_all.shape[2]), lambda i, d, s: (i, block_idx(d, s), 0)),
            pl.BlockSpec((1, rows, LANES), lambda i, d, s: (i, block_idx(d, s), 0)),
            pl.BlockSpec((SUBLANES, LANES), lambda i, d, s: (0, 0)),
            pl.BlockSpec((2, LANES, inner), lambda i, d, s: (0, 0, 0)),
            pl.BlockSpec((1, inner), lambda i, d, s: (0, 0)),
        ],
        out_specs=pl.BlockSpec((1, rows, inner), lambda i, d, s: (i, out_idx(d, s), 0)),
        out_shape=jax.ShapeDtypeStruct((b, n_chunks * SSD_CHUNK, inner), BF16),
        scratch_shapes=[pltpu.VMEM((SSD_STATE, inner), F32),
                        pltpu.VMEM((n_chunks, SSD_CHUNK, inner), F32)],
        compiler_params=_cparams(3, 56),
        name="ssd_scan",
    )(xs_all, bc_all, dt_all, alog, expand, dskip)


def _mixer_epilogue(o, x, mod_ref, g_ref, rw_ref):
    xn = x + mod_ref[0, 2:3, :] * _rms(o, g_ref[1:2, :])
    h2 = _rms(xn, g_ref[2:3, :]) * (1.0 + mod_ref[0, 4:5, :]) + mod_ref[0, 3:4, :]
    h_hi = h2.astype(BF16)
    h_lo = (h2 - h_hi.astype(F32)).astype(BF16)
    ne = rw_ref.shape[0] // 2
    both = _dot_nt(rw_ref[...], h_hi)
    lg = both[:ne, :] + both[ne:, :] + _dot_nt(rw_ref[0:ne, :], h_lo)
    ex = jnp.exp(lg - jnp.max(lg, axis=0, keepdims=True))
    return xn, h_hi, ex / jnp.sum(ex, axis=0, keepdims=True)


def _ssd_out_kernel(y_ref, z_ref, x_ref, mod_ref, g_ref, ng_ref, w_ref, rw_ref, xo_ref, h2_ref, aff_ref):
    sub = TOKEN_TILE
    gdim = y_ref.shape[2] // SSD_GROUPS
    for r in range(y_ref.shape[1] // sub):
        rs = slice(r * sub, (r + 1) * sub)
        z = z_ref[0, rs, :].astype(F32)
        yg = y_ref[0, rs, :].astype(F32) * _silu(z)
        parts = [_rms(yg[:, g * gdim:(g + 1) * gdim], ng_ref[:, g * gdim:(g + 1) * gdim]).astype(BF16)
                 for g in range(SSD_GROUPS)]
        o = _dot(jnp.concatenate(parts, axis=1), w_ref[...])
        xn, hb, aff = _mixer_epilogue(o, x_ref[0, rs, :], mod_ref, g_ref, rw_ref)
        xo_ref[0, rs, :] = xn
        h2_ref[0, rs, :] = hb
        aff_ref[0, :, rs] = aff


def _epilogue_outs(b, l, d, tm):
    out_specs = [pl.BlockSpec((1, tm, d), lambda i, j: (i, j, 0)),
                 pl.BlockSpec((1, tm, d), lambda i, j: (i, j, 0)),
                 pl.BlockSpec((1, N_EXPERTS, tm), lambda i, j: (i, 0, j))]
    out_shape = [jax.ShapeDtypeStruct((b, l, d), F32),
                 jax.ShapeDtypeStruct((b, l, d), BF16),
                 jax.ShapeDtypeStruct((b, N_EXPERTS, l), F32)]
    return out_specs, out_shape


def _ssd_out(y, z, x, mods, g, ng, w, rw2):
    b, l, d = x.shape
    inner = y.shape[2]
    tm = 2 * TOKEN_TILE
    out_specs, out_shape = _epilogue_outs(b, l, d, tm)
    return pl.pallas_call(
        _ssd_out_kernel,
        grid=(b, l // tm),
        in_specs=[
            pl.BlockSpec((1, tm, inner), lambda i, j: (i, j, 0)),
            pl.BlockSpec((1, tm, inner), lambda i, j: (i, j, 0)),
            pl.BlockSpec((1, tm, d), lambda i, j: (i, j, 0)),
            pl.BlockSpec((1, SUBLANES, d), lambda i, j: (i, 0, 0)),
            pl.BlockSpec((SUBLANES, d), lambda i, j: (0, 0)),
            pl.BlockSpec((1, inner), lambda i, j: (0, 0)),
            pl.BlockSpec(w.shape, lambda i, j: (0, 0)),
            pl.BlockSpec(rw2.shape, lambda i, j: (0, 0)),
        ],
        out_specs=out_specs,
        out_shape=out_shape,
        compiler_params=_cparams(2, 48),
        name="ssd_out",
    )(y, z, x, mods, g, ng, w, rw2)


def _epilogue_kernel(o_ref, x_ref, mod_ref, g_ref, rw_ref, xo_ref, h2_ref, aff_ref):
    sub = TOKEN_TILE
    for r in range(o_ref.shape[1] // sub):
        rs = slice(r * sub, (r + 1) * sub)
        xn, hb, aff = _mixer_epilogue(o_ref[0, rs, :], x_ref[0, rs, :], mod_ref, g_ref, rw_ref)
        xo_ref[0, rs, :] = xn
        h2_ref[0, rs, :] = hb
        aff_ref[0, :, rs] = aff


def _epilogue(o, x, mods, g, rw_t):
    b, l, d = x.shape
    tm = 2 * TOKEN_TILE
    out_specs, out_shape = _epilogue_outs(b, l, d, tm)
    return pl.pallas_call(
        _epilogue_kernel,
        grid=(b, l // tm),
        in_specs=[
            pl.BlockSpec((1, tm, d), lambda i, j: (i, j, 0)),
            pl.BlockSpec((1, tm, d), lambda i, j: (i, j, 0)),
            pl.BlockSpec((1, SUBLANES, d), lambda i, j: (i, 0, 0)),
            pl.BlockSpec((SUBLANES, d), lambda i, j: (0, 0)),
            pl.BlockSpec(rw_t.shape, lambda i, j: (0, 0)),
        ],
        out_specs=out_specs,
        out_shape=out_shape,
        compiler_params=_cparams(2, 40),
        name="mixer_epilogue",
    )(o, x, mods, g, rw_t)


def _lane_prefix_exclusive(m01, upper):
    e, t = m01.shape
    carry = jnp.zeros((e, 1), F32)
    outs = []
    for k in range(t // LANES):
        tile = m01[:, k * LANES:(k + 1) * LANES]
        incl = _dot(tile.astype(BF16), upper)
        outs.append(incl - tile + carry)
        carry = carry + incl[:, LANES - 1:LANES]
    return jnp.concatenate(outs, axis=1)


def _route(aff, cap):
    e, t = aff.shape
    key = pltpu.bitcast(aff, jnp.int32)

    def enough(cand):
        return jnp.sum(jnp.where(key >= cand, 1.0, 0.0), axis=1, keepdims=True) >= cap

    def body(i, thr):
        sh = 29 - 2 * i
        c1, c2, c3 = (jnp.bitwise_or(thr, jnp.left_shift(jnp.int32(v), sh)) for v in (1, 2, 3))
        return jnp.where(enough(c3), c3, jnp.where(enough(c2), c2, jnp.where(enough(c1), c1, thr)))

    thr = lax.fori_loop(0, 15, body, jnp.zeros((e, 1), jnp.int32))
    last = jnp.bitwise_or(thr, 1)
    thr = jnp.where(enough(last), last, thr)
    gt = jnp.where(key > thr, 1.0, 0.0)
    eq = jnp.where(key == thr, 1.0, 0.0)
    need = cap - jnp.sum(gt, axis=1, keepdims=True)
    r = lax.broadcasted_iota(jnp.int32, (LANES, LANES), 0)
    cidx = lax.broadcasted_iota(jnp.int32, (LANES, LANES), 1)
    upper = jnp.where(r <= cidx, 1.0, 0.0).astype(BF16)
    eq_rank = _lane_prefix_exclusive(eq, upper)
    sel = gt + eq * jnp.where(eq_rank < need, 1.0, 0.0)
    pos = _lane_prefix_exclusive(sel, upper)
    return jnp.where(sel > 0.5, pos, -1.0).astype(jnp.int32)


MOE_EXPERTS_PER_STEP = 4
MOE_FFN_TILE = 1024


def _pick_rows(slot_ref, e, cap, t):
    srow = slot_ref[0, pl.ds(e, 1), :]
    return lax.broadcasted_iota(jnp.int32, (cap, t), 0) == srow


def _moe_gather_kernel(h_ref, aff_ref, xin_ref, slot_ref, *, cap):
    eg = pl.program_id(1)

    @pl.when(eg == 0)
    def _():
        slot_ref[0] = _route(aff_ref[0], cap)

    t = h_ref.shape[1]
    ps = [jnp.where(_pick_rows(slot_ref, eg * MOE_EXPERTS_PER_STEP + i, cap, t), 1.0, 0.0).astype(BF16)
          for i in range(MOE_EXPERTS_PER_STEP)]
    xin = _dot(jnp.concatenate(ps, axis=0), h_ref[0])
    for i in range(MOE_EXPERTS_PER_STEP):
        xin_ref[i] = xin[i * cap:(i + 1) * cap, :].astype(BF16)


def _moe_gather(h2, aff_t):
    b, t, d = h2.shape
    ne = aff_t.shape[1]
    cap = (CAPACITY_FACTOR * t) // ne
    g = MOE_EXPERTS_PER_STEP
    return pl.pallas_call(
        functools.partial(_moe_gather_kernel, cap=cap),
        grid=(b, ne // g),
        in_specs=[pl.BlockSpec((1, t, d), lambda i, e: (i, 0, 0)),
                  pl.BlockSpec((1, ne, t), lambda i, e: (i, 0, 0))],
        out_specs=[pl.BlockSpec((g, cap, d), lambda i, e: (e, i, 0)),
                   pl.BlockSpec((1, ne, t), lambda i, e: (i, 0, 0))],
        out_shape=[jax.ShapeDtypeStruct((ne, b * cap, d), BF16),
                   jax.ShapeDtypeStruct((b, ne, t), jnp.int32)],
        compiler_params=_cparams(2, 48),
        name="moe_gather",
    )(h2, aff_t)


def _moe_ffn_kernel(x_ref, wg_ref, wu_ref, wd_ref, y_ref, acc_ref):
    half = pl.program_id(2)
    x = x_ref[0]
    hid = _dot(x, wg_ref[0, 0].astype(BF16))
    hid = _silu(hid) * _dot(x, wu_ref[0, 0].astype(BF16))
    part = _dot(hid.astype(BF16), wd_ref[0, 0].astype(BF16))

    @pl.when(half == 0)
    def _():
        acc_ref[...] = part

    @pl.when(half == 1)
    def _():
        y_ref[0] = (acc_ref[...] + part).astype(BF16)


def _moe_ffn(xin, w_gate, w_up, w_down, layer):
    ne, m, d = xin.shape
    ff = w_gate.shape[3]
    tm = MOE_FFN_TILE
    hf = ff // 2
    return pl.pallas_call(
        _moe_ffn_kernel,
        grid=(ne, m // tm, 2),
        in_specs=[pl.BlockSpec((1, tm, d), lambda e, j, h: (e, j, 0)),
                  pl.BlockSpec((1, 1, d, hf), lambda e, j, h: (layer, e, 0, h)),
                  pl.BlockSpec((1, 1, d, hf), lambda e, j, h: (layer, e, 0, h)),
                  pl.BlockSpec((1, 1, hf, d), lambda e, j, h: (layer, e, h, 0))],
        out_specs=pl.BlockSpec((1, tm, d), lambda e, j, h: (e, j, 0)),
        out_shape=jax.ShapeDtypeStruct((ne, m, d), BF16),
        scratch_shapes=[pltpu.VMEM((tm, d), F32)],
        compiler_params=_cparams(3, 56),
        name="moe_ffn",
    )(xin, w_gate, w_up, w_down)


def _moe_scatter_kernel(slot_ref, aff_ref, y_ref, x_ref, mod_ref, g_ref, o_ref, xst_ref, *, cap):
    eg = pl.program_id(1)
    t = o_ref.shape[1]
    q = x_ref.shape[1]
    xst_ref[pl.ds(pl.multiple_of(eg * q, q), q), :] = x_ref[0]
    ps = []
    for i in range(MOE_EXPERTS_PER_STEP):
        e = eg * MOE_EXPERTS_PER_STEP + i
        ps.append(jnp.where(_pick_rows(slot_ref, e, cap, t), aff_ref[0, pl.ds(e, 1), :], 0.0).astype(BF16))
    p = jnp.concatenate(ps, axis=0)
    y = y_ref[...].reshape(MOE_EXPERTS_PER_STEP * cap, y_ref.shape[2])
    contrib = _dot_tn(p, y)

    @pl.when(eg == 0)
    def _():
        o_ref[0] = contrib

    @pl.when(eg > 0)
    def _():
        o_ref[0] += contrib

    @pl.when(eg == pl.num_programs(1) - 1)
    def _():
        sub = TOKEN_TILE
        for r in range(t // sub):
            rs = slice(r * sub, (r + 1) * sub)
            o_ref[0, rs, :] = xst_ref[rs, :] + mod_ref[0, 5:6, :] * _rms(o_ref[0, rs, :], g_ref[3:4, :])


def _moe_scatter(slot, aff_t, y, x, mods, g):
    ne, m, d = y.shape
    b, _, t = slot.shape
    cap = m // b
    grp = MOE_EXPERTS_PER_STEP
    steps = ne // grp
    return pl.pallas_call(
        functools.partial(_moe_scatter_kernel, cap=cap),
        grid=(b, steps),
        in_specs=[pl.BlockSpec((1, ne, t), lambda i, e: (i, 0, 0)),
                  pl.BlockSpec((1, ne, t), lambda i, e: (i, 0, 0)),
                  pl.BlockSpec((grp, cap, d), lambda i, e: (e, i, 0)),
                  pl.BlockSpec((1, t // steps, d), lambda i, e: (i, e, 0)),
                  pl.BlockSpec((1, SUBLANES, d), lambda i, e: (i, 0, 0)),
                  pl.BlockSpec((SUBLANES, d), lambda i, e: (0, 0))],
        out_specs=pl.BlockSpec((1, t, d), lambda i, e: (i, 0, 0)),
        out_shape=jax.ShapeDtypeStruct((b, t, d), F32),
        scratch_shapes=[pltpu.VMEM((t, d), F32)],
        compiler_params=_cparams(2, 56),
        name="moe_scatter",
    )(slot, aff_t, y, x, mods, g)


def _moe(h2, aff_t, w_gate, w_up, w_down, layer, x, mods, g):
    xin, slot = _moe_gather(h2, aff_t)
    y = _moe_ffn(xin, w_gate, w_up, w_down, layer)
    return _moe_scatter(slot, aff_t, y, x, mods, g)


GELU_C1 = 0.7978845608028654
GELU_C2 = GELU_C1 * 0.044715


def _mlp_in_kernel(x_ref, mod_ref, g_ref, w_ref, vg_ref, u_ref, v_ref):
    h = _rms(x_ref[0], g_ref[0:1, :]) * (1.0 + mod_ref[0, 1:2, :]) + mod_ref[0, 0:1, :]
    r = _dot(h.astype(BF16), w_ref[...])
    hr = 0.5 * r
    ge = hr + hr * jnp.tanh(r * (GELU_C1 + GELU_C2 * (r * r)))
    half = ge.shape[1] // 2
    u_ref[0] = ge[:, :half].astype(BF16)
    v = ge[:, half:]
    vc = v - jnp.mean(v, axis=-1, keepdims=True)
    vn = vc * lax.rsqrt(jnp.mean(vc * vc, axis=-1, keepdims=True) + EPS) * vg_ref[...]
    v_ref[0] = vn.astype(BF16)


def _mlp_in(x, mods, g, w, vg):
    b, l, d = x.shape
    tm = TOKEN_TILE
    half = w.shape[1] // 2
    tok = lambda i, j: (i, j, 0)
    return pl.pallas_call(
        _mlp_in_kernel,
        grid=(b, l // tm),
        in_specs=[
            pl.BlockSpec((1, tm, d), tok),
            pl.BlockSpec((1, SUBLANES, d), lambda i, j: (i, 0, 0)),
            pl.BlockSpec((SUBLANES, d), lambda i, j: (0, 0)),
            pl.BlockSpec(w.shape, lambda i, j: (0, 0), pipeline_mode=pl.Buffered(1)),
            pl.BlockSpec((1, half), lambda i, j: (0, 0)),
        ],
        out_specs=[pl.BlockSpec((1, tm, half), tok),
                   pl.BlockSpec((1, tm, half), tok)],
        out_shape=[jax.ShapeDtypeStruct((b, l, half), BF16),
                   jax.ShapeDtypeStruct((b, l, half), BF16)],
        compiler_params=_cparams(2, 48),
        name="mlp_in_proj",
    )(x, mods, g, w, vg)


MLP_GROUPS_PER_STEP = 4
MLP_COL_PITCH = GRID_W + SUBLANES


def _mlp_mix_kernel(u_ref, v_ref, ws_ref, bs_ref, w_ref, o_ref, vf_ref, sf_ref, us_ref, *, rows):
    gp = pl.program_id(1)
    t = u_ref.shape[1]
    n_chunks = t // MLP_CHUNK
    per_step = ws_ref.shape[0]
    cols_per_chunk = MLP_CHUNK // rows
    pitch = MLP_COL_PITCH

    @pl.when(gp < MLP_ROW_GROUPS // per_step)
    def _():
        for i in range(per_step):
            cs = slice(i * LANES, (i + 1) * LANES)
            for c in range(n_chunks):
                rs = slice(c * MLP_CHUNK, (c + 1) * MLP_CHUNK)
                s = _dot(ws_ref[i], v_ref[0, rs, cs]) + bs_ref[i]
                us_ref[rs, cs] = (u_ref[0, rs, cs].astype(F32) * s).astype(BF16)

    @pl.when(gp >= MLP_ROW_GROUPS // per_step)
    def _():
        for i in range(per_step):
            cs = slice(i * LANES, (i + 1) * LANES)
            for r in range(rows):
                vf_ref[r * pitch:r * pitch + GRID_W, :] = v_ref[0, r * GRID_W:(r + 1) * GRID_W, cs].astype(F32)
            for k in range(n_chunks):
                xk = jnp.concatenate(
                    [vf_ref[pl.ds(k * cols_per_chunk + wl, rows, stride=pitch), :] for wl in range(cols_per_chunk)],
                    axis=0)
                s = _dot(ws_ref[i], xk.astype(BF16)) + bs_ref[i]
                for wl in range(cols_per_chunk):
                    sf_ref[pl.ds(k * cols_per_chunk + wl, rows, stride=pitch), :] = s[wl * rows:(wl + 1) * rows, :]
            for r in range(rows):
                rr = slice(r * GRID_W, (r + 1) * GRID_W)
                us_ref[rr, cs] = (u_ref[0, rr, cs].astype(F32) * sf_ref[r * pitch:r * pitch + GRID_W, :]).astype(BF16)

    part = _dot(us_ref[...], w_ref[...])

    @pl.when(gp == 0)
    def _():
        o_ref[0] = part

    @pl.when(gp > 0)
    def _():
        o_ref[0] += part


def _mlp_mix(u, v, ws, bsb, w):
    b, l, inner = u.shape
    d = w.shape[1]
    per_step = MLP_GROUPS_PER_STEP
    steps = MLP_GROUPS // per_step
    kc = per_step * LANES
    rows = l // GRID_W
    return pl.pallas_call(
        functools.partial(_mlp_mix_kernel, rows=rows),
        grid=(b, steps),
        in_specs=[
            pl.BlockSpec((1, l, kc), lambda i, g: (i, 0, g)),
            pl.BlockSpec((1, l, kc), lambda i, g: (i, 0, g)),
            pl.BlockSpec((per_step, MLP_CHUNK, MLP_CHUNK), lambda i, g: (g, 0, 0)),
            pl.BlockSpec((per_step, MLP_CHUNK, LANES), lambda i, g: (g, 0, 0)),
            pl.BlockSpec((kc, d), lambda i, g: (g, 0)),
        ],
        out_specs=pl.BlockSpec((1, l, d), lambda i, g: (i, 0, 0)),
        out_shape=jax.ShapeDtypeStruct((b, l, d), F32),
        scratch_shapes=[pltpu.VMEM((rows * MLP_COL_PITCH, LANES), F32),
                        pltpu.VMEM((rows * MLP_COL_PITCH, LANES), F32),
                        pltpu.VMEM((l, kc), BF16)],
        compiler_params=_cparams(2, 48),
        name="mlp_mix",
    )(u, v, ws, bsb, w)


def _pad_rows(a, rows):
    return jnp.pad(a, ((0, rows - a.shape[0]),) + ((0, 0),) * (a.ndim - 1))


def kernel(x, c, ctx, c_ctx, mod_w, mod_b, norm_g, ssd_in_w, ssd_conv_w, ssd_conv_b, ssd_dt_bias, ssd_a_log,
           ssd_d, ssd_norm_g, ssd_out_w, mlp_in_w, mlp_v_g, mlp_ws, mlp_bs, mlp_out_w, router_w, exp_w_gate,
           exp_w_up, exp_w_down):
    b, l, d = x.shape
    inner = SSD_HEADS * SSD_HEADDIM
    n_chunks = l // SSD_CHUNK
    n_ctx_chunks = ctx.shape[1] // SSD_CHUNK

    crows = _pad_rows(jnp.concatenate([c, c_ctx[None, :]], axis=0), -(-(b + 1) // SUBLANES) * SUBLANES)
    mod = _modulation(crows, mod_w, mod_b)
    mods_lat = [jnp.pad(mod[i, :b].reshape(b, N_MOD, d), ((0, 0), (0, SUBLANES - N_MOD), (0, 0)))
                for i in range(2)]
    mods_ctx = jnp.pad(mod[0, b].reshape(1, N_MOD, d), ((0, 0), (0, SUBLANES - N_MOD), (0, 0)))
    mods01 = jnp.stack([jnp.broadcast_to(mods_ctx, (b, SUBLANES, d)), mods_lat[0]], axis=1)
    gains = [_pad_rows(norm_g[i], SUBLANES) for i in range(2)]
    rw2 = []
    for i in range(2):
        rw_t = router_w[i].T
        rw_hi = rw_t.astype(BF16)
        rw2.append(jnp.concatenate([rw_hi, (rw_t - rw_hi.astype(F32)).astype(BF16)], axis=0))

    in_w = ssd_in_w[0]
    conv_dim = ssd_conv_w.shape[2]
    w_in = jnp.pad(in_w, ((0, 0), (0, LANES - 2 * SSD_HEADS))).astype(BF16)
    dt_bias = jnp.pad(ssd_dt_bias[0].reshape(1, 2 * SSD_HEADS), ((0, 0), (0, LANES - 2 * SSD_HEADS)))
    z, xs_all, bc_all, dt_all = _ssd_in_proj(
        ctx, x, mods01, gains[0], w_in, _pad_rows(ssd_conv_w[0], SUBLANES), ssd_conv_b[0].reshape(1, conv_dim),
        dt_bias)

    alog = _pad_rows(jnp.stack([jnp.pad(ssd_a_log[0, 0], (0, LANES - SSD_HEADS)),
                                jnp.pad(ssd_a_log[0, 1], (SSD_HEADS, LANES - 2 * SSD_HEADS))]), SUBLANES)
    head_of_col = jnp.arange(inner, dtype=jnp.int32) // SSD_HEADDIM
    rows128 = jnp.arange(LANES, dtype=jnp.int32)[:, None]
    expand = jnp.stack([(rows128 == head_of_col[None, :] + SSD_HEADS * dd) for dd in range(2)]).astype(BF16)
    dskip = jnp.repeat(ssd_d[0], SSD_HEADDIM).reshape(1, inner)
    y = _ssd_scan(xs_all, bc_all, dt_all, alog, expand, dskip, n_ctx_chunks, n_chunks)

    xa, h2, aff_t = _ssd_out(y, z, x, mods_lat[0], gains[0], ssd_norm_g[0].reshape(1, inner),
                             ssd_out_w[0].astype(BF16), rw2[0])
    x1 = _moe(h2, aff_t, exp_w_gate, exp_w_up, exp_w_down, 0, xa, mods_lat[0], gains[0])

    u, v = _mlp_in(x1, mods_lat[1], gains[1], mlp_in_w[0].astype(BF16), mlp_v_g[0].reshape(1, -1))
    bsb = jnp.broadcast_to(mlp_bs[0][:, :, None], (MLP_GROUPS, MLP_CHUNK, LANES))
    o1 = _mlp_mix(u, v, mlp_ws[0].astype(BF16), bsb, mlp_out_w[0].astype(BF16))
    xb, h2b, aff_tb = _epilogue(o1, x1, mods_lat[1], gains[1], rw2[1])
    return _moe(h2b, aff_tb, exp_w_gate, exp_w_up, exp_w_down, 1, xb, mods_lat[1], gains[1])
```

```python
import functools

import jax
import jax.numpy as jnp
from jax import lax
from jax.experimental import pallas as pl
from jax.experimental.pallas import tpu as pltpu

F32 = jnp.float32
BF16 = jnp.bfloat16
HIGHEST = lax.Precision.HIGHEST
EPS = 1e-6

LANES = 128
SUBLANES = 8
MIB = 1024 * 1024

N_MOD = 6
GRID_W = 64
SSD_HEADDIM = 64
SSD_HEADS = 32
SSD_GROUPS = 4
SSD_HPG = SSD_HEADS // SSD_GROUPS
SSD_STATE = 128
SSD_CONV = 5
SSD_CHUNK = 128
MLP_CHUNK = 128
MLP_GROUPS = 16
MLP_ROW_GROUPS = 8
N_EXPERTS = 16
CAPACITY_FACTOR = 2

CONV_HALO = SUBLANES
TOKEN_TILE = 256
CONV_COL_BLOCK = 1024


def _cparams(n_axes, vmem_mib):
    return pltpu.CompilerParams(dimension_semantics=("arbitrary",) * n_axes,
                                vmem_limit_bytes=vmem_mib * MIB)


def _silu(x):
    h = 0.5 * x
    return h + h * jnp.tanh(h)


def _rms(x, g):
    return x * lax.rsqrt(jnp.mean(x * x, axis=-1, keepdims=True) + EPS) * g


def _dot(a, b):
    return jnp.dot(a, b, preferred_element_type=F32)


def _dot_nt(a, b, precision=None):
    return lax.dot_general(a, b, (((1,), (1,)), ((), ())), preferred_element_type=F32, precision=precision)


def _dot_tn(a, b):
    return lax.dot_general(a, b, (((0,), (0,)), ((), ())), preferred_element_type=F32)


def _mod_kernel(c_ref, w_ref, b_ref, o_ref):
    c = c_ref[...]
    s = _silu(c)
    o_ref[0] = jnp.dot(s, w_ref[0], preferred_element_type=F32, precision=HIGHEST) + b_ref[0]


def _modulation(crows, mod_w, mod_b):
    depth, d, n = mod_w.shape
    rows = crows.shape[0]
    tn = 1536
    return pl.pallas_call(
        _mod_kernel,
        grid=(depth, n // tn),
        in_specs=[pl.BlockSpec((rows, d), lambda i, j: (0, 0)),
                  pl.BlockSpec((1, d, tn), lambda i, j: (i, 0, j)),
                  pl.BlockSpec((1, 1, tn), lambda i, j: (i, 0, j))],
        out_specs=pl.BlockSpec((1, rows, tn), lambda i, j: (i, 0, j)),
        out_shape=jax.ShapeDtypeStruct((depth, rows, n), F32),
        compiler_params=_cparams(2, 40),
        name="modulation",
    )(crows, mod_w, mod_b.reshape(depth, 1, n))


def _ssd_in_kernel(ctx_ref, x_ref, xp_ref, xn_ref, mod_ref, g_ref, w_ref, cw_ref, cb_ref, dtb_ref, sh_ref,
                   z_ref, xs_ref, bc_ref, dt_ref, *, n_tiles, inner, conv_dim):
    j = pl.program_id(1)
    tm = x_ref.shape[1]
    halo = xp_ref.shape[1]
    xc = jnp.where(j == 0, ctx_ref[0], x_ref[0])
    xa = jnp.concatenate([xp_ref[0], xc, xn_ref[0]], axis=0)
    h = (_rms(xa, g_ref[0:1, :]) * (1.0 + mod_ref[0, 0, 1:2, :]) + mod_ref[0, 0, 0:1, :]).astype(BF16)
    hc = h[halo:halo + tm, :]
    z_ref[0] = _dot(hc, w_ref[:, :inner]).astype(BF16)
    dtr = _dot(hc, w_ref[:, inner + conv_dim:]) + dtb_ref[...]
    dt_ref[0] = jnp.maximum(dtr, 0.0) + jnp.log1p(jnp.exp(-jnp.abs(dtr)))
    keep_top = jnp.where(j <= 1, 0.0, 1.0)
    keep_bot = jnp.where(jnp.logical_or(j == 0, j == n_tiles - 1), 0.0, 1.0)
    hrow = lax.broadcasted_iota(jnp.int32, (halo, 1), 0)
    pad = (SSD_CONV - 1) // 2
    taps = [k for k in range(SSD_CONV) if k != pad]
    cblk = CONV_COL_BLOCK
    for cbi in range(conv_dim // cblk):
        c0 = cbi * cblk
        u = _dot(h, w_ref[:, inner + c0:inner + c0 + cblk])
        uc = u[halo:halo + tm, :]
        top = u[:halo, :] * keep_top
        bot = u[halo + tm:, :] * keep_bot
        shifted = _dot(sh_ref[...], uc.astype(BF16))
        acc = cb_ref[:, c0:c0 + cblk] + cw_ref[pad:pad + 1, c0:c0 + cblk] * uc
        corr_top = jnp.zeros((halo, cblk), F32)
        corr_bot = jnp.zeros((halo, cblk), F32)
        for jj, k in enumerate(taps):
            wk = cw_ref[k:k + 1, c0:c0 + cblk]
            acc = acc + wk * shifted[jj * tm:(jj + 1) * tm, :]
            d = k - pad
            if d < 0:
                corr_top = corr_top + wk * jnp.where(hrow < -d, pltpu.roll(top, -d, axis=0), 0.0)
            else:
                corr_bot = corr_bot + wk * jnp.where(hrow >= halo - d, pltpu.roll(bot, halo - d, axis=0), 0.0)
        acc = jnp.concatenate([acc[:halo, :] + corr_top, acc[halo:tm - halo, :], acc[tm - halo:, :] + corr_bot],
                              axis=0)
        act = _silu(acc).astype(BF16)
        if c0 < inner:
            xs_ref[0, :, c0:c0 + cblk] = act
        else:
            bc_ref[0, :, c0 - inner:c0 - inner + cblk] = act


def _ssd_in_proj(ctx, x, mods01, g, w, conv_w, conv_b, dt_bias):
    b, l, d = x.shape
    lc = ctx.shape[1]
    tm = TOKEN_TILE
    assert lc == tm and l % tm == 0
    n_tiles = 1 + l // tm
    inner = SSD_HEADS * SSD_HEADDIM
    conv_dim = conv_w.shape[1]
    bc_dim = conv_dim - inner
    halo = CONV_HALO
    blocks_per_tile = tm // halo
    last_halo_block = l // halo - 1
    lt = lc + l
    pad = (SSD_CONV - 1) // 2
    t_idx = jnp.arange(tm, dtype=jnp.int32)
    shifts = jnp.concatenate([t_idx[None, :] == t_idx[:, None] + (k - pad) for k in range(SSD_CONV) if k != pad],
                             axis=0).astype(BF16)
    kern = functools.partial(_ssd_in_kernel, n_tiles=n_tiles, inner=inner, conv_dim=conv_dim)
    return pl.pallas_call(
        kern,
        grid=(b, n_tiles),
        in_specs=[
            pl.BlockSpec((1, lc, d), lambda i, j: (i, 0, 0)),
            pl.BlockSpec((1, tm, d), lambda i, j: (i, jnp.maximum(j - 1, 0), 0)),
            pl.BlockSpec((1, halo, d), lambda i, j: (i, jnp.maximum((j - 1) * blocks_per_tile - 1, 0), 0)),
            pl.BlockSpec((1, halo, d), lambda i, j: (i, jnp.minimum(j * blocks_per_tile, last_halo_block), 0)),
            pl.BlockSpec((1, 1, SUBLANES, d), lambda i, j: (i, jnp.minimum(j, 1), 0, 0)),
            pl.BlockSpec((SUBLANES, d), lambda i, j: (0, 0)),
            pl.BlockSpec(w.shape, lambda i, j: (0, 0), pipeline_mode=pl.Buffered(1)),
            pl.BlockSpec((SUBLANES, conv_dim), lambda i, j: (0, 0)),
            pl.BlockSpec((1, conv_dim), lambda i, j: (0, 0)),
            pl.BlockSpec((1, LANES), lambda i, j: (0, 0)),
            pl.BlockSpec(shifts.shape, lambda i, j: (0, 0)),
        ],
        out_specs=[
            pl.BlockSpec((1, tm, inner), lambda i, j: (i, jnp.maximum(j - 1, 0), 0)),
            pl.BlockSpec((1, tm, inner), lambda i, j: (i, j, 0)),
            pl.BlockSpec((1, tm, bc_dim), lambda i, j: (i, j, 0)),
            pl.BlockSpec((1, tm, LANES), lambda i, j: (i, j, 0)),
        ],
        out_shape=[
            jax.ShapeDtypeStruct((b, l, inner), BF16),
            jax.ShapeDtypeStruct((b, lt, inner), BF16),
            jax.ShapeDtypeStruct((b, lt, bc_dim), BF16),
            jax.ShapeDtypeStruct((b, lt, LANES), F32),
        ],
        compiler_params=_cparams(2, 56),
        name="ssd_in_proj",
    )(ctx, x, x, x, mods01, g, w, conv_w, conv_b, dt_bias, shifts)


SSD_CHUNKS_PER_STEP = 2
SSD_DECAY_ROWS = 16


def _split3(x):
    hi = x.astype(BF16)
    r1 = x - hi.astype(F32)
    mid = r1.astype(BF16)
    lo = (r1 - mid.astype(F32)).astype(BF16)
    return hi, mid, lo


def _ssd_step(rev, lat, lblk, xs_ref, bc_ref, dt_ref, alog_ref, e_ref, dskip_ref, y_ref, st_ref, ybuf_ref):
    c = SSD_CHUNK
    nck = SSD_CHUNKS_PER_STEP
    dcol = SSD_HEADS * int(rev)
    gw = SSD_HPG * SSD_HEADDIM
    bc_off = SSD_GROUPS * SSD_STATE
    a_neg = -jnp.exp(alog_ref[int(rev):int(rev) + 1, :])
    li = lax.broadcasted_iota(jnp.int32, (c, c), 0)
    si = lax.broadcasted_iota(jnp.int32, (c, c), 1)
    causal = (si >= li) if rev else (si <= li)
    tri = jnp.where(causal, 1.0, 0.0).astype(BF16)
    e = e_ref[int(rev)]
    head = lax.broadcasted_iota(jnp.int32, (c, LANES), 1)

    dts = [dt_ref[0, k * c:(k + 1) * c, :] for k in range(nck)]
    pieces = []
    for k in range(nck):
        pieces.extend(_split3(dts[k] * a_neg))
    run = _dot(tri, jnp.concatenate(pieces, axis=1))
    css = [run[:, (3 * k) * LANES:(3 * k + 1) * LANES] + run[:, (3 * k + 1) * LANES:(3 * k + 2) * LANES]
           + run[:, (3 * k + 2) * LANES:(3 * k + 3) * LANES] for k in range(nck)]
    tots = [cs[0:1, :] if rev else cs[c - 1:c, :] for cs in css]

    blocks = []
    for k in range(nck):
        blocks.append((jnp.exp(tots[k] - css[k]) * dts[k]).astype(BF16))
        blocks.append(jnp.exp(css[k]).astype(BF16))
    drow = lax.broadcasted_iota(jnp.int32, (SSD_DECAY_ROWS, LANES), 0)
    for k in range(nck):
        hi, mid, lo = _split3(jnp.broadcast_to(jnp.exp(tots[k]), (SSD_DECAY_ROWS, LANES)))
        rows3 = jnp.where(drow == 0, hi.astype(F32),
                          jnp.where(drow == 1, mid.astype(F32), jnp.where(drow == 2, lo.astype(F32), 0.0)))
        blocks.append(rows3.astype(BF16))
    ex = _dot(jnp.concatenate(blocks, axis=0), e).astype(BF16)
    dec0 = 2 * nck * c
    even_cols = jnp.where(head < SSD_HEADDIM, 1.0, 0.0).astype(BF16)
    odd_cols = jnp.where(head < SSD_HEADDIM, 0.0, 1.0).astype(BF16)

    for k in (range(nck - 1, -1, -1) if rev else range(nck)):
        base = 2 * k * c
        xs = xs_ref[0, k * c:(k + 1) * c, :]
        bc = bc_ref[0, k * c:(k + 1) * c, :]
        xw = xs * ex[base:base + c, :]

        def y_part(k=k, base=base, xs=xs, bc=bc):
            lc = lblk * nck + k
            cs = css[k]
            rs_t = (cs - jnp.log(dts[k])).T
            for g in range(SSD_GROUPS):
                bg = bc[:, g * SSD_STATE:(g + 1) * SSD_STATE]
                cg = bc[:, bc_off + g * SSD_STATE:bc_off + (g + 1) * SSD_STATE]
                px = ex[base + c:base + 2 * c, g * gw:(g + 1) * gw].astype(F32)
                y_off = _dot(cg, st_ref[:, g * gw:(g + 1) * gw].astype(BF16)) * px
                cb = _dot_nt(cg, bg)
                for hp in range(SSD_HPG // 2):
                    ms = []
                    for hh in range(2):
                        hd = dcol + g * SSD_HPG + 2 * hp + hh
                        diff = cs[:, hd:hd + 1] - rs_t[hd:hd + 1, :]
                        ms.append(cb * jnp.exp(jnp.where(causal, diff, -jnp.inf)))
                    mp = jnp.concatenate(ms, axis=1).astype(BF16)
                    col0 = (g * SSD_HPG + 2 * hp) * SSD_HEADDIM
                    xp = xs[:, col0:col0 + LANES]
                    rhs = jnp.concatenate([xp * even_cols, xp * odd_cols], axis=0)
                    y_pair = _dot(mp, rhs) + y_off[:, 2 * hp * SSD_HEADDIM:2 * hp * SSD_HEADDIM + LANES]
                    if rev:
                        tot_y = (ybuf_ref[lc, :, col0:col0 + LANES] + y_pair
                                 + dskip_ref[:, col0:col0 + LANES] * xp.astype(F32))
                        y_ref[0, k * c:(k + 1) * c, col0:col0 + LANES] = tot_y.astype(BF16)
                    else:
                        ybuf_ref[lc, :, col0:col0 + LANES] = y_pair

        if lat:
            y_part()

        drows = ex[dec0 + k * SSD_DECAY_ROWS:dec0 + (k + 1) * SSD_DECAY_ROWS, :].astype(F32)
        decay = drows[0:1, :] + drows[1:2, :] + drows[2:3, :]
        for g in range(SSD_GROUPS):
            bg = bc[:, g * SSD_STATE:(g + 1) * SSD_STATE]
            upd = _dot_tn(bg, xw[:, g * gw:(g + 1) * gw])
            st_ref[:, g * gw:(g + 1) * gw] = st_ref[:, g * gw:(g + 1) * gw] * decay[:, g * gw:(g + 1) * gw] + upd


def _ssd_scan_kernel(xs_ref, bc_ref, dt_ref, alog_ref, e_ref, dskip_ref, y_ref, st_ref, ybuf_ref, *,
                     n_ctx_blocks, n_blocks):
    d = pl.program_id(1)
    s = pl.program_id(2)

    @pl.when(s == 0)
    def _():
        st_ref[...] = jnp.zeros_like(st_ref)

    for rev in (False, True):
        if rev:
            blk = jnp.where(s < n_ctx_blocks, n_ctx_blocks - 1 - s, n_blocks + 2 * n_ctx_blocks - 1 - s)
        else:
            blk = s
        for lat in (False, True):
            in_part = (blk >= n_ctx_blocks) if lat else (blk < n_ctx_blocks)

            @pl.when(jnp.logical_and(d == int(rev), in_part))
            def _(rev=rev, lat=lat, blk=blk):
                _ssd_step(rev, lat, jnp.maximum(blk - n_ctx_blocks, 0), xs_ref, bc_ref, dt_ref,
                          alog_ref, e_ref, dskip_ref, y_ref, st_ref, ybuf_ref)


def _ssd_scan(xs_all, bc_all, dt_all, alog, expand, dskip, n_ctx_chunks, n_chunks):
    b, lt, inner = xs_all.shape
    nck = SSD_CHUNKS_PER_STEP
    rows = nck * SSD_CHUNK
    assert n_ctx_chunks % nck == 0 and n_chunks % nck == 0
    n_ctx_blocks = n_ctx_chunks // nck
    n_blocks = n_chunks // nck
    steps = n_ctx_blocks + n_blocks

    def block_idx(d, s):
        bwd = jnp.where(s < n_ctx_blocks, n_ctx_blocks - 1 - s, steps + n_ctx_blocks - 1 - s)
        return jnp.where(d == 0, s, bwd)

    def out_idx(d, s):
        return jnp.where(jnp.logical_and(d == 1, s >= n_ctx_blocks), steps - 1 - s, n_blocks - 1)

    kern = functools.partial(_ssd_scan_kernel, n_ctx_blocks=n_ctx_blocks, n_blocks=n_blocks)
    return pl.pallas_call(
        kern,
        grid=(b, 2, steps),
        in_specs=[
            pl.BlockSpec((1, rows, inner), lambda i, d, s: (i, block_idx(d, s), 0)),
            pl.BlockSpec((1, rows, bc_all.shape[2]), lambda i, d, s: (i, block_idx(d, s), 0)),
            pl.BlockSpec((1, rows, LANES), lambda i, d, s: (i, block_idx(d, s), 0)),
            pl.BlockSpec((SUBLANES, LANES), lambda i, d, s: (0, 0)),
            pl.BlockSpec((2, LANES, inner), lambda i, d, s: (0, 0, 0)),
            pl.BlockSpec((1, inner), lambda i, d, s: (0, 0)),
        ],
        out_specs=pl.BlockSpec((1, rows, inner), lambda i, d, s: (i, out_idx(d, s), 0)),
        out_shape=jax.ShapeDtypeStruct((b, n_chunks * SSD_CHUNK, inner), BF16),
        scratch_shapes=[pltpu.VMEM((SSD_STATE, inner), F32),
                        pltpu.VMEM((n_chunks, SSD_CHUNK, inner), F32)],
        compiler_params=_cparams(3, 56),
        name="ssd_scan",
    )(xs_all, bc_all, dt_all, alog, expand, dskip)


def _mixer_epilogue(o, x, mod_ref, g_ref, rw_ref):
    xn = x + mod_ref[0, 2:3, :] * _rms(o, g_ref[1:2, :])
    h2 = _rms(xn, g_ref[2:3, :]) * (1.0 + mod_ref[0, 4:5, :]) + mod_ref[0, 3:4, :]
    h_hi = h2.astype(BF16)
    h_lo = (h2 - h_hi.astype(F32)).astype(BF16)
    ne = rw_ref.shape[0] // 2
    both = _dot_nt(rw_ref[...], h_hi)
    lg = both[:ne, :] + both[ne:, :] + _dot_nt(rw_ref[0:ne, :], h_lo)
    ex = jnp.exp(lg - jnp.max(lg, axis=0, keepdims=True))
    return xn, h_hi, ex / jnp.sum(ex, axis=0, keepdims=True)


def _ssd_out_kernel(y_ref, z_ref, x_ref, mod_ref, g_ref, ng_ref, w_ref, rw_ref, xo_ref, h2_ref, aff_ref):
    sub = TOKEN_TILE
    gdim = y_ref.shape[2] // SSD_GROUPS
    for r in range(y_ref.shape[1] // sub):
        rs = slice(r * sub, (r + 1) * sub)
        z = z_ref[0, rs, :].astype(F32)
        yg = y_ref[0, rs, :].astype(F32) * _silu(z)
        parts = [_rms(yg[:, g * gdim:(g + 1) * gdim], ng_ref[:, g * gdim:(g + 1) * gdim]).astype(BF16)
                 for g in range(SSD_GROUPS)]
        o = _dot(jnp.concatenate(parts, axis=1), w_ref[...])
        xn, hb, aff = _mixer_epilogue(o, x_ref[0, rs, :], mod_ref, g_ref, rw_ref)
        xo_ref[0, rs, :] = xn
        h2_ref[0, rs, :] = hb
        aff_ref[0, :, rs] = aff


def _epilogue_outs(b, l, d, tm):
    out_specs = [pl.BlockSpec((1, tm, d), lambda i, j: (i, j, 0)),
                 pl.BlockSpec((1, tm, d), lambda i, j: (i, j, 0)),
                 pl.BlockSpec((1, N_EXPERTS, tm), lambda i, j: (i, 0, j))]
    out_shape = [jax.ShapeDtypeStruct((b, l, d), F32),
                 jax.ShapeDtypeStruct((b, l, d), BF16),
                 jax.ShapeDtypeStruct((b, N_EXPERTS, l), F32)]
    return out_specs, out_shape


def _ssd_out(y, z, x, mods, g, ng, w, rw2):
    b, l, d = x.shape
    inner = y.shape[2]
    tm = 2 * TOKEN_TILE
    out_specs, out_shape = _epilogue_outs(b, l, d, tm)
    return pl.pallas_call(
        _ssd_out_kernel,
        grid=(b, l // tm),
        in_specs=[
            pl.BlockSpec((1, tm, inner), lambda i, j: (i, j, 0)),
            pl.BlockSpec((1, tm, inner), lambda i, j: (i, j, 0)),
            pl.BlockSpec((1, tm, d), lambda i, j: (i, j, 0)),
            pl.BlockSpec((1, SUBLANES, d), lambda i, j: (i, 0, 0)),
            pl.BlockSpec((SUBLANES, d), lambda i, j: (0, 0)),
            pl.BlockSpec((1, inner), lambda i, j: (0, 0)),
            pl.BlockSpec(w.shape, lambda i, j: (0, 0)),
            pl.BlockSpec(rw2.shape, lambda i, j: (0, 0)),
        ],
        out_specs=out_specs,
        out_shape=out_shape,
        compiler_params=_cparams(2, 48),
        name="ssd_out",
    )(y, z, x, mods, g, ng, w, rw2)


def _epilogue_kernel(o_ref, x_ref, mod_ref, g_ref, rw_ref, xo_ref, h2_ref, aff_ref):
    sub = TOKEN_TILE
    for r in range(o_ref.shape[1] // sub):
        rs = slice(r * sub, (r + 1) * sub)
        xn, hb, aff = _mixer_epilogue(o_ref[0, rs, :], x_ref[0, rs, :], mod_ref, g_ref, rw_ref)
        xo_ref[0, rs, :] = xn
        h2_ref[0, rs, :] = hb
        aff_ref[0, :, rs] = aff


def _epilogue(o, x, mods, g, rw_t):
    b, l, d = x.shape
    tm = 2 * TOKEN_TILE
    out_specs, out_shape = _epilogue_outs(b, l, d, tm)
    return pl.pallas_call(
        _epilogue_kernel,
        grid=(b, l // tm),
        in_specs=[
            pl.BlockSpec((1, tm, d), lambda i, j: (i, j, 0)),
            pl.BlockSpec((1, tm, d), lambda i, j: (i, j, 0)),
            pl.BlockSpec((1, SUBLANES, d), lambda i, j: (i, 0, 0)),
            pl.BlockSpec((SUBLANES, d), lambda i, j: (0, 0)),
            pl.BlockSpec(rw_t.shape, lambda i, j: (0, 0)),
        ],
        out_specs=out_specs,
        out_shape=out_shape,
        compiler_params=_cparams(2, 40),
        name="mixer_epilogue",
    )(o, x, mods, g, rw_t)


def _lane_prefix_exclusive(m01, upper):
    e, t = m01.shape
    carry = jnp.zeros((e, 1), F32)
    outs = []
    for k in range(t // LANES):
        tile = m01[:, k * LANES:(k + 1) * LANES]
        incl = _dot(tile.astype(BF16), upper)
        outs.append(incl - tile + carry)
        carry = carry + incl[:, LANES - 1:LANES]
    return jnp.concatenate(outs, axis=1)


def _route(aff, cap):
    e, t = aff.shape
    key = pltpu.bitcast(aff, jnp.int32)

    def enough(cand):
        return jnp.sum(jnp.where(key >= cand, 1.0, 0.0), axis=1, keepdims=True) >= cap

    def body(i, thr):
        sh = 29 - 2 * i
        c1, c2, c3 = (jnp.bitwise_or(thr, jnp.left_shift(jnp.int32(v), sh)) for v in (1, 2, 3))
        return jnp.where(enough(c3), c3, jnp.where(enough(c2), c2, jnp.where(enough(c1), c1, thr)))

    thr = lax.fori_loop(0, 15, body, jnp.zeros((e, 1), jnp.int32))
    last = jnp.bitwise_or(thr, 1)
    thr = jnp.where(enough(last), last, thr)
    gt = jnp.where(key > thr, 1.0, 0.0)
    eq = jnp.where(key == thr, 1.0, 0.0)
    need = cap - jnp.sum(gt, axis=1, keepdims=True)
    r = lax.broadcasted_iota(jnp.int32, (LANES, LANES), 0)
    cidx = lax.broadcasted_iota(jnp.int32, (LANES, LANES), 1)
    upper = jnp.where(r <= cidx, 1.0, 0.0).astype(BF16)
    eq_rank = _lane_prefix_exclusive(eq, upper)
    sel = gt + eq * jnp.where(eq_rank < need, 1.0, 0.0)
    pos = _lane_prefix_exclusive(sel, upper)
    return jnp.where(sel > 0.5, pos, -1.0).astype(jnp.int32)


MOE_EXPERTS_PER_STEP = 4
MOE_FFN_TILE = 1024


def _pick_rows(slot_ref, e, cap, t):
    srow = slot_ref[0, pl.ds(e, 1), :]
    return lax.broadcasted_iota(jnp.int32, (cap, t), 0) == srow


def _moe_gather_kernel(h_ref, aff_ref, xin_ref, slot_ref, *, cap):
    eg = pl.program_id(1)

    @pl.when(eg == 0)
    def _():
        slot_ref[0] = _route(aff_ref[0], cap)

    t = h_ref.shape[1]
    ps = [jnp.where(_pick_rows(slot_ref, eg * MOE_EXPERTS_PER_STEP + i, cap, t), 1.0, 0.0).astype(BF16)
          for i in range(MOE_EXPERTS_PER_STEP)]
    xin = _dot(jnp.concatenate(ps, axis=0), h_ref[0])
    for i in range(MOE_EXPERTS_PER_STEP):
        xin_ref[i] = xin[i * cap:(i + 1) * cap, :].astype(BF16)


def _moe_gather(h2, aff_t):
    b, t, d = h2.shape
    ne = aff_t.shape[1]
    cap = (CAPACITY_FACTOR * t) // ne
    g = MOE_EXPERTS_PER_STEP
    return pl.pallas_call(
        functools.partial(_moe_gather_kernel, cap=cap),
        grid=(b, ne // g),
        in_specs=[pl.BlockSpec((1, t, d), lambda i, e: (i, 0, 0)),
                  pl.BlockSpec((1, ne, t), lambda i, e: (i, 0, 0))],
        out_specs=[pl.BlockSpec((g, cap, d), lambda i, e: (e, i, 0)),
                   pl.BlockSpec((1, ne, t), lambda i, e: (i, 0, 0))],
        out_shape=[jax.ShapeDtypeStruct((ne, b * cap, d), BF16),
                   jax.ShapeDtypeStruct((b, ne, t), jnp.int32)],
        compiler_params=_cparams(2, 48),
        name="moe_gather",
    )(h2, aff_t)


def _moe_ffn_kernel(x_ref, wg_ref, wu_ref, wd_ref, y_ref, acc_ref):
    half = pl.program_id(2)
    x = x_ref[0]
    hid = _dot(x, wg_ref[0, 0].astype(BF16))
    hid = _silu(hid) * _dot(x, wu_ref[0, 0].astype(BF16))
    part = _dot(hid.astype(BF16), wd_ref[0, 0].astype(BF16))

    @pl.when(half == 0)
    def _():
        acc_ref[...] = part

    @pl.when(half == 1)
    def _():
        y_ref[0] = (acc_ref[...] + part).astype(BF16)


def _moe_ffn(xin, w_gate, w_up, w_down, layer):
    ne, m, d = xin.shape
    ff = w_gate.shape[3]
    tm = MOE_FFN_TILE
    hf = ff // 2
    return pl.pallas_call(
        _moe_ffn_kernel,
        grid=(ne, m // tm, 2),
        in_specs=[pl.BlockSpec((1, tm, d), lambda e, j, h: (e, j, 0)),
                  pl.BlockSpec((1, 1, d, hf), lambda e, j, h: (layer, e, 0, h)),
                  pl.BlockSpec((1, 1, d, hf), lambda e, j, h: (layer, e, 0, h)),
                  pl.BlockSpec((1, 1, hf, d), lambda e, j, h: (layer, e, h, 0))],
        out_specs=pl.BlockSpec((1, tm, d), lambda e, j, h: (e, j, 0)),
        out_shape=jax.ShapeDtypeStruct((ne, m, d), BF16),
        scratch_shapes=[pltpu.VMEM((tm, d), F32)],
        compiler_params=_cparams(3, 56),
        name="moe_ffn",
    )(xin, w_gate, w_up, w_down)


def _moe_scatter_kernel(slot_ref, aff_ref, y_ref, x_ref, mod_ref, g_ref, o_ref, xst_ref, *, cap):
    eg = pl.program_id(1)
    t = o_ref.shape[1]
    q = x_ref.shape[1]
    xst_ref[pl.ds(pl.multiple_of(eg * q, q), q), :] = x_ref[0]
    ps = []
    for i in range(MOE_EXPERTS_PER_STEP):
        e = eg * MOE_EXPERTS_PER_STEP + i
        ps.append(jnp.where(_pick_rows(slot_ref, e, cap, t), aff_ref[0, pl.ds(e, 1), :], 0.0).astype(BF16))
    p = jnp.concatenate(ps, axis=0)
    y = y_ref[...].reshape(MOE_EXPERTS_PER_STEP * cap, y_ref.shape[2])
    contrib = _dot_tn(p, y)

    @pl.when(eg == 0)
    def _():
        o_ref[0] = contrib

    @pl.when(eg > 0)
    def _():
        o_ref[0] += contrib

    @pl.when(eg == pl.num_programs(1) - 1)
    def _():
        sub = TOKEN_TILE
        for r in range(t // sub):
            rs = slice(r * sub, (r + 1) * sub)
            o_ref[0, rs, :] = xst_ref[rs, :] + mod_ref[0, 5:6, :] * _rms(o_ref[0, rs, :], g_ref[3:4, :])


def _moe_scatter(slot, aff_t, y, x, mods, g):
    ne, m, d = y.shape
    b, _, t = slot.shape
    cap = m // b
    grp = MOE_EXPERTS_PER_STEP
    steps = ne // grp
    return pl.pallas_call(
        functools.partial(_moe_scatter_kernel, cap=cap),
        grid=(b, steps),
        in_specs=[pl.BlockSpec((1, ne, t), lambda i, e: (i, 0, 0)),
                  pl.BlockSpec((1, ne, t), lambda i, e: (i, 0, 0)),
                  pl.BlockSpec((grp, cap, d), lambda i, e: (e, i, 0)),
                  pl.BlockSpec((1, t // steps, d), lambda i, e: (i, e, 0)),
                  pl.BlockSpec((1, SUBLANES, d), lambda i, e: (i, 0, 0)),
                  pl.BlockSpec((SUBLANES, d), lambda i, e: (0, 0))],
        out_specs=pl.BlockSpec((1, t, d), lambda i, e: (i, 0, 0)),
        out_shape=jax.ShapeDtypeStruct((b, t, d), F32),
        scratch_shapes=[pltpu.VMEM((t, d), F32)],
        compiler_params=_cparams(2, 56),
        name="moe_scatter",
    )(slot, aff_t, y, x, mods, g)


def _moe(h2, aff_t, w_gate, w_up, w_down, layer, x, mods, g):
    xin, slot = _moe_gather(h2, aff_t)
    y = _moe_ffn(xin, w_gate, w_up, w_down, layer)
    return _moe_scatter(slot, aff_t, y, x, mods, g)


GELU_C1 = 0.7978845608028654
GELU_C2 = GELU_C1 * 0.044715


def _mlp_in_kernel(x_ref, mod_ref, g_ref, w_ref, vg_ref, u_ref, v_ref):
    h = _rms(x_ref[0], g_ref[0:1, :]) * (1.0 + mod_ref[0, 1:2, :]) + mod_ref[0, 0:1, :]
    r = _dot(h.astype(BF16), w_ref[...])
    hr = 0.5 * r
    ge = hr + hr * jnp.tanh(r * (GELU_C1 + GELU_C2 * (r * r)))
    half = ge.shape[1] // 2
    u_ref[0] = ge[:, :half].astype(BF16)
    v = ge[:, half:]
    vc = v - jnp.mean(v, axis=-1, keepdims=True)
    vn = vc * lax.rsqrt(jnp.mean(vc * vc, axis=-1, keepdims=True) + EPS) * vg_ref[...]
    v_ref[0] = vn.astype(BF16)


def _mlp_in(x, mods, g, w, vg):
    b, l, d = x.shape
    tm = TOKEN_TILE
    half = w.shape[1] // 2
    tok = lambda i, j: (i, j, 0)
    return pl.pallas_call(
        _mlp_in_kernel,
        grid=(b, l // tm),
        in_specs=[
            pl.BlockSpec((1, tm, d), tok),
            pl.BlockSpec((1, SUBLANES, d), lambda i, j: (i, 0, 0)),
            pl.BlockSpec((SUBLANES, d), lambda i, j: (0, 0)),
            pl.BlockSpec(w.shape, lambda i, j: (0, 0), pipeline_mode=pl.Buffered(1)),
            pl.BlockSpec((1, half), lambda i, j: (0, 0)),
        ],
        out_specs=[pl.BlockSpec((1, tm, half), tok),
                   pl.BlockSpec((1, tm, half), tok)],
        out_shape=[jax.ShapeDtypeStruct((b, l, half), BF16),
                   jax.ShapeDtypeStruct((b, l, half), BF16)],
        compiler_params=_cparams(2, 48),
        name="mlp_in_proj",
    )(x, mods, g, w, vg)


MLP_GROUPS_PER_STEP = 4
MLP_COL_PITCH = GRID_W + SUBLANES


def _mlp_mix_kernel(u_ref, v_ref, ws_ref, bs_ref, w_ref, o_ref, vf_ref, sf_ref, us_ref, *, rows):
    gp = pl.program_id(1)
    t = u_ref.shape[1]
    n_chunks = t // MLP_CHUNK
    per_step = ws_ref.shape[0]
    cols_per_chunk = MLP_CHUNK // rows
    pitch = MLP_COL_PITCH

    @pl.when(gp < MLP_ROW_GROUPS // per_step)
    def _():
        for i in range(per_step):
            cs = slice(i * LANES, (i + 1) * LANES)
            for c in range(n_chunks):
                rs = slice(c * MLP_CHUNK, (c + 1) * MLP_CHUNK)
                s = _dot(ws_ref[i], v_ref[0, rs, cs]) + bs_ref[i]
                us_ref[rs, cs] = (u_ref[0, rs, cs].astype(F32) * s).astype(BF16)

    @pl.when(gp >= MLP_ROW_GROUPS // per_step)
    def _():
        for i in range(per_step):
            cs = slice(i * LANES, (i + 1) * LANES)
            for r in range(rows):
                vf_ref[r * pitch:r * pitch + GRID_W, :] = v_ref[0, r * GRID_W:(r + 1) * GRID_W, cs].astype(F32)
            for k in range(n_chunks):
                xk = jnp.concatenate(
                    [vf_ref[pl.ds(k * cols_per_chunk + wl, rows, stride=pitch), :] for wl in range(cols_per_chunk)],
                    axis=0)
                s = _dot(ws_ref[i], xk.astype(BF16)) + bs_ref[i]
                for wl in range(cols_per_chunk):
                    sf_ref[pl.ds(k * cols_per_chunk + wl, rows, stride=pitch), :] = s[wl * rows:(wl + 1) * rows, :]
            for r in range(rows):
                rr = slice(r * GRID_W, (r + 1) * GRID_W)
                us_ref[rr, cs] = (u_ref[0, rr, cs].astype(F32) * sf_ref[r * pitch:r * pitch + GRID_W, :]).astype(BF16)

    part = _dot(us_ref[...], w_ref[...])

    @pl.when(gp == 0)
    def _():
        o_ref[0] = part

    @pl.when(gp > 0)
    def _():
        o_ref[0] += part


def _mlp_mix(u, v, ws, bsb, w):
    b, l, inner = u.shape
    d = w.shape[1]
    per_step = MLP_GROUPS_PER_STEP
    steps = MLP_GROUPS // per_step
    kc = per_step * LANES
    rows = l // GRID_W
    return pl.pallas_call(
        functools.partial(_mlp_mix_kernel, rows=rows),
        grid=(b, steps),
        in_specs=[
            pl.BlockSpec((1, l, kc), lambda i, g: (i, 0, g)),
            pl.BlockSpec((1, l, kc), lambda i, g: (i, 0, g)),
            pl.BlockSpec((per_step, MLP_CHUNK, MLP_CHUNK), lambda i, g: (g, 0, 0)),
            pl.BlockSpec((per_step, MLP_CHUNK, LANES), lambda i, g: (g, 0, 0)),
            pl.BlockSpec((kc, d), lambda i, g: (g, 0)),
        ],
        out_specs=pl.BlockSpec((1, l, d), lambda i, g: (i, 0, 0)),
        out_shape=jax.ShapeDtypeStruct((b, l, d), F32),
        scratch_shapes=[pltpu.VMEM((rows * MLP_COL_PITCH, LANES), F32),
                        pltpu.VMEM((rows * MLP_COL_PITCH, LANES), F32),
                        pltpu.VMEM((l, kc), BF16)],
        compiler_params=_cparams(2, 48),
        name="mlp_mix",
    )(u, v, ws, bsb, w)


def _pad_rows(a, rows):
    return jnp.pad(a, ((0, rows - a.shape[0]),) + ((0, 0),) * (a.ndim - 1))


def kernel(x, c, ctx, c_ctx, mod_w, mod_b, norm_g, ssd_in_w, ssd_conv_w, ssd_conv_b, ssd_dt_bias, ssd_a_log,
           ssd_d, ssd_norm_g, ssd_out_w, mlp_in_w, mlp_v_g, mlp_ws, mlp_bs, mlp_out_w, router_w, exp_w_gate,
           exp_w_up, exp_w_down):
    b, l, d = x.shape
    inner = SSD_HEADS * SSD_HEADDIM
    n_chunks = l // SSD_CHUNK
    n_ctx_chunks = ctx.shape[1] // SSD_CHUNK

    crows = _pad_rows(jnp.concatenate([c, c_ctx[None, :]], axis=0), -(-(b + 1) // SUBLANES) * SUBLANES)
    mod = _modulation(crows, mod_w, mod_b)
    mods_lat = [jnp.pad(mod[i, :b].reshape(b, N_MOD, d), ((0, 0), (0, SUBLANES - N_MOD), (0, 0)))
                for i in range(2)]
    mods_ctx = jnp.pad(mod[0, b].reshape(1, N_MOD, d), ((0, 0), (0, SUBLANES - N_MOD), (0, 0)))
    mods01 = jnp.stack([jnp.broadcast_to(mods_ctx, (b, SUBLANES, d)), mods_lat[0]], axis=1)
    gains = [_pad_rows(norm_g[i], SUBLANES) for i in range(2)]
    rw2 = []
    for i in range(2):
        rw_t = router_w[i].T
        rw_hi = rw_t.astype(BF16)
        rw2.append(jnp.concatenate([rw_hi, (rw_t - rw_hi.astype(F32)).astype(BF16)], axis=0))

    in_w = ssd_in_w[0]
    conv_dim = ssd_conv_w.shape[2]
    w_in = jnp.pad(in_w, ((0, 0), (0, LANES - 2 * SSD_HEADS))).astype(BF16)
    dt_bias = jnp.pad(ssd_dt_bias[0].reshape(1, 2 * SSD_HEADS), ((0, 0), (0, LANES - 2 * SSD_HEADS)))
    z, xs_all, bc_all, dt_all = _ssd_in_proj(
        ctx, x, mods01, gains[0], w_in, _pad_rows(ssd_conv_w[0], SUBLANES), ssd_conv_b[0].reshape(1, conv_dim),
        dt_bias)

    alog = _pad_rows(jnp.stack([jnp.pad(ssd_a_log[0, 0], (0, LANES - SSD_HEADS)),
                                jnp.pad(ssd_a_log[0, 1], (SSD_HEADS, LANES - 2 * SSD_HEADS))]), SUBLANES)
    head_of_col = jnp.arange(inner, dtype=jnp.int32) // SSD_HEADDIM
    rows128 = jnp.arange(LANES, dtype=jnp.int32)[:, None]
    expand = jnp.stack([(rows128 == head_of_col[None, :] + SSD_HEADS * dd) for dd in range(2)]).astype(BF16)
    dskip = jnp.repeat(ssd_d[0], SSD_HEADDIM).reshape(1, inner)
    y = _ssd_scan(xs_all, bc_all, dt_all, alog, expand, dskip, n_ctx_chunks, n_chunks)

    xa, h2, aff_t = _ssd_out(y, z, x, mods_lat[0], gains[0], ssd_norm_g[0].reshape(1, inner),
                             ssd_out_w[0].astype(BF16), rw2[0])
    x1 = _moe(h2, aff_t, exp_w_gate, exp_w_up, exp_w_down, 0, xa, mods_lat[0], gains[0])

    u, v = _mlp_in(x1, mods_lat[1], gains[1], mlp_in_w[0].astype(BF16), mlp_v_g[0].reshape(1, -1))
    bsb = jnp.broadcast_to(mlp_bs[0][:, :, None], (MLP_GROUPS, MLP_CHUNK, LANES))
    o1 = _mlp_mix(u, v, mlp_ws[0].astype(BF16), bsb, mlp_out_w[0].astype(BF16))
    xb, h2b, aff_tb = _epilogue(o1, x1, mods_lat[1], gains[1], rw2[1])
    return _moe(h2b, aff_tb, exp_w_gate, exp_w_up, exp_w_down, 1, xb, mods_lat[1], gains[1])
```

```python
import functools

import jax
import jax.numpy as jnp
from jax import lax
from jax.experimental import pallas as pl
from jax.experimental.pallas import tpu as pltpu

F32 = jnp.float32
BF16 = jnp.bfloat16
HIGHEST = lax.Precision.HIGHEST
EPS = 1e-6

LANES = 128
SUBLANES = 8
MIB = 1024 * 1024

N_MOD = 6
GRID_W = 64
SSD_HEADDIM = 64
SSD_HEADS = 32
SSD_GROUPS = 4
SSD_HPG = SSD_HEADS // SSD_GROUPS
SSD_STATE = 128
SSD_CONV = 5
SSD_CHUNK = 128
MLP_CHUNK = 128
MLP_GROUPS = 16
MLP_ROW_GROUPS = 8
N_EXPERTS = 16
CAPACITY_FACTOR = 2

CONV_HALO = SUBLANES
TOKEN_TILE = 256
CONV_COL_BLOCK = 1024


def _cparams(n_axes, vmem_mib):
    return pltpu.CompilerParams(dimension_semantics=("arbitrary",) * n_axes,
                                vmem_limit_bytes=vmem_mib * MIB)


def _silu(x):
    h = 0.5 * x
    return h + h * jnp.tanh(h)


def _rms(x, g):
    return x * lax.rsqrt(jnp.mean(x * x, axis=-1, keepdims=True) + EPS) * g


def _dot(a, b):
    return jnp.dot(a, b, preferred_element_type=F32)


def _dot_nt(a, b, precision=None):
    return lax.dot_general(a, b, (((1,), (1,)), ((), ())), preferred_element_type=F32, precision=precision)


def _dot_tn(a, b):
    return lax.dot_general(a, b, (((0,), (0,)), ((), ())), preferred_element_type=F32)


def _mod_kernel(c_ref, w_ref, b_ref, o_ref):
    c = c_ref[...]
    s = _silu(c)
    o_ref[0] = jnp.dot(s, w_ref[0], preferred_element_type=F32, precision=HIGHEST) + b_ref[0]


def _modulation(crows, mod_w, mod_b):
    depth, d, n = mod_w.shape
    rows = crows.shape[0]
    tn = 1536
    return pl.pallas_call(
        _mod_kernel,
        grid=(depth, n // tn),
        in_specs=[pl.BlockSpec((rows, d), lambda i, j: (0, 0)),
                  pl.BlockSpec((1, d, tn), lambda i, j: (i, 0, j)),
                  pl.BlockSpec((1, 1, tn), lambda i, j: (i, 0, j))],
        out_specs=pl.BlockSpec((1, rows, tn), lambda i, j: (i, 0, j)),
        out_shape=jax.ShapeDtypeStruct((depth, rows, n), F32),
        compiler_params=_cparams(2, 40),
        name="modulation",
    )(crows, mod_w, mod_b.reshape(depth, 1, n))


def _ssd_in_kernel(ctx_ref, x_ref, xp_ref, xn_ref, mod_ref, g_ref, w_ref, cw_ref, cb_ref, dtb_ref, sh_ref,
                   z_ref, xs_ref, bc_ref, dt_ref, *, n_tiles, inner, conv_dim):
    j = pl.program_id(1)
    tm = x_ref.shape[1]
    halo = xp_ref.shape[1]
    xc = jnp.where(j == 0, ctx_ref[0], x_ref[0])
    xa = jnp.concatenate([xp_ref[0], xc, xn_ref[0]], axis=0)
    h = (_rms(xa, g_ref[0:1, :]) * (1.0 + mod_ref[0, 0, 1:2, :]) + mod_ref[0, 0, 0:1, :]).astype(BF16)
    hc = h[halo:halo + tm, :]
    z_ref[0] = _dot(hc, w_ref[:, :inner]).astype(BF16)
    dtr = _dot(hc, w_ref[:, inner + conv_dim:]) + dtb_ref[...]
    dt_ref[0] = jnp.maximum(dtr, 0.0) + jnp.log1p(jnp.exp(-jnp.abs(dtr)))
    keep_top = jnp.where(j <= 1, 0.0, 1.0)
    keep_bot = jnp.where(jnp.logical_or(j == 0, j == n_tiles - 1), 0.0, 1.0)
    hrow = lax.broadcasted_iota(jnp.int32, (halo, 1), 0)
    pad = (SSD_CONV - 1) // 2
    taps = [k for k in range(SSD_CONV) if k != pad]
    cblk = CONV_COL_BLOCK
    for cbi in range(conv_dim // cblk):
        c0 = cbi * cblk
        u = _dot(h, w_ref[:, inner + c0:inner + c0 + cblk])
        uc = u[halo:halo + tm, :]
        top = u[:halo, :] * keep_top
        bot = u[halo + tm:, :] * keep_bot
        shifted = _dot(sh_ref[...], uc.astype(BF16))
        acc = cb_ref[:, c0:c0 + cblk] + cw_ref[pad:pad + 1, c0:c0 + cblk] * uc
        corr_top = jnp.zeros((halo, cblk), F32)
        corr_bot = jnp.zeros((halo, cblk), F32)
        for jj, k in enumerate(taps):
            wk = cw_ref[k:k + 1, c0:c0 + cblk]
            acc = acc + wk * shifted[jj * tm:(jj + 1) * tm, :]
            d = k - pad
            if d < 0:
                corr_top = corr_top + wk * jnp.where(hrow < -d, pltpu.roll(top, -d, axis=0), 0.0)
            else:
                corr_bot = corr_bot + wk * jnp.where(hrow >= halo - d, pltpu.roll(bot, halo - d, axis=0), 0.0)
        acc = jnp.concatenate([acc[:halo, :] + corr_top, acc[halo:tm - halo, :], acc[tm - halo:, :] + corr_bot],
                              axis=0)
        act = _silu(acc).astype(BF16)
        if c0 < inner:
            xs_ref[0, :, c0:c0 + cblk] = act
        else:
            bc_ref[0, :, c0 - inner:c0 - inner + cblk] = act


def _ssd_in_proj(ctx, x, mods01, g, w, conv_w, conv_b, dt_bias):
    b, l, d = x.shape
    lc = ctx.shape[1]
    tm = TOKEN_TILE
    assert lc == tm and l % tm == 0
    n_tiles = 1 + l // tm
    inner = SSD_HEADS * SSD_HEADDIM
    conv_dim = conv_w.shape[1]
    bc_dim = conv_dim - inner
    halo = CONV_HALO
    blocks_per_tile = tm // halo
    last_halo_block = l // halo - 1
    lt = lc + l
    pad = (SSD_CONV - 1) // 2
    t_idx = jnp.arange(tm, dtype=jnp.int32)
    shifts = jnp.concatenate([t_idx[None, :] == t_idx[:, None] + (k - pad) for k in range(SSD_CONV) if k != pad],
                             axis=0).astype(BF16)
    kern = functools.partial(_ssd_in_kernel, n_tiles=n_tiles, inner=inner, conv_dim=conv_dim)
    return pl.pallas_call(
        kern,
        grid=(b, n_tiles),
        in_specs=[
            pl.BlockSpec((1, lc, d), lambda i, j: (i, 0, 0)),
            pl.BlockSpec((1, tm, d), lambda i, j: (i, jnp.maximum(j - 1, 0), 0)),
            pl.BlockSpec((1, halo, d), lambda i, j: (i, jnp.maximum((j - 1) * blocks_per_tile - 1, 0), 0)),
            pl.BlockSpec((1, halo, d), lambda i, j: (i, jnp.minimum(j * blocks_per_tile, last_halo_block), 0)),
            pl.BlockSpec((1, 1, SUBLANES, d), lambda i, j: (i, jnp.minimum(j, 1), 0, 0)),
            pl.BlockSpec((SUBLANES, d), lambda i, j: (0, 0)),
            pl.BlockSpec(w.shape, lambda i, j: (0, 0), pipeline_mode=pl.Buffered(1)),
            pl.BlockSpec((SUBLANES, conv_dim), lambda i, j: (0, 0)),
            pl.BlockSpec((1, conv_dim), lambda i, j: (0, 0)),
            pl.BlockSpec((1, LANES), lambda i, j: (0, 0)),
            pl.BlockSpec(shifts.shape, lambda i, j: (0, 0)),
        ],
        out_specs=[
            pl.BlockSpec((1, tm, inner), lambda i, j: (i, jnp.maximum(j - 1, 0), 0)),
            pl.BlockSpec((1, tm, inner), lambda i, j: (i, j, 0)),
            pl.BlockSpec((1, tm, bc_dim), lambda i, j: (i, j, 0)),
            pl.BlockSpec((1, tm, LANES), lambda i, j: (i, j, 0)),
        ],
        out_shape=[
            jax.ShapeDtypeStruct((b, l, inner), BF16),
            jax.ShapeDtypeStruct((b, lt, inner), BF16),
            jax.ShapeDtypeStruct((b, lt, bc_dim), BF16),
            jax.ShapeDtypeStruct((b, lt, LANES), F32),
        ],
        compiler_params=_cparams(2, 56),
        name="ssd_in_proj",
    )(ctx, x, x, x, mods01, g, w, conv_w, conv_b, dt_bias, shifts)


SSD_CHUNKS_PER_STEP = 2
SSD_DECAY_ROWS = 16


def _split3(x):
    hi = x.astype(BF16)
    r1 = x - hi.astype(F32)
    mid = r1.astype(BF16)
    lo = (r1 - mid.astype(F32)).astype(BF16)
    return hi, mid, lo


def _ssd_step(rev, lat, lblk, xs_ref, bc_ref, dt_ref, alog_ref, e_ref, dskip_ref, y_ref, st_ref, ybuf_ref):
    c = SSD_CHUNK
    nck = SSD_CHUNKS_PER_STEP
    dcol = SSD_HEADS * int(rev)
    gw = SSD_HPG * SSD_HEADDIM
    bc_off = SSD_GROUPS * SSD_STATE
    a_neg = -jnp.exp(alog_ref[int(rev):int(rev) + 1, :])
    li = lax.broadcasted_iota(jnp.int32, (c, c), 0)
    si = lax.broadcasted_iota(jnp.int32, (c, c), 1)
    causal = (si >= li) if rev else (si <= li)
    tri = jnp.where(causal, 1.0, 0.0).astype(BF16)
    e = e_ref[int(rev)]
    head = lax.broadcasted_iota(jnp.int32, (c, LANES), 1)

    dts = [dt_ref[0, k * c:(k + 1) * c, :] for k in range(nck)]
    pieces = []
    for k in range(nck):
        pieces.extend(_split3(dts[k] * a_neg))
    run = _dot(tri, jnp.concatenate(pieces, axis=1))
    css = [run[:, (3 * k) * LANES:(3 * k + 1) * LANES] + run[:, (3 * k + 1) * LANES:(3 * k + 2) * LANES]
           + run[:, (3 * k + 2) * LANES:(3 * k + 3) * LANES] for k in range(nck)]
    tots = [cs[0:1, :] if rev else cs[c - 1:c, :] for cs in css]

    blocks = []
    for k in range(nck):
        blocks.append((jnp.exp(tots[k] - css[k]) * dts[k]).astype(BF16))
        blocks.append(jnp.exp(css[k]).astype(BF16))
    drow = lax.broadcasted_iota(jnp.int32, (SSD_DECAY_ROWS, LANES), 0)
    for k in range(nck):
        hi, mid, lo = _split3(jnp.broadcast_to(jnp.exp(tots[k]), (SSD_DECAY_ROWS, LANES)))
        rows3 = jnp.where(drow == 0, hi.astype(F32),
                          jnp.where(drow == 1, mid.astype(F32), jnp.where(drow == 2, lo.astype(F32), 0.0)))
        blocks.append(rows3.astype(BF16))
    ex = _dot(jnp.concatenate(blocks, axis=0), e).astype(BF16)
    dec0 = 2 * nck * c
    even_cols = jnp.where(head < SSD_HEADDIM, 1.0, 0.0).astype(BF16)
    odd_cols = jnp.where(head < SSD_HEADDIM, 0.0, 1.0).astype(BF16)

    for k in (range(nck - 1, -1, -1) if rev else range(nck)):
        base = 2 * k * c
        xs = xs_ref[0, k * c:(k + 1) * c, :]
        bc = bc_ref[0, k * c:(k + 1) * c, :]
        xw = xs * ex[base:base + c, :]

        def y_part(k=k, base=base, xs=xs, bc=bc):
            lc = lblk * nck + k
            cs = css[k]
            rs_t = (cs - jnp.log(dts[k])).T
            for g in range(SSD_GROUPS):
                bg = bc[:, g * SSD_STATE:(g + 1) * SSD_STATE]
                cg = bc[:, bc_off + g * SSD_STATE:bc_off + (g + 1) * SSD_STATE]
                px = ex[base + c:base + 2 * c, g * gw:(g + 1) * gw].astype(F32)
                y_off = _dot(cg, st_ref[:, g * gw:(g + 1) * gw].astype(BF16)) * px
                cb = _dot_nt(cg, bg)
                for hp in range(SSD_HPG // 2):
                    ms = []
                    for hh in range(2):
                        hd = dcol + g * SSD_HPG + 2 * hp + hh
                        diff = cs[:, hd:hd + 1] - rs_t[hd:hd + 1, :]
                        ms.append(cb * jnp.exp(jnp.where(causal, diff, -jnp.inf)))
                    mp = jnp.concatenate(ms, axis=1).astype(BF16)
                    col0 = (g * SSD_HPG + 2 * hp) * SSD_HEADDIM
                    xp = xs[:, col0:col0 + LANES]
                    rhs = jnp.concatenate([xp * even_cols, xp * odd_cols], axis=0)
                    y_pair = _dot(mp, rhs) + y_off[:, 2 * hp * SSD_HEADDIM:2 * hp * SSD_HEADDIM + LANES]
                    if rev:
                        tot_y = (ybuf_ref[lc, :, col0:col0 + LANES] + y_pair
                                 + dskip_ref[:, col0:col0 + LANES] * xp.astype(F32))
                        y_ref[0, k * c:(k + 1) * c, col0:col0 + LANES] = tot_y.astype(BF16)
                    else:
                        ybuf_ref[lc, :, col0:col0 + LANES] = y_pair

        if lat:
            y_part()

        drows = ex[dec0 + k * SSD_DECAY_ROWS:dec0 + (k + 1) * SSD_DECAY_ROWS, :].astype(F32)
        decay = drows[0:1, :] + drows[1:2, :] + drows[2:3, :]
        for g in range(SSD_GROUPS):
            bg = bc[:, g * SSD_STATE:(g + 1) * SSD_STATE]
            upd = _dot_tn(bg, xw[:, g * gw:(g + 1) * gw])
            st_ref[:, g * gw:(g + 1) * gw] = st_ref[:, g * gw:(g + 1) * gw] * decay[:, g * gw:(g + 1) * gw] + upd


def _ssd_scan_kernel(xs_ref, bc_ref, dt_ref, alog_ref, e_ref, dskip_ref, y_ref, st_ref, ybuf_ref, *,
                     n_ctx_blocks, n_blocks):
    d = pl.program_id(1)
    s = pl.program_id(2)

    @pl.when(s == 0)
    def _():
        st_ref[...] = jnp.zeros_like(st_ref)

    for rev in (False, True):
        if rev:
            blk = jnp.where(s < n_ctx_blocks, n_ctx_blocks - 1 - s, n_blocks + 2 * n_ctx_blocks - 1 - s)
        else:
            blk = s
        for lat in (False, True):
            in_part = (blk >= n_ctx_blocks) if lat else (blk < n_ctx_blocks)

            @pl.when(jnp.logical_and(d == int(rev), in_part))
            def _(rev=rev, lat=lat, blk=blk):
                _ssd_step(rev, lat, jnp.maximum(blk - n_ctx_blocks, 0), xs_ref, bc_ref, dt_ref,
                          alog_ref, e_ref, dskip_ref, y_ref, st_ref, ybuf_ref)


def _ssd_scan(xs_all, bc_all, dt_all, alog, expand, dskip, n_ctx_chunks, n_chunks):
    b, lt, inner = xs_all.shape
    nck = SSD_CHUNKS_PER_STEP
    rows = nck * SSD_CHUNK
    assert n_ctx_chunks % nck == 0 and n_chunks % nck == 0
    n_ctx_blocks = n_ctx_chunks // nck
    n_blocks = n_chunks // nck
    steps = n_ctx_blocks + n_blocks

    def block_idx(d, s):
        bwd = jnp.where(s < n_ctx_blocks, n_ctx_blocks - 1 - s, steps + n_ctx_blocks - 1 - s)
        return jnp.where(d == 0, s, bwd)

    def out_idx(d, s):
        return jnp.where(jnp.logical_and(d == 1, s >= n_ctx_blocks), steps - 1 - s, n_blocks - 1)

    kern = functools.partial(_ssd_scan_kernel, n_ctx_blocks=n_ctx_blocks, n_blocks=n_blocks)
    return pl.pallas_call(
        kern,
        grid=(b, 2, steps),
        in_specs=[
            pl.BlockSpec((1, rows, inner), lambda i, d, s: (i, block_idx(d, s), 0)),
            pl.BlockSpec((1, rows, bc_all.shape[2]), lambda i, d, s: (i, block_idx(d, s), 0)),
            pl.BlockSpec((1, rows, LANES), lambda i, d, s: (i, block_idx(d, s), 0)),
            pl.BlockSpec((SUBLANES, LANES), lambda i, d, s: (0, 0)),
            pl.BlockSpec((2, LANES, inner), lambda i, d, s: (0, 0, 0)),
            pl.BlockSpec((1, inner), lambda i, d, s: (0, 0)),
        ],
        out_specs=pl.BlockSpec((1, rows, inner), lambda i, d, s: (i, out_idx(d, s), 0)),
        out_shape=jax.ShapeDtypeStruct((b, n_chunks * SSD_CHUNK, inner), BF16),
        scratch_shapes=[pltpu.VMEM((SSD_STATE, inner), F32),
                        pltpu.VMEM((n_chunks, SSD_CHUNK, inner), F32)],
        compiler_params=_cparams(3, 56),
        name="ssd_scan",
    )(xs_all, bc_all, dt_all, alog, expand, dskip)


def _mixer_epilogue(o, x, mod_ref, g_ref, rw_ref):
    xn = x + mod_ref[0, 2:3, :] * _rms(o, g_ref[1:2, :])
    h2 = _rms(xn, g_ref[2:3, :]) * (1.0 + mod_ref[0, 4:5, :]) + mod_ref[0, 3:4, :]
    h_hi = h2.astype(BF16)
    h_lo = (h2 - h_hi.astype(F32)).astype(BF16)
    ne = rw_ref.shape[0] // 2
    both = _dot_nt(rw_ref[...], h_hi)
    lg = both[:ne, :] + both[ne:, :] + _dot_nt(rw_ref[0:ne, :], h_lo)
    ex = jnp.exp(lg - jnp.max(lg, axis=0, keepdims=True))
    return xn, h_hi, ex / jnp.sum(ex, axis=0, keepdims=True)


def _ssd_out_kernel(y_ref, z_ref, x_ref, mod_ref, g_ref, ng_ref, w_ref, rw_ref, xo_ref, h2_ref, aff_ref):
    sub = TOKEN_TILE
    gdim = y_ref.shape[2] // SSD_GROUPS
    for r in range(y_ref.shape[1] // sub):
        rs = slice(r * sub, (r + 1) * sub)
        z = z_ref[0, rs, :].astype(F32)
        yg = y_ref[0, rs, :].astype(F32) * _silu(z)
        parts = [_rms(yg[:, g * gdim:(g + 1) * gdim], ng_ref[:, g * gdim:(g + 1) * gdim]).astype(BF16)
                 for g in range(SSD_GROUPS)]
        o = _dot(jnp.concatenate(parts, axis=1), w_ref[...])
        xn, hb, aff = _mixer_epilogue(o, x_ref[0, rs, :], mod_ref, g_ref, rw_ref)
        xo_ref[0, rs, :] = xn
        h2_ref[0, rs, :] = hb
        aff_ref[0, :, rs] = aff


def _epilogue_outs(b, l, d, tm):
    out_specs = [pl.BlockSpec((1, tm, d), lambda i, j: (i, j, 0)),
                 pl.BlockSpec((1, tm, d), lambda i, j: (i, j, 0)),
                 pl.BlockSpec((1, N_EXPERTS, tm), lambda i, j: (i, 0, j))]
    out_shape = [jax.ShapeDtypeStruct((b, l, d), F32),
                 jax.ShapeDtypeStruct((b, l, d), BF16),
                 jax.ShapeDtypeStruct((b, N_EXPERTS, l), F32)]
    return out_specs, out_shape


def _ssd_out(y, z, x, mods, g, ng, w, rw2):
    b, l, d = x.shape
    inner = y.shape[2]
    tm = 2 * TOKEN_TILE
    out_specs, out_shape = _epilogue_outs(b, l, d, tm)
    return pl.pallas_call(
        _ssd_out_kernel,
        grid=(b, l // tm),
        in_specs=[
            pl.BlockSpec((1, tm, inner), lambda i, j: (i, j, 0)),
            pl.BlockSpec((1, tm, inner), lambda i, j: (i, j, 0)),
            pl.BlockSpec((1, tm, d), lambda i, j: (i, j, 0)),
            pl.BlockSpec((1, SUBLANES, d), lambda i, j: (i, 0, 0)),
            pl.BlockSpec((SUBLANES, d), lambda i, j: (0, 0)),
            pl.BlockSpec((1, inner), lambda i, j: (0, 0)),
            pl.BlockSpec(w.shape, lambda i, j: (0, 0)),
            pl.BlockSpec(rw2.shape, lambda i, j: (0, 0)),
        ],
        out_specs=out_specs,
        out_shape=out_shape,
        compiler_params=_cparams(2, 48),
        name="ssd_out",
    )(y, z, x, mods, g, ng, w, rw2)


def _lane_prefix_exclusive(m01, upper):
    e, t = m01.shape
    carry = jnp.zeros((e, 1), F32)
    outs = []
    for k in range(t // LANES):
        tile = m01[:, k * LANES:(k + 1) * LANES]
        incl = _dot(tile.astype(BF16), upper)
        outs.append(incl - tile + carry)
        carry = carry + incl[:, LANES - 1:LANES]
    return jnp.concatenate(outs, axis=1)


def _route(aff, cap):
    e, t = aff.shape
    key = pltpu.bitcast(aff, jnp.int32)

    def enough(cand):
        return jnp.sum(jnp.where(key >= cand, 1.0, 0.0), axis=1, keepdims=True) >= cap

    def body(i, thr):
        sh = 29 - 2 * i
        c1, c2, c3 = (jnp.bitwise_or(thr, jnp.left_shift(jnp.int32(v), sh)) for v in (1, 2, 3))
        return jnp.where(enough(c3), c3, jnp.where(enough(c2), c2, jnp.where(enough(c1), c1, thr)))

    thr = lax.fori_loop(0, 15, body, jnp.zeros((e, 1), jnp.int32))
    last = jnp.bitwise_or(thr, 1)
    thr = jnp.where(enough(last), last, thr)
    gt = jnp.where(key > thr, 1.0, 0.0)
    eq = jnp.where(key == thr, 1.0, 0.0)
    need = cap - jnp.sum(gt, axis=1, keepdims=True)
    r = lax.broadcasted_iota(jnp.int32, (LANES, LANES), 0)
    cidx = lax.broadcasted_iota(jnp.int32, (LANES, LANES), 1)
    upper = jnp.where(r <= cidx, 1.0, 0.0).astype(BF16)
    eq_rank = _lane_prefix_exclusive(eq, upper)
    sel = gt + eq * jnp.where(eq_rank < need, 1.0, 0.0)
    pos = _lane_prefix_exclusive(sel, upper)
    return jnp.where(sel > 0.5, pos, -1.0).astype(jnp.int32)


MOE_EXPERTS_PER_STEP = 4
MOE_FFN_TILE = 1024


def _pick_rows(slot_ref, e, cap, t):
    srow = slot_ref[0, pl.ds(e, 1), :]
    return lax.broadcasted_iota(jnp.int32, (cap, t), 0) == srow


def _moe_gather_kernel(h_ref, aff_ref, xin_ref, slot_ref, *, cap):
    eg = pl.program_id(1)

    @pl.when(eg == 0)
    def _():
        slot_ref[0] = _route(aff_ref[0], cap)

    t = h_ref.shape[1]
    ps = [jnp.where(_pick_rows(slot_ref, eg * MOE_EXPERTS_PER_STEP + i, cap, t), 1.0, 0.0).astype(BF16)
          for i in range(MOE_EXPERTS_PER_STEP)]
    xin = _dot(jnp.concatenate(ps, axis=0), h_ref[0])
    for i in range(MOE_EXPERTS_PER_STEP):
        xin_ref[i] = xin[i * cap:(i + 1) * cap, :].astype(BF16)


def _moe_gather(h2, aff_t):
    b, t, d = h2.shape
    ne = aff_t.shape[1]
    cap = (CAPACITY_FACTOR * t) // ne
    g = MOE_EXPERTS_PER_STEP
    return pl.pallas_call(
        functools.partial(_moe_gather_kernel, cap=cap),
        grid=(b, ne // g),
        in_specs=[pl.BlockSpec((1, t, d), lambda i, e: (i, 0, 0)),
                  pl.BlockSpec((1, ne, t), lambda i, e: (i, 0, 0))],
        out_specs=[pl.BlockSpec((g, cap, d), lambda i, e: (e, i, 0)),
                   pl.BlockSpec((1, ne, t), lambda i, e: (i, 0, 0))],
        out_shape=[jax.ShapeDtypeStruct((ne, b * cap, d), BF16),
                   jax.ShapeDtypeStruct((b, ne, t), jnp.int32)],
        compiler_params=_cparams(2, 48),
        name="moe_gather",
    )(h2, aff_t)


def _moe_ffn_kernel(x_ref, wg_ref, wu_ref, wd_ref, y_ref, acc_ref):
    half = pl.program_id(2)

    def half_ffn():
        x = x_ref[0]
        hid = _dot(x, wg_ref[0, 0].astype(BF16))
        hid = _silu(hid) * _dot(x, wu_ref[0, 0].astype(BF16))
        return _dot(hid.astype(BF16), wd_ref[0, 0].astype(BF16))

    @pl.when(half == 0)
    def _():
        acc_ref[...] = half_ffn()

    @pl.when(half == 1)
    def _():
        y_ref[0] = (acc_ref[...] + half_ffn()).astype(BF16)


def _moe_ffn(xin, w_gate, w_up, w_down, layer):
    ne, m, d = xin.shape
    ff = w_gate.shape[3]
    tm = MOE_FFN_TILE
    hf = ff // 2
    return pl.pallas_call(
        _moe_ffn_kernel,
        grid=(ne, m // tm, 2),
        in_specs=[pl.BlockSpec((1, tm, d), lambda e, j, h: (e, j, 0)),
                  pl.BlockSpec((1, 1, d, hf), lambda e, j, h: (layer, e, 0, h)),
                  pl.BlockSpec((1, 1, d, hf), lambda e, j, h: (layer, e, 0, h)),
                  pl.BlockSpec((1, 1, hf, d), lambda e, j, h: (layer, e, h, 0))],
        out_specs=pl.BlockSpec((1, tm, d), lambda e, j, h: (e, j, 0)),
        out_shape=jax.ShapeDtypeStruct((ne, m, d), BF16),
        scratch_shapes=[pltpu.VMEM((tm, d), F32)],
        compiler_params=_cparams(3, 56),
        name="moe_ffn",
    )(xin, w_gate, w_up, w_down)


def _moe_scatter_kernel(slot_ref, aff_ref, y_ref, x_ref, mod_ref, g_ref, o_ref, xst_ref, *, cap):
    eg = pl.program_id(1)
    t = o_ref.shape[1]
    q = x_ref.shape[1]
    xst_ref[pl.ds(pl.multiple_of(eg * q, q), q), :] = x_ref[0]

    def contribution():
        ps = []
        for i in range(MOE_EXPERTS_PER_STEP):
            e = eg * MOE_EXPERTS_PER_STEP + i
            ps.append(jnp.where(_pick_rows(slot_ref, e, cap, t), aff_ref[0, pl.ds(e, 1), :], 0.0).astype(BF16))
        p = jnp.concatenate(ps, axis=0)
        y = y_ref[...].reshape(MOE_EXPERTS_PER_STEP * cap, y_ref.shape[2])
        return _dot_tn(p, y)

    @pl.when(eg == 0)
    def _():
        o_ref[0] = contribution()

    @pl.when(eg > 0)
    def _():
        o_ref[0] += contribution()

    @pl.when(eg == pl.num_programs(1) - 1)
    def _():
        sub = TOKEN_TILE
        for r in range(t // sub):
            rs = slice(r * sub, (r + 1) * sub)
            o_ref[0, rs, :] = xst_ref[rs, :] + mod_ref[0, 5:6, :] * _rms(o_ref[0, rs, :], g_ref[3:4, :])


def _moe_scatter(slot, aff_t, y, x, mods, g):
    ne, m, d = y.shape
    b, _, t = slot.shape
    cap = m // b
    grp = MOE_EXPERTS_PER_STEP
    steps = ne // grp
    return pl.pallas_call(
        functools.partial(_moe_scatter_kernel, cap=cap),
        grid=(b, steps),
        in_specs=[pl.BlockSpec((1, ne, t), lambda i, e: (i, 0, 0)),
                  pl.BlockSpec((1, ne, t), lambda i, e: (i, 0, 0)),
                  pl.BlockSpec((grp, cap, d), lambda i, e: (e, i, 0)),
                  pl.BlockSpec((1, t // steps, d), lambda i, e: (i, e, 0)),
                  pl.BlockSpec((1, SUBLANES, d), lambda i, e: (i, 0, 0)),
                  pl.BlockSpec((SUBLANES, d), lambda i, e: (0, 0))],
        out_specs=pl.BlockSpec((1, t, d), lambda i, e: (i, 0, 0)),
        out_shape=jax.ShapeDtypeStruct((b, t, d), F32),
        scratch_shapes=[pltpu.VMEM((t, d), F32)],
        compiler_params=_cparams(2, 56),
        name="moe_scatter",
    )(slot, aff_t, y, x, mods, g)


def _moe(h2, aff_t, w_gate, w_up, w_down, layer, x, mods, g):
    xin, slot = _moe_gather(h2, aff_t)
    y = _moe_ffn(xin, w_gate, w_up, w_down, layer)
    return _moe_scatter(slot, aff_t, y, x, mods, g)


GELU_C1 = 0.7978845608028654
GELU_C2 = GELU_C1 * 0.044715


def _mlp_in_kernel(x_ref, mod_ref, g_ref, w_ref, vg_ref, u_ref, v_ref):
    h = _rms(x_ref[0], g_ref[0:1, :]) * (1.0 + mod_ref[0, 1:2, :]) + mod_ref[0, 0:1, :]
    r = _dot(h.astype(BF16), w_ref[...])
    hr = 0.5 * r
    ge = hr + hr * jnp.tanh(r * (GELU_C1 + GELU_C2 * (r * r)))
    half = ge.shape[1] // 2
    u_ref[0] = ge[:, :half].astype(BF16)
    v = ge[:, half:]
    vc = v - jnp.mean(v, axis=-1, keepdims=True)
    vn = vc * lax.rsqrt(jnp.mean(vc * vc, axis=-1, keepdims=True) + EPS) * vg_ref[...]
    v_ref[0] = vn.astype(BF16)


def _mlp_in(x, mods, g, w, vg):
    b, l, d = x.shape
    tm = TOKEN_TILE
    half = w.shape[1] // 2
    tok = lambda i, j: (i, j, 0)
    return pl.pallas_call(
        _mlp_in_kernel,
        grid=(b, l // tm),
        in_specs=[
            pl.BlockSpec((1, tm, d), tok),
            pl.BlockSpec((1, SUBLANES, d), lambda i, j: (i, 0, 0)),
            pl.BlockSpec((SUBLANES, d), lambda i, j: (0, 0)),
            pl.BlockSpec(w.shape, lambda i, j: (0, 0), pipeline_mode=pl.Buffered(1)),
            pl.BlockSpec((1, half), lambda i, j: (0, 0)),
        ],
        out_specs=[pl.BlockSpec((1, tm, half), tok),
                   pl.BlockSpec((1, tm, half), tok)],
        out_shape=[jax.ShapeDtypeStruct((b, l, half), BF16),
                   jax.ShapeDtypeStruct((b, l, half), BF16)],
        compiler_params=_cparams(2, 48),
        name="mlp_in_proj",
    )(x, mods, g, w, vg)


MLP_GROUPS_PER_STEP = 4
MLP_COL_PITCH = GRID_W + SUBLANES


def _mlp_mix_kernel(u_ref, v_ref, ws_ref, bs_ref, w_ref, x_ref, mod_ref, g_ref, rw_ref,
                    xo_ref, h2_ref, aff_ref, acc_ref, vf_ref, sf_ref, us_ref, *, rows, mix_steps):
    gp = pl.program_id(1)
    t = u_ref.shape[1]
    n_chunks = t // MLP_CHUNK
    per_step = ws_ref.shape[0]
    row_steps = MLP_ROW_GROUPS // per_step
    cols_per_chunk = MLP_CHUNK // rows
    pitch = MLP_COL_PITCH

    @pl.when(gp < row_steps)
    def _():
        for i in range(per_step):
            cs = slice(i * LANES, (i + 1) * LANES)
            for c in range(n_chunks):
                rs = slice(c * MLP_CHUNK, (c + 1) * MLP_CHUNK)
                s = _dot(ws_ref[i], v_ref[0, rs, cs]) + bs_ref[i]
                us_ref[rs, cs] = (u_ref[0, rs, cs].astype(F32) * s).astype(BF16)
        acc_ref[...] = _dot(us_ref[...], w_ref[...]) + jnp.where(gp == 0, 0.0, acc_ref[...])

    @pl.when(jnp.logical_and(gp >= row_steps, gp < mix_steps))
    def _():
        for i in range(per_step):
            cs = slice(i * LANES, (i + 1) * LANES)
            for r in range(rows):
                vf_ref[r * pitch:r * pitch + GRID_W, :] = v_ref[0, r * GRID_W:(r + 1) * GRID_W, cs].astype(F32)
            for k in range(n_chunks):
                xk = jnp.concatenate(
                    [vf_ref[pl.ds(k * cols_per_chunk + wl, rows, stride=pitch), :] for wl in range(cols_per_chunk)],
                    axis=0)
                s = _dot(ws_ref[i], xk.astype(BF16)) + bs_ref[i]
                for wl in range(cols_per_chunk):
                    sf_ref[pl.ds(k * cols_per_chunk + wl, rows, stride=pitch), :] = s[wl * rows:(wl + 1) * rows, :]
            for r in range(rows):
                rr = slice(r * GRID_W, (r + 1) * GRID_W)
                us_ref[rr, cs] = (u_ref[0, rr, cs].astype(F32) * sf_ref[r * pitch:r * pitch + GRID_W, :]).astype(BF16)
        acc_ref[...] += _dot(us_ref[...], w_ref[...])

    @pl.when(gp >= mix_steps)
    def _():
        sub = TOKEN_TILE
        blk = x_ref.shape[1]
        for r in range(blk // sub):
            row0 = pl.multiple_of((gp - mix_steps) * blk + r * sub, sub)
            rs = slice(r * sub, (r + 1) * sub)
            xn, hb, aff = _mixer_epilogue(acc_ref[pl.ds(row0, sub), :], x_ref[0, rs, :], mod_ref, g_ref, rw_ref)
            xo_ref[0, rs, :] = xn
            h2_ref[0, rs, :] = hb
            aff_ref[0, :, rs] = aff


def _mlp_mix(u, v, ws, bsb, w, x, mods, g, rw2):
    b, l, inner = u.shape
    d = w.shape[1]
    per_step = MLP_GROUPS_PER_STEP
    mix_steps = MLP_GROUPS // per_step
    kc = per_step * LANES
    rows = l // GRID_W
    tm = 2 * TOKEN_TILE
    epi_steps = l // tm
    last = mix_steps - 1
    mix = lambda s: jnp.minimum(s, last)
    epi = lambda s: jnp.maximum(s - mix_steps, 0)
    return pl.pallas_call(
        functools.partial(_mlp_mix_kernel, rows=rows, mix_steps=mix_steps),
        grid=(b, mix_steps + epi_steps),
        in_specs=[
            pl.BlockSpec((1, l, kc), lambda i, s: (i, 0, mix(s))),
            pl.BlockSpec((1, l, kc), lambda i, s: (i, 0, mix(s))),
            pl.BlockSpec((per_step, MLP_CHUNK, MLP_CHUNK), lambda i, s: (mix(s), 0, 0)),
            pl.BlockSpec((per_step, MLP_CHUNK, LANES), lambda i, s: (mix(s), 0, 0)),
            pl.BlockSpec((kc, d), lambda i, s: (mix(s), 0)),
            pl.BlockSpec((1, tm, d), lambda i, s: (i, epi(s), 0)),
            pl.BlockSpec((1, SUBLANES, d), lambda i, s: (i, 0, 0)),
            pl.BlockSpec((SUBLANES, d), lambda i, s: (0, 0)),
            pl.BlockSpec(rw2.shape, lambda i, s: (0, 0)),
        ],
        out_specs=[pl.BlockSpec((1, tm, d), lambda i, s: (i, epi(s), 0)),
                   pl.BlockSpec((1, tm, d), lambda i, s: (i, epi(s), 0)),
                   pl.BlockSpec((1, N_EXPERTS, tm), lambda i, s: (i, 0, epi(s)))],
        out_shape=[jax.ShapeDtypeStruct((b, l, d), F32),
                   jax.ShapeDtypeStruct((b, l, d), BF16),
                   jax.ShapeDtypeStruct((b, N_EXPERTS, l), F32)],
        scratch_shapes=[pltpu.VMEM((l, d), F32),
                        pltpu.VMEM((rows * MLP_COL_PITCH, LANES), F32),
                        pltpu.VMEM((rows * MLP_COL_PITCH, LANES), F32),
                        pltpu.VMEM((l, kc), BF16)],
        compiler_params=_cparams(2, 48),
        name="mlp_mix",
    )(u, v, ws, bsb, w, x, mods, g, rw2)


def _pad_rows(a, rows):
    return jnp.pad(a, ((0, rows - a.shape[0]),) + ((0, 0),) * (a.ndim - 1))


def kernel(x, c, ctx, c_ctx, mod_w, mod_b, norm_g, ssd_in_w, ssd_conv_w, ssd_conv_b, ssd_dt_bias, ssd_a_log,
           ssd_d, ssd_norm_g, ssd_out_w, mlp_in_w, mlp_v_g, mlp_ws, mlp_bs, mlp_out_w, router_w, exp_w_gate,
           exp_w_up, exp_w_down):
    b, l, d = x.shape
    inner = SSD_HEADS * SSD_HEADDIM
    n_chunks = l // SSD_CHUNK
    n_ctx_chunks = ctx.shape[1] // SSD_CHUNK

    crows = _pad_rows(jnp.concatenate([c, c_ctx[None, :]], axis=0), -(-(b + 1) // SUBLANES) * SUBLANES)
    mod = _modulation(crows, mod_w, mod_b)
    mods_lat = [jnp.pad(mod[i, :b].reshape(b, N_MOD, d), ((0, 0), (0, SUBLANES - N_MOD), (0, 0)))
                for i in range(2)]
    mods_ctx = jnp.pad(mod[0, b].reshape(1, N_MOD, d), ((0, 0), (0, SUBLANES - N_MOD), (0, 0)))
    mods01 = jnp.stack([jnp.broadcast_to(mods_ctx, (b, SUBLANES, d)), mods_lat[0]], axis=1)
    gains = [_pad_rows(norm_g[i], SUBLANES) for i in range(2)]
    rw2 = []
    for i in range(2):
        rw_t = router_w[i].T
        rw_hi = rw_t.astype(BF16)
        rw2.append(jnp.concatenate([rw_hi, (rw_t - rw_hi.astype(F32)).astype(BF16)], axis=0))

    in_w = ssd_in_w[0]
    conv_dim = ssd_conv_w.shape[2]
    w_in = jnp.pad(in_w, ((0, 0), (0, LANES - 2 * SSD_HEADS))).astype(BF16)
    dt_bias = jnp.pad(ssd_dt_bias[0].reshape(1, 2 * SSD_HEADS), ((0, 0), (0, LANES - 2 * SSD_HEADS)))
    z, xs_all, bc_all, dt_all = _ssd_in_proj(
        ctx, x, mods01, gains[0], w_in, _pad_rows(ssd_conv_w[0], SUBLANES), ssd_conv_b[0].reshape(1, conv_dim),
        dt_bias)

    alog = _pad_rows(jnp.stack([jnp.pad(ssd_a_log[0, 0], (0, LANES - SSD_HEADS)),
                                jnp.pad(ssd_a_log[0, 1], (SSD_HEADS, LANES - 2 * SSD_HEADS))]), SUBLANES)
    head_of_col = jnp.arange(inner, dtype=jnp.int32) // SSD_HEADDIM
    rows128 = jnp.arange(LANES, dtype=jnp.int32)[:, None]
    expand = jnp.stack([(rows128 == head_of_col[None, :] + SSD_HEADS * dd) for dd in range(2)]).astype(BF16)
    dskip = jnp.repeat(ssd_d[0], SSD_HEADDIM).reshape(1, inner)
    y = _ssd_scan(xs_all, bc_all, dt_all, alog, expand, dskip, n_ctx_chunks, n_chunks)

    xa, h2, aff_t = _ssd_out(y, z, x, mods_lat[0], gains[0], ssd_norm_g[0].reshape(1, inner),
                             ssd_out_w[0].astype(BF16), rw2[0])
    x1 = _moe(h2, aff_t, exp_w_gate, exp_w_up, exp_w_down, 0, xa, mods_lat[0], gains[0])

    u, v = _mlp_in(x1, mods_lat[1], gains[1], mlp_in_w[0].astype(BF16), mlp_v_g[0].reshape(1, -1))
    bsb = jnp.broadcast_to(mlp_bs[0][:, :, None], (MLP_GROUPS, MLP_CHUNK, LANES))
    xb, h2b, aff_tb = _mlp_mix(u, v, mlp_ws[0].astype(BF16), bsb, mlp_out_w[0].astype(BF16), x1, mods_lat[1],
                               gains[1], rw2[1])
    return _moe(h2b, aff_tb, exp_w_gate, exp_w_up, exp_w_down, 1, xb, mods_lat[1], gains[1])
```

```python
import functools

import jax
import jax.numpy as jnp
from jax import lax
from jax.experimental import pallas as pl
from jax.experimental.pallas import tpu as pltpu

F32 = jnp.float32
BF16 = jnp.bfloat16
HIGHEST = lax.Precision.HIGHEST
EPS = 1e-6

LANES = 128
SUBLANES = 8
MIB = 1024 * 1024

N_MOD = 6
GRID_W = 64
SSD_HEADDIM = 64
SSD_HEADS = 32
SSD_GROUPS = 4
SSD_HPG = SSD_HEADS // SSD_GROUPS
SSD_STATE = 128
SSD_CONV = 5
SSD_CHUNK = 128
MLP_CHUNK = 128
MLP_GROUPS = 16
MLP_ROW_GROUPS = 8
N_EXPERTS = 16
CAPACITY_FACTOR = 2

CONV_HALO = SUBLANES
TOKEN_TILE = 256
CONV_COL_BLOCK = 1024


def _cparams(n_axes, vmem_mib):
    return pltpu.CompilerParams(dimension_semantics=("arbitrary",) * n_axes,
                                vmem_limit_bytes=vmem_mib * MIB)


def _silu(x):
    h = 0.5 * x
    return h + h * jnp.tanh(h)


def _rms(x, g):
    return x * lax.rsqrt(jnp.mean(x * x, axis=-1, keepdims=True) + EPS) * g


def _dot(a, b):
    return jnp.dot(a, b, preferred_element_type=F32)


def _dot_nt(a, b, precision=None):
    return lax.dot_general(a, b, (((1,), (1,)), ((), ())), preferred_element_type=F32, precision=precision)


def _dot_tn(a, b):
    return lax.dot_general(a, b, (((0,), (0,)), ((), ())), preferred_element_type=F32)


def _mod_kernel(c_ref, w_ref, b_ref, o_ref):
    c = c_ref[...]
    s = _silu(c)
    o_ref[0] = jnp.dot(s, w_ref[0], preferred_element_type=F32, precision=HIGHEST) + b_ref[0]


def _modulation(crows, mod_w, mod_b):
    depth, d, n = mod_w.shape
    rows = crows.shape[0]
    tn = 1536
    return pl.pallas_call(
        _mod_kernel,
        grid=(depth, n // tn),
        in_specs=[pl.BlockSpec((rows, d), lambda i, j: (0, 0)),
                  pl.BlockSpec((1, d, tn), lambda i, j: (i, 0, j)),
                  pl.BlockSpec((1, 1, tn), lambda i, j: (i, 0, j))],
        out_specs=pl.BlockSpec((1, rows, tn), lambda i, j: (i, 0, j)),
        out_shape=jax.ShapeDtypeStruct((depth, rows, n), F32),
        compiler_params=_cparams(2, 40),
        name="modulation",
    )(crows, mod_w, mod_b.reshape(depth, 1, n))


def _ssd_in_kernel(ctx_ref, x_ref, xp_ref, xn_ref, mod_ref, g_ref, w_ref, cw_ref, cb_ref, dtb_ref, sh_ref,
                   z_ref, xs_ref, bc_ref, dt_ref, *, n_tiles, inner, conv_dim):
    j = pl.program_id(1)
    tm = x_ref.shape[1]
    halo = xp_ref.shape[1]
    xc = jnp.where(j == 0, ctx_ref[0], x_ref[0])
    xa = jnp.concatenate([xp_ref[0], xc, xn_ref[0]], axis=0)
    h = (_rms(xa, g_ref[0:1, :]) * (1.0 + mod_ref[0, 0, 1:2, :]) + mod_ref[0, 0, 0:1, :]).astype(BF16)
    hc = h[halo:halo + tm, :]
    z_ref[0] = _dot(hc, w_ref[:, :inner]).astype(BF16)
    dtr = _dot(hc, w_ref[:, inner + conv_dim:]) + dtb_ref[...]
    dt_ref[0] = jnp.maximum(dtr, 0.0) + jnp.log1p(jnp.exp(-jnp.abs(dtr)))
    keep_top = jnp.where(j <= 1, 0.0, 1.0)
    keep_bot = jnp.where(jnp.logical_or(j == 0, j == n_tiles - 1), 0.0, 1.0)
    hrow = lax.broadcasted_iota(jnp.int32, (halo, 1), 0)
    pad = (SSD_CONV - 1) // 2
    taps = [k for k in range(SSD_CONV) if abs(k - pad) > 1]
    near_taps = [k for k in range(SSD_CONV) if abs(k - pad) == 1]
    rows = tm + 2 * halo
    cblk = CONV_COL_BLOCK
    for cbi in range(conv_dim // cblk):
        c0 = cbi * cblk
        u = _dot(h, w_ref[:, inner + c0:inner + c0 + cblk])
        uc = u[halo:halo + tm, :]
        top = u[:halo, :] * keep_top
        bot = u[halo + tm:, :] * keep_bot
        acc = cb_ref[:, c0:c0 + cblk] + cw_ref[pad:pad + 1, c0:c0 + cblk] * uc
        um = jnp.concatenate([top, uc, bot], axis=0)
        for k in near_taps:
            acc = acc + cw_ref[k:k + 1, c0:c0 + cblk] * pltpu.roll(um, (pad - k) % rows, axis=0)[halo:halo + tm, :]
        shifted = _dot(sh_ref[...], uc.astype(BF16))
        corr_top = jnp.zeros((halo, cblk), F32)
        corr_bot = jnp.zeros((halo, cblk), F32)
        for jj, k in enumerate(taps):
            wk = cw_ref[k:k + 1, c0:c0 + cblk]
            acc = acc + wk * shifted[jj * tm:(jj + 1) * tm, :]
            d = k - pad
            if d < 0:
                corr_top = corr_top + wk * jnp.where(hrow < -d, pltpu.roll(top, -d, axis=0), 0.0)
            else:
                corr_bot = corr_bot + wk * jnp.where(hrow >= halo - d, pltpu.roll(bot, halo - d, axis=0), 0.0)
        acc = jnp.concatenate([acc[:halo, :] + corr_top, acc[halo:tm - halo, :], acc[tm - halo:, :] + corr_bot],
                              axis=0)
        act = _silu(acc).astype(BF16)
        if c0 < inner:
            xs_ref[0, :, c0:c0 + cblk] = act
        else:
            bc_ref[0, :, c0 - inner:c0 - inner + cblk] = act


def _ssd_in_proj(ctx, x, mods01, g, w, conv_w, conv_b, dt_bias):
    b, l, d = x.shape
    lc = ctx.shape[1]
    tm = TOKEN_TILE
    assert lc == tm and l % tm == 0
    n_tiles = 1 + l // tm
    inner = SSD_HEADS * SSD_HEADDIM
    conv_dim = conv_w.shape[1]
    bc_dim = conv_dim - inner
    halo = CONV_HALO
    blocks_per_tile = tm // halo
    last_halo_block = l // halo - 1
    lt = lc + l
    pad = (SSD_CONV - 1) // 2
    t_idx = jnp.arange(tm, dtype=jnp.int32)
    shifts = jnp.concatenate([t_idx[None, :] == t_idx[:, None] + (k - pad)
                              for k in range(SSD_CONV) if abs(k - pad) > 1],
                             axis=0).astype(BF16)
    kern = functools.partial(_ssd_in_kernel, n_tiles=n_tiles, inner=inner, conv_dim=conv_dim)
    return pl.pallas_call(
        kern,
        grid=(b, n_tiles),
        in_specs=[
            pl.BlockSpec((1, lc, d), lambda i, j: (i, 0, 0)),
            pl.BlockSpec((1, tm, d), lambda i, j: (i, jnp.maximum(j - 1, 0), 0)),
            pl.BlockSpec((1, halo, d), lambda i, j: (i, jnp.maximum((j - 1) * blocks_per_tile - 1, 0), 0)),
            pl.BlockSpec((1, halo, d), lambda i, j: (i, jnp.minimum(j * blocks_per_tile, last_halo_block), 0)),
            pl.BlockSpec((1, 1, SUBLANES, d), lambda i, j: (i, jnp.minimum(j, 1), 0, 0)),
            pl.BlockSpec((SUBLANES, d), lambda i, j: (0, 0)),
            pl.BlockSpec(w.shape, lambda i, j: (0, 0), pipeline_mode=pl.Buffered(1)),
            pl.BlockSpec((SUBLANES, conv_dim), lambda i, j: (0, 0)),
            pl.BlockSpec((1, conv_dim), lambda i, j: (0, 0)),
            pl.BlockSpec((1, LANES), lambda i, j: (0, 0)),
            pl.BlockSpec(shifts.shape, lambda i, j: (0, 0)),
        ],
        out_specs=[
            pl.BlockSpec((1, tm, inner), lambda i, j: (i, jnp.maximum(j - 1, 0), 0)),
            pl.BlockSpec((1, tm, inner), lambda i, j: (i, j, 0)),
            pl.BlockSpec((1, tm, bc_dim), lambda i, j: (i, j, 0)),
            pl.BlockSpec((1, tm, LANES), lambda i, j: (i, j, 0)),
        ],
        out_shape=[
            jax.ShapeDtypeStruct((b, l, inner), BF16),
            jax.ShapeDtypeStruct((b, lt, inner), BF16),
            jax.ShapeDtypeStruct((b, lt, bc_dim), BF16),
            jax.ShapeDtypeStruct((b, lt, LANES), F32),
        ],
        compiler_params=_cparams(2, 56),
        name="ssd_in_proj",
    )(ctx, x, x, x, mods01, g, w, conv_w, conv_b, dt_bias, shifts)


SSD_CHUNKS_PER_STEP = 2
SSD_DECAY_ROWS = 16


def _split3(x):
    hi = x.astype(BF16)
    r1 = x - hi.astype(F32)
    mid = r1.astype(BF16)
    lo = (r1 - mid.astype(F32)).astype(BF16)
    return hi, mid, lo


def _ssd_step(rev, lat, lblk, xs_ref, bc_ref, dt_ref, alog_ref, e_ref, dskip_ref, y_ref, st_ref, ybuf_ref):
    c = SSD_CHUNK
    nck = SSD_CHUNKS_PER_STEP
    dcol = SSD_HEADS * int(rev)
    gw = SSD_HPG * SSD_HEADDIM
    bc_off = SSD_GROUPS * SSD_STATE
    a_neg = -jnp.exp(alog_ref[int(rev):int(rev) + 1, :])
    li = lax.broadcasted_iota(jnp.int32, (c, c), 0)
    si = lax.broadcasted_iota(jnp.int32, (c, c), 1)
    causal = (si >= li) if rev else (si <= li)
    tri = jnp.where(causal, 1.0, 0.0).astype(BF16)
    e = e_ref[int(rev)]
    head = lax.broadcasted_iota(jnp.int32, (c, LANES), 1)

    dts = [dt_ref[0, k * c:(k + 1) * c, :] for k in range(nck)]
    pieces = []
    for k in range(nck):
        pieces.extend(_split3(dts[k] * a_neg))
    run = _dot(tri, jnp.concatenate(pieces, axis=1))
    css = [run[:, (3 * k) * LANES:(3 * k + 1) * LANES] + run[:, (3 * k + 1) * LANES:(3 * k + 2) * LANES]
           + run[:, (3 * k + 2) * LANES:(3 * k + 3) * LANES] for k in range(nck)]
    tots = [cs[0:1, :] if rev else cs[c - 1:c, :] for cs in css]

    blocks = []
    for k in range(nck):
        blocks.append((jnp.exp(tots[k] - css[k]) * dts[k]).astype(BF16))
        blocks.append(jnp.exp(css[k]).astype(BF16))
    drow = lax.broadcasted_iota(jnp.int32, (SSD_DECAY_ROWS, LANES), 0)
    for k in range(nck):
        hi, mid, lo = _split3(jnp.broadcast_to(jnp.exp(tots[k]), (SSD_DECAY_ROWS, LANES)))
        rows3 = jnp.where(drow == 0, hi.astype(F32),
                          jnp.where(drow == 1, mid.astype(F32), jnp.where(drow == 2, lo.astype(F32), 0.0)))
        blocks.append(rows3.astype(BF16))
    ex = _dot(jnp.concatenate(blocks, axis=0), e).astype(BF16)
    dec0 = 2 * nck * c
    even_cols = jnp.where(head < SSD_HEADDIM, 1.0, 0.0).astype(BF16)
    odd_cols = jnp.where(head < SSD_HEADDIM, 0.0, 1.0).astype(BF16)

    for k in (range(nck - 1, -1, -1) if rev else range(nck)):
        base = 2 * k * c
        xs = xs_ref[0, k * c:(k + 1) * c, :]
        bc = bc_ref[0, k * c:(k + 1) * c, :]
        xw = xs * ex[base:base + c, :]

        def y_part(k=k, base=base, xs=xs, bc=bc):
            lc = lblk * nck + k
            cs = css[k]
            rs_t = (cs - jnp.log(dts[k])).T
            for g in range(SSD_GROUPS):
                bg = bc[:, g * SSD_STATE:(g + 1) * SSD_STATE]
                cg = bc[:, bc_off + g * SSD_STATE:bc_off + (g + 1) * SSD_STATE]
                px = ex[base + c:base + 2 * c, g * gw:(g + 1) * gw].astype(F32)
                y_off = _dot(cg, st_ref[:, g * gw:(g + 1) * gw].astype(BF16)) * px
                cb = _dot_nt(cg, bg)
                for hp in range(SSD_HPG // 2):
                    ms = []
                    for hh in range(2):
                        hd = dcol + g * SSD_HPG + 2 * hp + hh
                        diff = cs[:, hd:hd + 1] - rs_t[hd:hd + 1, :]
                        ms.append(cb * jnp.exp(jnp.where(causal, diff, -jnp.inf)))
                    mp = jnp.concatenate(ms, axis=1).astype(BF16)
                    col0 = (g * SSD_HPG + 2 * hp) * SSD_HEADDIM
                    xp = xs[:, col0:col0 + LANES]
                    rhs = jnp.concatenate([xp * even_cols, xp * odd_cols], axis=0)
                    y_pair = _dot(mp, rhs) + y_off[:, 2 * hp * SSD_HEADDIM:2 * hp * SSD_HEADDIM + LANES]
                    if rev:
                        tot_y = (ybuf_ref[lc, :, col0:col0 + LANES] + y_pair
                                 + dskip_ref[:, col0:col0 + LANES] * xp.astype(F32))
                        y_ref[0, k * c:(k + 1) * c, col0:col0 + LANES] = tot_y.astype(BF16)
                    else:
                        ybuf_ref[lc, :, col0:col0 + LANES] = y_pair

        if lat:
            y_part()

        drows = ex[dec0 + k * SSD_DECAY_ROWS:dec0 + (k + 1) * SSD_DECAY_ROWS, :].astype(F32)
        decay = drows[0:1, :] + drows[1:2, :] + drows[2:3, :]
        for g in range(SSD_GROUPS):
            bg = bc[:, g * SSD_STATE:(g + 1) * SSD_STATE]
            upd = _dot_tn(bg, xw[:, g * gw:(g + 1) * gw])
            st_ref[:, g * gw:(g + 1) * gw] = st_ref[:, g * gw:(g + 1) * gw] * decay[:, g * gw:(g + 1) * gw] + upd


def _ssd_scan_kernel(xs_ref, bc_ref, dt_ref, alog_ref, e_ref, dskip_ref, y_ref, st_ref, ybuf_ref, *,
                     n_ctx_blocks, n_blocks):
    d = pl.program_id(1)
    s = pl.program_id(2)

    @pl.when(s == 0)
    def _():
        st_ref[...] = jnp.zeros_like(st_ref)

    for rev in (False, True):
        if rev:
            blk = jnp.where(s < n_ctx_blocks, n_ctx_blocks - 1 - s, n_blocks + 2 * n_ctx_blocks - 1 - s)
        else:
            blk = s
        for lat in (False, True):
            in_part = (blk >= n_ctx_blocks) if lat else (blk < n_ctx_blocks)

            @pl.when(jnp.logical_and(d == int(rev), in_part))
            def _(rev=rev, lat=lat, blk=blk):
                _ssd_step(rev, lat, jnp.maximum(blk - n_ctx_blocks, 0), xs_ref, bc_ref, dt_ref,
                          alog_ref, e_ref, dskip_ref, y_ref, st_ref, ybuf_ref)


def _ssd_scan(xs_all, bc_all, dt_all, alog, expand, dskip, n_ctx_chunks, n_chunks):
    b, lt, inner = xs_all.shape
    nck = SSD_CHUNKS_PER_STEP
    rows = nck * SSD_CHUNK
    assert n_ctx_chunks % nck == 0 and n_chunks % nck == 0
    n_ctx_blocks = n_ctx_chunks // nck
    n_blocks = n_chunks // nck
    steps = n_ctx_blocks + n_blocks

    def block_idx(d, s):
        bwd = jnp.where(s < n_ctx_blocks, n_ctx_blocks - 1 - s, steps + n_ctx_blocks - 1 - s)
        return jnp.where(d == 0, s, bwd)

    def out_idx(d, s):
        return jnp.where(jnp.logical_and(d == 1, s >= n_ctx_blocks), steps - 1 - s, n_blocks - 1)

    kern = functools.partial(_ssd_scan_kernel, n_ctx_blocks=n_ctx_blocks, n_blocks=n_blocks)
    return pl.pallas_call(
        kern,
        grid=(b, 2, steps),
        in_specs=[
            pl.BlockSpec((1, rows, inner), lambda i, d, s: (i, block_idx(d, s), 0)),
            pl.BlockSpec((1, rows, bc_all.shape[2]), lambda i, d, s: (i, block_idx(d, s), 0)),
            pl.BlockSpec((1, rows, LANES), lambda i, d, s: (i, block_idx(d, s), 0)),
            pl.BlockSpec((SUBLANES, LANES), lambda i, d, s: (0, 0)),
            pl.BlockSpec((2, LANES, inner), lambda i, d, s: (0, 0, 0)),
            pl.BlockSpec((1, inner), lambda i, d, s: (0, 0)),
        ],
        out_specs=pl.BlockSpec((1, rows, inner), lambda i, d, s: (i, out_idx(d, s), 0)),
        out_shape=jax.ShapeDtypeStruct((b, n_chunks * SSD_CHUNK, inner), BF16),
        scratch_shapes=[pltpu.VMEM((SSD_STATE, inner), F32),
                        pltpu.VMEM((n_chunks, SSD_CHUNK, inner), F32)],
        compiler_params=_cparams(3, 56),
        name="ssd_scan",
    )(xs_all, bc_all, dt_all, alog, expand, dskip)


def _mixer_epilogue(o, x, mod_ref, g_ref, rw_ref):
    xn = x + mod_ref[0, 2:3, :] * _rms(o, g_ref[1:2, :])
    h2 = _rms(xn, g_ref[2:3, :]) * (1.0 + mod_ref[0, 4:5, :]) + mod_ref[0, 3:4, :]
    h_hi = h2.astype(BF16)
    h_lo = (h2 - h_hi.astype(F32)).astype(BF16)
    ne = rw_ref.shape[0] // 2
    both = _dot_nt(rw_ref[...], h_hi)
    lg = both[:ne, :] + both[ne:, :] + _dot_nt(rw_ref[0:ne, :], h_lo)
    ex = jnp.exp(lg - jnp.max(lg, axis=0, keepdims=True))
    return xn, h_hi, ex / jnp.sum(ex, axis=0, keepdims=True)


def _ssd_out_kernel(y_ref, z_ref, x_ref, mod_ref, g_ref, ng_ref, w_ref, rw_ref, xo_ref, h2_ref, aff_ref):
    sub = TOKEN_TILE
    gdim = y_ref.shape[2] // SSD_GROUPS
    for r in range(y_ref.shape[1] // sub):
        rs = slice(r * sub, (r + 1) * sub)
        z = z_ref[0, rs, :].astype(F32)
        yg = y_ref[0, rs, :].astype(F32) * _silu(z)
        parts = [_rms(yg[:, g * gdim:(g + 1) * gdim], ng_ref[:, g * gdim:(g + 1) * gdim]).astype(BF16)
                 for g in range(SSD_GROUPS)]
        o = _dot(jnp.concatenate(parts, axis=1), w_ref[...])
        xn, hb, aff = _mixer_epilogue(o, x_ref[0, rs, :], mod_ref, g_ref, rw_ref)
        xo_ref[0, rs, :] = xn
        h2_ref[0, rs, :] = hb
        aff_ref[0, :, rs] = aff


def _epilogue_outs(b, l, d, tm):
    out_specs = [pl.BlockSpec((1, tm, d), lambda i, j: (i, j, 0)),
                 pl.BlockSpec((1, tm, d), lambda i, j: (i, j, 0)),
                 pl.BlockSpec((1, N_EXPERTS, tm), lambda i, j: (i, 0, j))]
    out_shape = [jax.ShapeDtypeStruct((b, l, d), F32),
                 jax.ShapeDtypeStruct((b, l, d), BF16),
                 jax.ShapeDtypeStruct((b, N_EXPERTS, l), F32)]
    return out_specs, out_shape


def _ssd_out(y, z, x, mods, g, ng, w, rw2):
    b, l, d = x.shape
    inner = y.shape[2]
    tm = 2 * TOKEN_TILE
    out_specs, out_shape = _epilogue_outs(b, l, d, tm)
    return pl.pallas_call(
        _ssd_out_kernel,
        grid=(b, l // tm),
        in_specs=[
            pl.BlockSpec((1, tm, inner), lambda i, j: (i, j, 0)),
            pl.BlockSpec((1, tm, inner), lambda i, j: (i, j, 0)),
            pl.BlockSpec((1, tm, d), lambda i, j: (i, j, 0)),
            pl.BlockSpec((1, SUBLANES, d), lambda i, j: (i, 0, 0)),
            pl.BlockSpec((SUBLANES, d), lambda i, j: (0, 0)),
            pl.BlockSpec((1, inner), lambda i, j: (0, 0)),
            pl.BlockSpec(w.shape, lambda i, j: (0, 0)),
            pl.BlockSpec(rw2.shape, lambda i, j: (0, 0)),
        ],
        out_specs=out_specs,
        out_shape=out_shape,
        compiler_params=_cparams(2, 48),
        name="ssd_out",
    )(y, z, x, mods, g, ng, w, rw2)


def _lane_prefix_exclusive(m01, upper):
    e, t = m01.shape
    carry = jnp.zeros((e, 1), F32)
    outs = []
    for k in range(t // LANES):
        tile = m01[:, k * LANES:(k + 1) * LANES]
        incl = _dot(tile.astype(BF16), upper)
        outs.append(incl - tile + carry)
        carry = carry + incl[:, LANES - 1:LANES]
    return jnp.concatenate(outs, axis=1)


def _route(aff, cap):
    e, t = aff.shape
    key = pltpu.bitcast(aff, jnp.int32)

    def enough(cand):
        return jnp.sum(jnp.where(key >= cand, 1.0, 0.0), axis=1, keepdims=True) >= cap

    def body(i, thr):
        sh = 29 - 2 * i
        c1, c2, c3 = (jnp.bitwise_or(thr, jnp.left_shift(jnp.int32(v), sh)) for v in (1, 2, 3))
        return jnp.where(enough(c3), c3, jnp.where(enough(c2), c2, jnp.where(enough(c1), c1, thr)))

    thr = lax.fori_loop(0, 15, body, jnp.zeros((e, 1), jnp.int32))
    last = jnp.bitwise_or(thr, 1)
    thr = jnp.where(enough(last), last, thr)
    gt = jnp.where(key > thr, 1.0, 0.0)
    eq = jnp.where(key == thr, 1.0, 0.0)
    need = cap - jnp.sum(gt, axis=1, keepdims=True)
    r = lax.broadcasted_iota(jnp.int32, (LANES, LANES), 0)
    cidx = lax.broadcasted_iota(jnp.int32, (LANES, LANES), 1)
    upper = jnp.where(r <= cidx, 1.0, 0.0).astype(BF16)
    eq_rank = _lane_prefix_exclusive(eq, upper)
    sel = gt + eq * jnp.where(eq_rank < need, 1.0, 0.0)
    pos = _lane_prefix_exclusive(sel, upper)
    return jnp.where(sel > 0.5, pos, -1.0).astype(jnp.int32)


MOE_EXPERTS_PER_STEP = 4
MOE_FFN_TILE = 1024


def _pick_rows(slot_ref, e, cap, t):
    srow = slot_ref[0, pl.ds(e, 1), :]
    return lax.broadcasted_iota(jnp.int32, (cap, t), 0) == srow


def _moe_gather_kernel(h_ref, aff_ref, xin_ref, slot_ref, *, cap):
    eg = pl.program_id(1)

    @pl.when(eg == 0)
    def _():
        slot_ref[0] = _route(aff_ref[0], cap)

    t = h_ref.shape[1]
    for i0 in range(0, MOE_EXPERTS_PER_STEP, 2):
        ps = [jnp.where(_pick_rows(slot_ref, eg * MOE_EXPERTS_PER_STEP + i0 + i, cap, t), 1.0, 0.0).astype(BF16)
              for i in range(2)]
        xin = _dot(jnp.concatenate(ps, axis=0), h_ref[0])
        for i in range(2):
            xin_ref[i0 + i] = xin[i * cap:(i + 1) * cap, :].astype(BF16)


def _moe_gather(h2, aff_t):
    b, t, d = h2.shape
    ne = aff_t.shape[1]
    cap = (CAPACITY_FACTOR * t) // ne
    g = MOE_EXPERTS_PER_STEP
    return pl.pallas_call(
        functools.partial(_moe_gather_kernel, cap=cap),
        grid=(b, ne // g),
        in_specs=[pl.BlockSpec((1, t, d), lambda i, e: (i, 0, 0)),
                  pl.BlockSpec((1, ne, t), lambda i, e: (i, 0, 0))],
        out_specs=[pl.BlockSpec((g, cap, d), lambda i, e: (e, i, 0)),
                   pl.BlockSpec((1, ne, t), lambda i, e: (i, 0, 0))],
        out_shape=[jax.ShapeDtypeStruct((ne, b * cap, d), BF16),
                   jax.ShapeDtypeStruct((b, ne, t), jnp.int32)],
        compiler_params=_cparams(2, 48),
        name="moe_gather",
    )(h2, aff_t)


def _moe_ffn_kernel(x_ref, wg_ref, wu_ref, wd_ref, y_ref, acc_ref):
    half = pl.program_id(2)

    def half_ffn():
        x = x_ref[0]
        hid = _dot(x, wg_ref[0, 0].astype(BF16))
        hid = _silu(hid) * _dot(x, wu_ref[0, 0].astype(BF16))
        return _dot(hid.astype(BF16), wd_ref[0, 0].astype(BF16))

    @pl.when(half == 0)
    def _():
        acc_ref[...] = half_ffn()

    @pl.when(half == 1)
    def _():
        y_ref[0] = (acc_ref[...] + half_ffn()).astype(BF16)


def _moe_ffn(xin, w_gate, w_up, w_down, layer):
    ne, m, d = xin.shape
    ff = w_gate.shape[3]
    tm = MOE_FFN_TILE
    hf = ff // 2
    return pl.pallas_call(
        _moe_ffn_kernel,
        grid=(ne, m // tm, 2),
        in_specs=[pl.BlockSpec((1, tm, d), lambda e, j, h: (e, j, 0)),
                  pl.BlockSpec((1, 1, d, hf), lambda e, j, h: (layer, e, 0, h)),
                  pl.BlockSpec((1, 1, d, hf), lambda e, j, h: (layer, e, 0, h)),
                  pl.BlockSpec((1, 1, hf, d), lambda e, j, h: (layer, e, h, 0))],
        out_specs=pl.BlockSpec((1, tm, d), lambda e, j, h: (e, j, 0)),
        out_shape=jax.ShapeDtypeStruct((ne, m, d), BF16),
        scratch_shapes=[pltpu.VMEM((tm, d), F32)],
        compiler_params=_cparams(3, 56),
        name="moe_ffn",
    )(xin, w_gate, w_up, w_down)


def _moe_scatter_kernel(slot_ref, aff_ref, y_ref, x_ref, mod_ref, g_ref, o_ref, xst_ref, *, cap):
    eg = pl.program_id(1)
    t = o_ref.shape[1]
    q = x_ref.shape[1]
    xst_ref[pl.ds(pl.multiple_of(eg * q, q), q), :] = x_ref[0]

    def contribution():
        ps = []
        for i in range(MOE_EXPERTS_PER_STEP):
            e = eg * MOE_EXPERTS_PER_STEP + i
            ps.append(jnp.where(_pick_rows(slot_ref, e, cap, t), aff_ref[0, pl.ds(e, 1), :], 0.0).astype(BF16))
        p = jnp.concatenate(ps, axis=0)
        y = y_ref[...].reshape(MOE_EXPERTS_PER_STEP * cap, y_ref.shape[2])
        return _dot_tn(p, y)

    @pl.when(eg == 0)
    def _():
        o_ref[0] = contribution()

    @pl.when(eg > 0)
    def _():
        o_ref[0] += contribution()

    @pl.when(eg == pl.num_programs(1) - 1)
    def _():
        sub = TOKEN_TILE
        for r in range(t // sub):
            rs = slice(r * sub, (r + 1) * sub)
            o_ref[0, rs, :] = xst_ref[rs, :] + mod_ref[0, 5:6, :] * _rms(o_ref[0, rs, :], g_ref[3:4, :])


def _moe_scatter(slot, aff_t, y, x, mods, g):
    ne, m, d = y.shape
    b, _, t = slot.shape
    cap = m // b
    grp = MOE_EXPERTS_PER_STEP
    steps = ne // grp
    return pl.pallas_call(
        functools.partial(_moe_scatter_kernel, cap=cap),
        grid=(b, steps),
        in_specs=[pl.BlockSpec((1, ne, t), lambda i, e: (i, 0, 0)),
                  pl.BlockSpec((1, ne, t), lambda i, e: (i, 0, 0)),
                  pl.BlockSpec((grp, cap, d), lambda i, e: (e, i, 0)),
                  pl.BlockSpec((1, t // steps, d), lambda i, e: (i, e, 0)),
                  pl.BlockSpec((1, SUBLANES, d), lambda i, e: (i, 0, 0)),
                  pl.BlockSpec((SUBLANES, d), lambda i, e: (0, 0))],
        out_specs=pl.BlockSpec((1, t, d), lambda i, e: (i, 0, 0)),
        out_shape=jax.ShapeDtypeStruct((b, t, d), F32),
        scratch_shapes=[pltpu.VMEM((t, d), F32)],
        compiler_params=_cparams(2, 56),
        name="moe_scatter",
    )(slot, aff_t, y, x, mods, g)


def _moe(h2, aff_t, w_gate, w_up, w_down, layer, x, mods, g):
    xin, slot = _moe_gather(h2, aff_t)
    y = _moe_ffn(xin, w_gate, w_up, w_down, layer)
    return _moe_scatter(slot, aff_t, y, x, mods, g)


GELU_C1 = 0.7978845608028654
GELU_C2 = GELU_C1 * 0.044715


def _mlp_in_kernel(x_ref, mod_ref, g_ref, w_ref, vg_ref, u_ref, v_ref):
    h = _rms(x_ref[0], g_ref[0:1, :]) * (1.0 + mod_ref[0, 1:2, :]) + mod_ref[0, 0:1, :]
    r = _dot(h.astype(BF16), w_ref[...])
    hr = 0.5 * r
    ge = hr + hr * jnp.tanh(r * (GELU_C1 + GELU_C2 * (r * r)))
    half = ge.shape[1] // 2
    u_ref[0] = ge[:, :half].astype(BF16)
    v = ge[:, half:]
    vc = v - jnp.mean(v, axis=-1, keepdims=True)
    vn = vc * lax.rsqrt(jnp.mean(vc * vc, axis=-1, keepdims=True) + EPS) * vg_ref[...]
    v_ref[0] = vn.astype(BF16)


def _mlp_in(x, mods, g, w, vg):
    b, l, d = x.shape
    tm = TOKEN_TILE
    half = w.shape[1] // 2
    tok = lambda i, j: (i, j, 0)
    return pl.pallas_call(
        _mlp_in_kernel,
        grid=(b, l // tm),
        in_specs=[
            pl.BlockSpec((1, tm, d), tok),
            pl.BlockSpec((1, SUBLANES, d), lambda i, j: (i, 0, 0)),
            pl.BlockSpec((SUBLANES, d), lambda i, j: (0, 0)),
            pl.BlockSpec(w.shape, lambda i, j: (0, 0), pipeline_mode=pl.Buffered(1)),
            pl.BlockSpec((1, half), lambda i, j: (0, 0)),
        ],
        out_specs=[pl.BlockSpec((1, tm, half), tok),
                   pl.BlockSpec((1, tm, half), tok)],
        out_shape=[jax.ShapeDtypeStruct((b, l, half), BF16),
                   jax.ShapeDtypeStruct((b, l, half), BF16)],
        compiler_params=_cparams(2, 48),
        name="mlp_in_proj",
    )(x, mods, g, w, vg)


MLP_GROUPS_PER_STEP = 4
MLP_COL_PITCH = GRID_W + SUBLANES


def _mlp_mix_kernel(u_ref, v_ref, ws_ref, bs_ref, w_ref, x_ref, mod_ref, g_ref, rw_ref,
                    xo_ref, h2_ref, aff_ref, acc_ref, vf_ref, sf_ref, us_ref, *, rows, mix_steps):
    gp = pl.program_id(1)
    t = u_ref.shape[1]
    n_chunks = t // MLP_CHUNK
    per_step = ws_ref.shape[0]
    row_steps = MLP_ROW_GROUPS // per_step
    cols_per_chunk = MLP_CHUNK // rows
    pitch = MLP_COL_PITCH

    @pl.when(gp < row_steps)
    def _():
        for i in range(per_step):
            cs = slice(i * LANES, (i + 1) * LANES)
            for c in range(n_chunks):
                rs = slice(c * MLP_CHUNK, (c + 1) * MLP_CHUNK)
                s = _dot(ws_ref[i], v_ref[0, rs, cs]) + bs_ref[i]
                us_ref[rs, cs] = (u_ref[0, rs, cs].astype(F32) * s).astype(BF16)
        acc_ref[...] = _dot(us_ref[...], w_ref[...]) + jnp.where(gp == 0, 0.0, acc_ref[...])

    @pl.when(jnp.logical_and(gp >= row_steps, gp < mix_steps))
    def _():
        for i in range(per_step):
            cs = slice(i * LANES, (i + 1) * LANES)
            for r in range(rows):
                vf_ref[r * pitch:r * pitch + GRID_W, :] = v_ref[0, r * GRID_W:(r + 1) * GRID_W, cs].astype(F32)
            for k in range(n_chunks):
                xk = jnp.concatenate(
                    [vf_ref[pl.ds(k * cols_per_chunk + wl, rows, stride=pitch), :] for wl in range(cols_per_chunk)],
                    axis=0)
                s = _dot(ws_ref[i], xk.astype(BF16)) + bs_ref[i]
                for wl in range(cols_per_chunk):
                    sf_ref[pl.ds(k * cols_per_chunk + wl, rows, stride=pitch), :] = s[wl * rows:(wl + 1) * rows, :]
            for r in range(rows):
                rr = slice(r * GRID_W, (r + 1) * GRID_W)
                us_ref[rr, cs] = (u_ref[0, rr, cs].astype(F32) * sf_ref[r * pitch:r * pitch + GRID_W, :]).astype(BF16)
        acc_ref[...] += _dot(us_ref[...], w_ref[...])

    @pl.when(gp >= mix_steps)
    def _():
        sub = TOKEN_TILE
        blk = x_ref.shape[1]
        for r in range(blk // sub):
            row0 = pl.multiple_of((gp - mix_steps) * blk + r * sub, sub)
            rs = slice(r * sub, (r + 1) * sub)
            xn, hb, aff = _mixer_epilogue(acc_ref[pl.ds(row0, sub), :], x_ref[0, rs, :], mod_ref, g_ref, rw_ref)
            xo_ref[0, rs, :] = xn
            h2_ref[0, rs, :] = hb
            aff_ref[0, :, rs] = aff


def _mlp_mix(u, v, ws, bsb, w, x, mods, g, rw2):
    b, l, inner = u.shape
    d = w.shape[1]
    per_step = MLP_GROUPS_PER_STEP
    mix_steps = MLP_GROUPS // per_step
    kc = per_step * LANES
    rows = l // GRID_W
    tm = 2 * TOKEN_TILE
    epi_steps = l // tm
    last = mix_steps - 1
    mix = lambda s: jnp.minimum(s, last)
    epi = lambda s: jnp.maximum(s - mix_steps, 0)
    return pl.pallas_call(
        functools.partial(_mlp_mix_kernel, rows=rows, mix_steps=mix_steps),
        grid=(b, mix_steps + epi_steps),
        in_specs=[
            pl.BlockSpec((1, l, kc), lambda i, s: (i, 0, mix(s))),
            pl.BlockSpec((1, l, kc), lambda i, s: (i, 0, mix(s))),
            pl.BlockSpec((per_step, MLP_CHUNK, MLP_CHUNK), lambda i, s: (mix(s), 0, 0)),
            pl.BlockSpec((per_step, MLP_CHUNK, LANES), lambda i, s: (mix(s), 0, 0)),
            pl.BlockSpec((kc, d), lambda i, s: (mix(s), 0)),
            pl.BlockSpec((1, tm, d), lambda i, s: (i, epi(s), 0)),
            pl.BlockSpec((1, SUBLANES, d), lambda i, s: (i, 0, 0)),
            pl.BlockSpec((SUBLANES, d), lambda i, s: (0, 0)),
            pl.BlockSpec(rw2.shape, lambda i, s: (0, 0)),
        ],
        out_specs=[pl.BlockSpec((1, tm, d), lambda i, s: (i, epi(s), 0)),
                   pl.BlockSpec((1, tm, d), lambda i, s: (i, epi(s), 0)),
                   pl.BlockSpec((1, N_EXPERTS, tm), lambda i, s: (i, 0, epi(s)))],
        out_shape=[jax.ShapeDtypeStruct((b, l, d), F32),
                   jax.ShapeDtypeStruct((b, l, d), BF16),
                   jax.ShapeDtypeStruct((b, N_EXPERTS, l), F32)],
        scratch_shapes=[pltpu.VMEM((l, d), F32),
                        pltpu.VMEM((rows * MLP_COL_PITCH, LANES), F32),
                        pltpu.VMEM((rows * MLP_COL_PITCH, LANES), F32),
                        pltpu.VMEM((l, kc), BF16)],
        compiler_params=_cparams(2, 48),
        name="mlp_mix",
    )(u, v, ws, bsb, w, x, mods, g, rw2)


def _pad_rows(a, rows):
    return jnp.pad(a, ((0, rows - a.shape[0]),) + ((0, 0),) * (a.ndim - 1))


def kernel(x, c, ctx, c_ctx, mod_w, mod_b, norm_g, ssd_in_w, ssd_conv_w, ssd_conv_b, ssd_dt_bias, ssd_a_log,
           ssd_d, ssd_norm_g, ssd_out_w, mlp_in_w, mlp_v_g, mlp_ws, mlp_bs, mlp_out_w, router_w, exp_w_gate,
           exp_w_up, exp_w_down):
    b, l, d = x.shape
    inner = SSD_HEADS * SSD_HEADDIM
    n_chunks = l // SSD_CHUNK
    n_ctx_chunks = ctx.shape[1] // SSD_CHUNK

    crows = _pad_rows(jnp.concatenate([c, c_ctx[None, :]], axis=0), -(-(b + 1) // SUBLANES) * SUBLANES)
    mod = _modulation(crows, mod_w, mod_b)
    mods_lat = [jnp.pad(mod[i, :b].reshape(b, N_MOD, d), ((0, 0), (0, SUBLANES - N_MOD), (0, 0)))
                for i in range(2)]
    mods_ctx = jnp.pad(mod[0, b].reshape(1, N_MOD, d), ((0, 0), (0, SUBLANES - N_MOD), (0, 0)))
    mods01 = jnp.stack([jnp.broadcast_to(mods_ctx, (b, SUBLANES, d)), mods_lat[0]], axis=1)
    gains = [_pad_rows(norm_g[i], SUBLANES) for i in range(2)]
    rw2 = []
    for i in range(2):
        rw_t = router_w[i].T
        rw_hi = rw_t.astype(BF16)
        rw2.append(jnp.concatenate([rw_hi, (rw_t - rw_hi.astype(F32)).astype(BF16)], axis=0))

    in_w = ssd_in_w[0]
    conv_dim = ssd_conv_w.shape[2]
    w_in = jnp.pad(in_w, ((0, 0), (0, LANES - 2 * SSD_HEADS))).astype(BF16)
    dt_bias = jnp.pad(ssd_dt_bias[0].reshape(1, 2 * SSD_HEADS), ((0, 0), (0, LANES - 2 * SSD_HEADS)))
    z, xs_all, bc_all, dt_all = _ssd_in_proj(
        ctx, x, mods01, gains[0], w_in, _pad_rows(ssd_conv_w[0], SUBLANES), ssd_conv_b[0].reshape(1, conv_dim),
        dt_bias)

    alog = _pad_rows(jnp.stack([jnp.pad(ssd_a_log[0, 0], (0, LANES - SSD_HEADS)),
                                jnp.pad(ssd_a_log[0, 1], (SSD_HEADS, LANES - 2 * SSD_HEADS))]), SUBLANES)
    head_of_col = jnp.arange(inner, dtype=jnp.int32) // SSD_HEADDIM
    rows128 = jnp.arange(LANES, dtype=jnp.int32)[:, None]
    expand = jnp.stack([(rows128 == head_of_col[None, :] + SSD_HEADS * dd) for dd in range(2)]).astype(BF16)
    dskip = jnp.repeat(ssd_d[0], SSD_HEADDIM).reshape(1, inner)
    y = _ssd_scan(xs_all, bc_all, dt_all, alog, expand, dskip, n_ctx_chunks, n_chunks)

    xa, h2, aff_t = _ssd_out(y, z, x, mods_lat[0], gains[0], ssd_norm_g[0].reshape(1, inner),
                             ssd_out_w[0].astype(BF16), rw2[0])
    x1 = _moe(h2, aff_t, exp_w_gate, exp_w_up, exp_w_down, 0, xa, mods_lat[0], gains[0])

    u, v = _mlp_in(x1, mods_lat[1], gains[1], mlp_in_w[0].astype(BF16), mlp_v_g[0].reshape(1, -1))
    bsb = jnp.broadcast_to(mlp_bs[0][:, :, None], (MLP_GROUPS, MLP_CHUNK, LANES))
    xb, h2b, aff_tb = _mlp_mix(u, v, mlp_ws[0].astype(BF16), bsb, mlp_out_w[0].astype(BF16), x1, mods_lat[1],
                               gains[1], rw2[1])
    return _moe(h2b, aff_tb, exp_w_gate, exp_w_up, exp_w_down, 1, xb, mods_lat[1], gains[1])
```

```python
import functools

import jax
import jax.numpy as jnp
from jax import lax
from jax.experimental import pallas as pl
from jax.experimental.pallas import tpu as pltpu

F32 = jnp.float32
BF16 = jnp.bfloat16
HIGHEST = lax.Precision.HIGHEST
EPS = 1e-6

LANES = 128
SUBLANES = 8
MIB = 1024 * 1024

N_MOD = 6
GRID_W = 64
SSD_HEADDIM = 64
SSD_HEADS = 32
SSD_GROUPS = 4
SSD_HPG = SSD_HEADS // SSD_GROUPS
SSD_STATE = 128
SSD_CONV = 5
SSD_CHUNK = 128
MLP_CHUNK = 128
MLP_GROUPS = 16
MLP_ROW_GROUPS = 8
N_EXPERTS = 16
CAPACITY_FACTOR = 2

CONV_HALO = SUBLANES
TOKEN_TILE = 256
CONV_COL_BLOCK = 1024


def _cparams(n_axes, vmem_mib):
    return pltpu.CompilerParams(dimension_semantics=("arbitrary",) * n_axes,
                                vmem_limit_bytes=vmem_mib * MIB)


def _silu(x):
    h = 0.5 * x
    return h + h * jnp.tanh(h)


def _rms(x, g):
    return x * lax.rsqrt(jnp.mean(x * x, axis=-1, keepdims=True) + EPS) * g


def _dot(a, b):
    return jnp.dot(a, b, preferred_element_type=F32)


def _dot_nt(a, b, precision=None):
    return lax.dot_general(a, b, (((1,), (1,)), ((), ())), preferred_element_type=F32, precision=precision)


def _dot_tn(a, b):
    return lax.dot_general(a, b, (((0,), (0,)), ((), ())), preferred_element_type=F32)


def _mod_kernel(c_ref, w_ref, b_ref, o_ref):
    c = c_ref[...]
    s = _silu(c)
    o_ref[0] = jnp.dot(s, w_ref[0], preferred_element_type=F32, precision=HIGHEST) + b_ref[0]


def _modulation(crows, mod_w, mod_b):
    depth, d, n = mod_w.shape
    rows = crows.shape[0]
    tn = 1536
    return pl.pallas_call(
        _mod_kernel,
        grid=(depth, n // tn),
        in_specs=[pl.BlockSpec((rows, d), lambda i, j: (0, 0)),
                  pl.BlockSpec((1, d, tn), lambda i, j: (i, 0, j)),
                  pl.BlockSpec((1, 1, tn), lambda i, j: (i, 0, j))],
        out_specs=pl.BlockSpec((1, rows, tn), lambda i, j: (i, 0, j)),
        out_shape=jax.ShapeDtypeStruct((depth, rows, n), F32),
        compiler_params=_cparams(2, 40),
        name="modulation",
    )(crows, mod_w, mod_b.reshape(depth, 1, n))


def _ssd_in_kernel(ctx_ref, x_ref, xp_ref, xn_ref, mod_ref, g_ref, w_ref, cw_ref, cb_ref, dtb_ref, sh_ref,
                   z_ref, xs_ref, bc_ref, dt_ref, *, n_tiles, inner, conv_dim):
    j = pl.program_id(1)
    tm = x_ref.shape[1]
    halo = xp_ref.shape[1]
    xc = jnp.where(j == 0, ctx_ref[0], x_ref[0])
    xa = jnp.concatenate([xp_ref[0], xc, xn_ref[0]], axis=0)
    h = (_rms(xa, g_ref[0:1, :]) * (1.0 + mod_ref[0, 0, 1:2, :]) + mod_ref[0, 0, 0:1, :]).astype(BF16)
    hc = h[halo:halo + tm, :]
    z_ref[0] = _dot(hc, w_ref[:, :inner]).astype(BF16)
    dtr = _dot(hc, w_ref[:, inner + conv_dim:]) + dtb_ref[...]
    dt_ref[0] = jnp.maximum(dtr, 0.0) + jnp.log1p(jnp.exp(-jnp.abs(dtr)))
    keep_top = jnp.where(j <= 1, 0.0, 1.0)
    keep_bot = jnp.where(jnp.logical_or(j == 0, j == n_tiles - 1), 0.0, 1.0)
    hrow = lax.broadcasted_iota(jnp.int32, (halo, 1), 0)
    pad = (SSD_CONV - 1) // 2
    taps = [k for k in range(SSD_CONV) if k != pad]
    cblk = CONV_COL_BLOCK
    for cbi in range(conv_dim // cblk):
        c0 = cbi * cblk
        u = _dot(h, w_ref[:, inner + c0:inner + c0 + cblk])
        uc = u[halo:halo + tm, :]
        top = u[:halo, :] * keep_top
        bot = u[halo + tm:, :] * keep_bot
        shifted = _dot(sh_ref[...], uc.astype(BF16))
        acc = cb_ref[:, c0:c0 + cblk] + cw_ref[pad:pad + 1, c0:c0 + cblk] * uc
        corr_top = jnp.zeros((halo, cblk), F32)
        corr_bot = jnp.zeros((halo, cblk), F32)
        for jj, k in enumerate(taps):
            wk = cw_ref[k:k + 1, c0:c0 + cblk]
            acc = acc + wk * shifted[jj * tm:(jj + 1) * tm, :]
            d = k - pad
            if d < 0:
                corr_top = corr_top + wk * jnp.where(hrow < -d, pltpu.roll(top, -d, axis=0), 0.0)
            else:
                corr_bot = corr_bot + wk * jnp.where(hrow >= halo - d, pltpu.roll(bot, halo - d, axis=0), 0.0)
        acc = jnp.concatenate([acc[:halo, :] + corr_top, acc[halo:tm - halo, :], acc[tm - halo:, :] + corr_bot],
                              axis=0)
        act = _silu(acc).astype(BF16)
        if c0 < inner:
            xs_ref[0, :, c0:c0 + cblk] = act
        else:
            bc_ref[0, :, c0 - inner:c0 - inner + cblk] = act


def _ssd_in_proj(ctx, x, mods01, g, w, conv_w, conv_b, dt_bias):
    b, l, d = x.shape
    lc = ctx.shape[1]
    tm = TOKEN_TILE
    assert lc == tm and l % tm == 0
    n_tiles = 1 + l // tm
    inner = SSD_HEADS * SSD_HEADDIM
    conv_dim = conv_w.shape[1]
    bc_dim = conv_dim - inner
    halo = CONV_HALO
    blocks_per_tile = tm // halo
    last_halo_block = l // halo - 1
    lt = lc + l
    pad = (SSD_CONV - 1) // 2
    t_idx = jnp.arange(tm, dtype=jnp.int32)
    shifts = jnp.concatenate([t_idx[None, :] == t_idx[:, None] + (k - pad) for k in range(SSD_CONV) if k != pad],
                             axis=0).astype(BF16)
    kern = functools.partial(_ssd_in_kernel, n_tiles=n_tiles, inner=inner, conv_dim=conv_dim)
    return pl.pallas_call(
        kern,
        grid=(b, n_tiles),
        in_specs=[
            pl.BlockSpec((1, lc, d), lambda i, j: (i, 0, 0)),
            pl.BlockSpec((1, tm, d), lambda i, j: (i, jnp.maximum(j - 1, 0), 0)),
            pl.BlockSpec((1, halo, d), lambda i, j: (i, jnp.maximum((j - 1) * blocks_per_tile - 1, 0), 0)),
            pl.BlockSpec((1, halo, d), lambda i, j: (i, jnp.minimum(j * blocks_per_tile, last_halo_block), 0)),
            pl.BlockSpec((1, 1, SUBLANES, d), lambda i, j: (i, jnp.minimum(j, 1), 0, 0)),
            pl.BlockSpec((SUBLANES, d), lambda i, j: (0, 0)),
            pl.BlockSpec(w.shape, lambda i, j: (0, 0), pipeline_mode=pl.Buffered(1)),
            pl.BlockSpec((SUBLANES, conv_dim), lambda i, j: (0, 0)),
            pl.BlockSpec((1, conv_dim), lambda i, j: (0, 0)),
            pl.BlockSpec((1, LANES), lambda i, j: (0, 0)),
            pl.BlockSpec(shifts.shape, lambda i, j: (0, 0)),
        ],
        out_specs=[
            pl.BlockSpec((1, tm, inner), lambda i, j: (i, jnp.maximum(j - 1, 0), 0)),
            pl.BlockSpec((1, tm, inner), lambda i, j: (i, j, 0)),
            pl.BlockSpec((1, tm, bc_dim), lambda i, j: (i, j, 0)),
            pl.BlockSpec((1, tm, LANES), lambda i, j: (i, j, 0)),
        ],
        out_shape=[
            jax.ShapeDtypeStruct((b, l, inner), BF16),
            jax.ShapeDtypeStruct((b, lt, inner), BF16),
            jax.ShapeDtypeStruct((b, lt, bc_dim), BF16),
            jax.ShapeDtypeStruct((b, lt, LANES), F32),
        ],
        compiler_params=_cparams(2, 56),
        name="ssd_in_proj",
    )(ctx, x, x, x, mods01, g, w, conv_w, conv_b, dt_bias, shifts)


SSD_CHUNKS_PER_STEP = 2
SSD_DECAY_ROWS = 16


def _split3(x):
    hi = x.astype(BF16)
    r1 = x - hi.astype(F32)
    mid = r1.astype(BF16)
    lo = (r1 - mid.astype(F32)).astype(BF16)
    return hi, mid, lo


def _ssd_step(rev, lat, lblk, xs_ref, bc_ref, dt_ref, alog_ref, e_ref, dskip_ref, y_ref, st_ref, ybuf_ref):
    c = SSD_CHUNK
    nck = SSD_CHUNKS_PER_STEP
    dcol = SSD_HEADS * int(rev)
    gw = SSD_HPG * SSD_HEADDIM
    bc_off = SSD_GROUPS * SSD_STATE
    a_neg = -jnp.exp(alog_ref[int(rev):int(rev) + 1, :])
    li = lax.broadcasted_iota(jnp.int32, (c, c), 0)
    si = lax.broadcasted_iota(jnp.int32, (c, c), 1)
    causal = (si >= li) if rev else (si <= li)
    tri = jnp.where(causal, 1.0, 0.0).astype(BF16)
    e = e_ref[int(rev)]
    head = lax.broadcasted_iota(jnp.int32, (c, LANES), 1)

    dts = [dt_ref[0, k * c:(k + 1) * c, :] for k in range(nck)]
    pieces = []
    for k in range(nck):
        pieces.extend(_split3(dts[k] * a_neg))
    run = _dot(tri, jnp.concatenate(pieces, axis=1))
    css = [run[:, (3 * k) * LANES:(3 * k + 1) * LANES] + run[:, (3 * k + 1) * LANES:(3 * k + 2) * LANES]
           + run[:, (3 * k + 2) * LANES:(3 * k + 3) * LANES] for k in range(nck)]
    tots = [cs[0:1, :] if rev else cs[c - 1:c, :] for cs in css]

    blocks = []
    for k in range(nck):
        blocks.append((jnp.exp(tots[k] - css[k]) * dts[k]).astype(BF16))
        blocks.append(jnp.exp(css[k]).astype(BF16))
    drow = lax.broadcasted_iota(jnp.int32, (SSD_DECAY_ROWS, LANES), 0)
    for k in range(nck):
        hi, mid, lo = _split3(jnp.broadcast_to(jnp.exp(tots[k]), (SSD_DECAY_ROWS, LANES)))
        rows3 = jnp.where(drow == 0, hi.astype(F32),
                          jnp.where(drow == 1, mid.astype(F32), jnp.where(drow == 2, lo.astype(F32), 0.0)))
        blocks.append(rows3.astype(BF16))
    ex = _dot(jnp.concatenate(blocks, axis=0), e).astype(BF16)
    dec0 = 2 * nck * c
    even_cols = jnp.where(head < SSD_HEADDIM, 1.0, 0.0).astype(BF16)
    odd_cols = jnp.where(head < SSD_HEADDIM, 0.0, 1.0).astype(BF16)

    for k in (range(nck - 1, -1, -1) if rev else range(nck)):
        base = 2 * k * c
        xs = xs_ref[0, k * c:(k + 1) * c, :]
        bc = bc_ref[0, k * c:(k + 1) * c, :]
        xw = xs * ex[base:base + c, :]

        def y_part(k=k, base=base, xs=xs, bc=bc):
            lc = lblk * nck + k
            cs = css[k]
            rs_t = (cs - jnp.log(dts[k])).T
            for g in range(SSD_GROUPS):
                bg = bc[:, g * SSD_STATE:(g + 1) * SSD_STATE]
                cg = bc[:, bc_off + g * SSD_STATE:bc_off + (g + 1) * SSD_STATE]
                px = ex[base + c:base + 2 * c, g * gw:(g + 1) * gw].astype(F32)
                y_off = _dot(cg, st_ref[:, g * gw:(g + 1) * gw].astype(BF16)) * px
                cb = _dot_nt(cg, bg)
                for hp in range(SSD_HPG // 2):
                    ms = []
                    for hh in range(2):
                        hd = dcol + g * SSD_HPG + 2 * hp + hh
                        diff = cs[:, hd:hd + 1] - rs_t[hd:hd + 1, :]
                        ms.append(cb * jnp.exp(jnp.where(causal, diff, -jnp.inf)))
                    mp = jnp.concatenate(ms, axis=1).astype(BF16)
                    col0 = (g * SSD_HPG + 2 * hp) * SSD_HEADDIM
                    xp = xs[:, col0:col0 + LANES]
                    rhs = jnp.concatenate([xp * even_cols, xp * odd_cols], axis=0)
                    y_pair = _dot(mp, rhs) + y_off[:, 2 * hp * SSD_HEADDIM:2 * hp * SSD_HEADDIM + LANES]
                    if rev:
                        tot_y = (ybuf_ref[lc, :, col0:col0 + LANES] + y_pair
                                 + dskip_ref[:, col0:col0 + LANES] * xp.astype(F32))
                        y_ref[0, k * c:(k + 1) * c, col0:col0 + LANES] = tot_y.astype(BF16)
                    else:
                        ybuf_ref[lc, :, col0:col0 + LANES] = y_pair

        if lat:
            y_part()

        drows = ex[dec0 + k * SSD_DECAY_ROWS:dec0 + (k + 1) * SSD_DECAY_ROWS, :].astype(F32)
        decay = drows[0:1, :] + drows[1:2, :] + drows[2:3, :]
        for g in range(SSD_GROUPS):
            bg = bc[:, g * SSD_STATE:(g + 1) * SSD_STATE]
            upd = _dot_tn(bg, xw[:, g * gw:(g + 1) * gw])
            st_ref[:, g * gw:(g + 1) * gw] = st_ref[:, g * gw:(g + 1) * gw] * decay[:, g * gw:(g + 1) * gw] + upd


def _ssd_scan_kernel(xs_ref, bc_ref, dt_ref, alog_ref, e_ref, dskip_ref, y_ref, st_ref, ybuf_ref, *,
                     n_ctx_blocks, n_blocks):
    d = pl.program_id(1)
    s = pl.program_id(2)

    @pl.when(s == 0)
    def _():
        st_ref[...] = jnp.zeros_like(st_ref)

    for rev in (False, True):
        if rev:
            blk = jnp.where(s < n_ctx_blocks, n_ctx_blocks - 1 - s, n_blocks + 2 * n_ctx_blocks - 1 - s)
        else:
            blk = s
        for lat in (False, True):
            in_part = (blk >= n_ctx_blocks) if lat else (blk < n_ctx_blocks)

            @pl.when(jnp.logical_and(d == int(rev), in_part))
            def _(rev=rev, lat=lat, blk=blk):
                _ssd_step(rev, lat, jnp.maximum(blk - n_ctx_blocks, 0), xs_ref, bc_ref, dt_ref,
                          alog_ref, e_ref, dskip_ref, y_ref, st_ref, ybuf_ref)


def _ssd_scan(xs_all, bc_all, dt_all, alog, expand, dskip, n_ctx_chunks, n_chunks):
    b, lt, inner = xs_all.shape
    nck = SSD_CHUNKS_PER_STEP
    rows = nck * SSD_CHUNK
    assert n_ctx_chunks % nck == 0 and n_chunks % nck == 0
    n_ctx_blocks = n_ctx_chunks // nck
    n_blocks = n_chunks // nck
    steps = n_ctx_blocks + n_blocks

    def block_idx(d, s):
        bwd = jnp.where(s < n_ctx_blocks, n_ctx_blocks - 1 - s, steps + n_ctx_blocks - 1 - s)
        return jnp.where(d == 0, s, bwd)

    def out_idx(d, s):
        return jnp.where(jnp.logical_and(d == 1, s >= n_ctx_blocks), steps - 1 - s, n_blocks - 1)

    kern = functools.partial(_ssd_scan_kernel, n_ctx_blocks=n_ctx_blocks, n_blocks=n_blocks)
    return pl.pallas_call(
        kern,
        grid=(b, 2, steps),
        in_specs=[
            pl.BlockSpec((1, rows, inner), lambda i, d, s: (i, block_idx(d, s), 0)),
            pl.BlockSpec((1, rows, bc_all.shape[2]), lambda i, d, s: (i, block_idx(d, s), 0)),
            pl.BlockSpec((1, rows, LANES), lambda i, d, s: (i, block_idx(d, s), 0)),
            pl.BlockSpec((SUBLANES, LANES), lambda i, d, s: (0, 0)),
            pl.BlockSpec((2, LANES, inner), lambda i, d, s: (0, 0, 0)),
            pl.BlockSpec((1, inner), lambda i, d, s: (0, 0)),
        ],
        out_specs=pl.BlockSpec((1, rows, inner), lambda i, d, s: (i, out_idx(d, s), 0)),
        out_shape=jax.ShapeDtypeStruct((b, n_chunks * SSD_CHUNK, inner), BF16),
        scratch_shapes=[pltpu.VMEM((SSD_STATE, inner), F32),
                        pltpu.VMEM((n_chunks, SSD_CHUNK, inner), F32)],
        compiler_params=_cparams(3, 56),
        name="ssd_scan",
    )(xs_all, bc_all, dt_all, alog, expand, dskip)


def _mixer_epilogue(o, x, mod_ref, g_ref, rw_ref):
    xn = x + mod_ref[0, 2:3, :] * _rms(o, g_ref[1:2, :])
    h2 = _rms(xn, g_ref[2:3, :]) * (1.0 + mod_ref[0, 4:5, :]) + mod_ref[0, 3:4, :]
    h_hi = h2.astype(BF16)
    h_lo = (h2 - h_hi.astype(F32)).astype(BF16)
    ne = rw_ref.shape[0] // 2
    both = _dot_nt(rw_ref[...], h_hi)
    lg = both[:ne, :] + both[ne:, :] + _dot_nt(rw_ref[0:ne, :], h_lo)
    ex = jnp.exp(lg - jnp.max(lg, axis=0, keepdims=True))
    return xn, h_hi, ex / jnp.sum(ex, axis=0, keepdims=True)


def _ssd_out_kernel(y_ref, z_ref, x_ref, mod_ref, g_ref, ng_ref, w_ref, rw_ref, xo_ref, h2_ref, aff_ref):
    sub = TOKEN_TILE
    gdim = y_ref.shape[2] // SSD_GROUPS
    for r in range(y_ref.shape[1] // sub):
        rs = slice(r * sub, (r + 1) * sub)
        z = z_ref[0, rs, :].astype(F32)
        yg = y_ref[0, rs, :].astype(F32) * _silu(z)
        parts = [_rms(yg[:, g * gdim:(g + 1) * gdim], ng_ref[:, g * gdim:(g + 1) * gdim]).astype(BF16)
                 for g in range(SSD_GROUPS)]
        o = _dot(jnp.concatenate(parts, axis=1), w_ref[...])
        xn, hb, aff = _mixer_epilogue(o, x_ref[0, rs, :], mod_ref, g_ref, rw_ref)
        xo_ref[0, rs, :] = xn
        h2_ref[0, rs, :] = hb
        aff_ref[0, :, rs] = aff


def _epilogue_outs(b, l, d, tm):
    out_specs = [pl.BlockSpec((1, tm, d), lambda i, j: (i, j, 0)),
                 pl.BlockSpec((1, tm, d), lambda i, j: (i, j, 0)),
                 pl.BlockSpec((1, N_EXPERTS, tm), lambda i, j: (i, 0, j))]
    out_shape = [jax.ShapeDtypeStruct((b, l, d), F32),
                 jax.ShapeDtypeStruct((b, l, d), BF16),
                 jax.ShapeDtypeStruct((b, N_EXPERTS, l), F32)]
    return out_specs, out_shape


def _ssd_out(y, z, x, mods, g, ng, w, rw2):
    b, l, d = x.shape
    inner = y.shape[2]
    tm = 2 * TOKEN_TILE
    out_specs, out_shape = _epilogue_outs(b, l, d, tm)
    return pl.pallas_call(
        _ssd_out_kernel,
        grid=(b, l // tm),
        in_specs=[
            pl.BlockSpec((1, tm, inner), lambda i, j: (i, j, 0)),
            pl.BlockSpec((1, tm, inner), lambda i, j: (i, j, 0)),
            pl.BlockSpec((1, tm, d), lambda i, j: (i, j, 0)),
            pl.BlockSpec((1, SUBLANES, d), lambda i, j: (i, 0, 0)),
            pl.BlockSpec((SUBLANES, d), lambda i, j: (0, 0)),
            pl.BlockSpec((1, inner), lambda i, j: (0, 0)),
            pl.BlockSpec(w.shape, lambda i, j: (0, 0)),
            pl.BlockSpec(rw2.shape, lambda i, j: (0, 0)),
        ],
        out_specs=out_specs,
        out_shape=out_shape,
        compiler_params=_cparams(2, 48),
        name="ssd_out",
    )(y, z, x, mods, g, ng, w, rw2)


def _lane_prefix_exclusive(m01, upper):
    e, t = m01.shape
    carry = jnp.zeros((e, 1), F32)
    outs = []
    for k in range(t // LANES):
        tile = m01[:, k * LANES:(k + 1) * LANES]
        incl = _dot(tile.astype(BF16), upper)
        outs.append(incl - tile + carry)
        carry = carry + incl[:, LANES - 1:LANES]
    return jnp.concatenate(outs, axis=1)


def _route(aff, cap):
    e, t = aff.shape
    key = pltpu.bitcast(aff, jnp.int32)

    def enough(cand):
        return jnp.sum(jnp.where(key >= cand, 1.0, 0.0), axis=1, keepdims=True) >= cap

    def body(i, thr):
        sh = 29 - 2 * i
        c1, c2, c3 = (jnp.bitwise_or(thr, jnp.left_shift(jnp.int32(v), sh)) for v in (1, 2, 3))
        return jnp.where(enough(c3), c3, jnp.where(enough(c2), c2, jnp.where(enough(c1), c1, thr)))

    thr = lax.fori_loop(0, 15, body, jnp.zeros((e, 1), jnp.int32))
    last = jnp.bitwise_or(thr, 1)
    thr = jnp.where(enough(last), last, thr)
    gt = jnp.where(key > thr, 1.0, 0.0)
    eq = jnp.where(key == thr, 1.0, 0.0)
    need = cap - jnp.sum(gt, axis=1, keepdims=True)
    r = lax.broadcasted_iota(jnp.int32, (LANES, LANES), 0)
    cidx = lax.broadcasted_iota(jnp.int32, (LANES, LANES), 1)
    upper = jnp.where(r <= cidx, 1.0, 0.0).astype(BF16)
    eq_rank = _lane_prefix_exclusive(eq, upper)
    sel = gt + eq * jnp.where(eq_rank < need, 1.0, 0.0)
    pos = _lane_prefix_exclusive(sel, upper)
    return jnp.where(sel > 0.5, pos, -1.0).astype(jnp.int32)


MOE_EXPERTS_PER_STEP = 4
MOE_FFN_TILE = 1024


def _pick_rows(slot_ref, e, cap, t):
    srow = slot_ref[0, pl.ds(e, 1), :]
    return lax.broadcasted_iota(jnp.int32, (cap, t), 0) == srow


def _moe_gather_kernel(h_ref, aff_ref, xin_ref, slot_ref, *, cap):
    eg = pl.program_id(1)

    @pl.when(eg == 0)
    def _():
        slot_ref[0] = _route(aff_ref[0], cap)

    t = h_ref.shape[1]
    ps = [jnp.where(_pick_rows(slot_ref, eg * MOE_EXPERTS_PER_STEP + i, cap, t), 1.0, 0.0).astype(BF16)
          for i in range(MOE_EXPERTS_PER_STEP)]
    xin = _dot(jnp.concatenate(ps, axis=0), h_ref[0])
    for i in range(MOE_EXPERTS_PER_STEP):
        xin_ref[i] = xin[i * cap:(i + 1) * cap, :].astype(BF16)


def _moe_gather(h2, aff_t):
    b, t, d = h2.shape
    ne = aff_t.shape[1]
    cap = (CAPACITY_FACTOR * t) // ne
    g = MOE_EXPERTS_PER_STEP
    return pl.pallas_call(
        functools.partial(_moe_gather_kernel, cap=cap),
        grid=(b, ne // g),
        in_specs=[pl.BlockSpec((1, t, d), lambda i, e: (i, 0, 0)),
                  pl.BlockSpec((1, ne, t), lambda i, e: (i, 0, 0))],
        out_specs=[pl.BlockSpec((g, cap, d), lambda i, e: (e, i, 0)),
                   pl.BlockSpec((1, ne, t), lambda i, e: (i, 0, 0))],
        out_shape=[jax.ShapeDtypeStruct((ne, b * cap, d), BF16),
                   jax.ShapeDtypeStruct((b, ne, t), jnp.int32)],
        compiler_params=_cparams(2, 48),
        name="moe_gather",
    )(h2, aff_t)


def _moe_ffn_kernel(x_ref, wg_ref, wu_ref, wd_ref, y_ref, acc_ref):
    half = pl.program_id(2)

    def half_ffn():
        x = x_ref[0]
        hid = _dot(x, wg_ref[0, 0].astype(BF16))
        hid = _silu(hid) * _dot(x, wu_ref[0, 0].astype(BF16))
        return _dot(hid.astype(BF16), wd_ref[0, 0].astype(BF16))

    @pl.when(half == 0)
    def _():
        acc_ref[...] = half_ffn()

    @pl.when(half == 1)
    def _():
        y_ref[0] = (acc_ref[...] + half_ffn()).astype(BF16)


def _moe_ffn(xin, w_gate, w_up, w_down, layer):
    ne, m, d = xin.shape
    ff = w_gate.shape[3]
    tm = MOE_FFN_TILE
    hf = ff // 2
    return pl.pallas_call(
        _moe_ffn_kernel,
        grid=(ne, m // tm, 2),
        in_specs=[pl.BlockSpec((1, tm, d), lambda e, j, h: (e, j, 0)),
                  pl.BlockSpec((1, 1, d, hf), lambda e, j, h: (layer, e, 0, h)),
                  pl.BlockSpec((1, 1, d, hf), lambda e, j, h: (layer, e, 0, h)),
                  pl.BlockSpec((1, 1, hf, d), lambda e, j, h: (layer, e, h, 0))],
        out_specs=pl.BlockSpec((1, tm, d), lambda e, j, h: (e, j, 0)),
        out_shape=jax.ShapeDtypeStruct((ne, m, d), BF16),
        scratch_shapes=[pltpu.VMEM((tm, d), F32)],
        compiler_params=_cparams(3, 56),
        name="moe_ffn",
    )(xin, w_gate, w_up, w_down)


def _moe_scatter_kernel(slot_ref, aff_ref, y_ref, x_ref, mod_ref, g_ref, o_ref, xst_ref, *, cap):
    eg = pl.program_id(1)
    t = o_ref.shape[1]
    q = x_ref.shape[1]
    xst_ref[pl.ds(pl.multiple_of(eg * q, q), q), :] = x_ref[0]

    def contribution():
        ps = []
        for i in range(MOE_EXPERTS_PER_STEP):
            e = eg * MOE_EXPERTS_PER_STEP + i
            ps.append(jnp.where(_pick_rows(slot_ref, e, cap, t), aff_ref[0, pl.ds(e, 1), :], 0.0).astype(BF16))
        p = jnp.concatenate(ps, axis=0)
        y = y_ref[...].reshape(MOE_EXPERTS_PER_STEP * cap, y_ref.shape[2])
        return _dot_tn(p, y)

    @pl.when(eg == 0)
    def _():
        o_ref[0] = contribution()

    @pl.when(eg > 0)
    def _():
        o_ref[0] += contribution()

    @pl.when(eg == pl.num_programs(1) - 1)
    def _():
        sub = TOKEN_TILE
        for r in range(t // sub):
            rs = slice(r * sub, (r + 1) * sub)
            o_ref[0, rs, :] = xst_ref[rs, :] + mod_ref[0, 5:6, :] * _rms(o_ref[0, rs, :], g_ref[3:4, :])


def _moe_scatter(slot, aff_t, y, x, mods, g):
    ne, m, d = y.shape
    b, _, t = slot.shape
    cap = m // b
    grp = MOE_EXPERTS_PER_STEP
    steps = ne // grp
    return pl.pallas_call(
        functools.partial(_moe_scatter_kernel, cap=cap),
        grid=(b, steps),
        in_specs=[pl.BlockSpec((1, ne, t), lambda i, e: (i, 0, 0)),
                  pl.BlockSpec((1, ne, t), lambda i, e: (i, 0, 0)),
                  pl.BlockSpec((grp, cap, d), lambda i, e: (e, i, 0)),
                  pl.BlockSpec((1, t // steps, d), lambda i, e: (i, e, 0)),
                  pl.BlockSpec((1, SUBLANES, d), lambda i, e: (i, 0, 0)),
                  pl.BlockSpec((SUBLANES, d), lambda i, e: (0, 0))],
        out_specs=pl.BlockSpec((1, t, d), lambda i, e: (i, 0, 0)),
        out_shape=jax.ShapeDtypeStruct((b, t, d), F32),
        scratch_shapes=[pltpu.VMEM((t, d), F32)],
        compiler_params=_cparams(2, 56),
        name="moe_scatter",
    )(slot, aff_t, y, x, mods, g)


def _moe(h2, aff_t, w_gate, w_up, w_down, layer, x, mods, g):
    xin, slot = _moe_gather(h2, aff_t)
    y = _moe_ffn(xin, w_gate, w_up, w_down, layer)
    return _moe_scatter(slot, aff_t, y, x, mods, g)


GELU_C1 = 0.7978845608028654
GELU_C2 = GELU_C1 * 0.044715


def _mlp_in_kernel(x_ref, mod_ref, g_ref, w_ref, vg_ref, u_ref, v_ref):
    sub = TOKEN_TILE
    half = w_ref.shape[1] // 2
    for i in range(x_ref.shape[1] // sub):
        rs = slice(i * sub, (i + 1) * sub)
        h = _rms(x_ref[0, rs, :], g_ref[0:1, :]) * (1.0 + mod_ref[0, 1:2, :]) + mod_ref[0, 0:1, :]
        hb = h.astype(BF16)
        r = _dot(hb, w_ref[:, :half])
        hr = 0.5 * r
        u_ref[0, rs, :] = (hr + hr * jnp.tanh(r * (GELU_C1 + GELU_C2 * (r * r)))).astype(BF16)
        r = _dot(hb, w_ref[:, half:])
        hr = 0.5 * r
        v = hr + hr * jnp.tanh(r * (GELU_C1 + GELU_C2 * (r * r)))
        vc = v - jnp.mean(v, axis=-1, keepdims=True)
        vn = vc * lax.rsqrt(jnp.mean(vc * vc, axis=-1, keepdims=True) + EPS) * vg_ref[...]
        v_ref[0, rs, :] = vn.astype(BF16)


def _mlp_in(x, mods, g, w, vg):
    b, l, d = x.shape
    tm = 2 * TOKEN_TILE
    half = w.shape[1] // 2
    tok = lambda i, j: (i, j, 0)
    return pl.pallas_call(
        _mlp_in_kernel,
        grid=(b, l // tm),
        in_specs=[
            pl.BlockSpec((1, tm, d), tok),
            pl.BlockSpec((1, SUBLANES, d), lambda i, j: (i, 0, 0)),
            pl.BlockSpec((SUBLANES, d), lambda i, j: (0, 0)),
            pl.BlockSpec(w.shape, lambda i, j: (0, 0), pipeline_mode=pl.Buffered(1)),
            pl.BlockSpec((1, half), lambda i, j: (0, 0)),
        ],
        out_specs=[pl.BlockSpec((1, tm, half), tok),
                   pl.BlockSpec((1, tm, half), tok)],
        out_shape=[jax.ShapeDtypeStruct((b, l, half), BF16),
                   jax.ShapeDtypeStruct((b, l, half), BF16)],
        compiler_params=_cparams(2, 48),
        name="mlp_in_proj",
    )(x, mods, g, w, vg)


MLP_GROUPS_PER_STEP = 4
MLP_COL_PITCH = GRID_W + SUBLANES


def _mlp_mix_kernel(u_ref, v_ref, ws_ref, bs_ref, w_ref, x_ref, mod_ref, g_ref, rw_ref,
                    xo_ref, h2_ref, aff_ref, acc_ref, vf_ref, sf_ref, us_ref, *, rows, mix_steps):
    gp = pl.program_id(1)
    t = u_ref.shape[1]
    n_chunks = t // MLP_CHUNK
    per_step = ws_ref.shape[0]
    row_steps = MLP_ROW_GROUPS // per_step
    cols_per_chunk = MLP_CHUNK // rows
    pitch = MLP_COL_PITCH

    @pl.when(gp < row_steps)
    def _():
        for i in range(per_step):
            cs = slice(i * LANES, (i + 1) * LANES)
            for c in range(n_chunks):
                rs = slice(c * MLP_CHUNK, (c + 1) * MLP_CHUNK)
                s = _dot(ws_ref[i], v_ref[0, rs, cs]) + bs_ref[i]
                us_ref[rs, cs] = (u_ref[0, rs, cs].astype(F32) * s).astype(BF16)
        acc_ref[...] = _dot(us_ref[...], w_ref[...]) + jnp.where(gp == 0, 0.0, acc_ref[...])

    @pl.when(jnp.logical_and(gp >= row_steps, gp < mix_steps))
    def _():
        for i in range(per_step):
            cs = slice(i * LANES, (i + 1) * LANES)
            for r in range(rows):
                vf_ref[r * pitch:r * pitch + GRID_W, :] = v_ref[0, r * GRID_W:(r + 1) * GRID_W, cs].astype(F32)
            for k in range(n_chunks):
                xk = jnp.concatenate(
                    [vf_ref[pl.ds(k * cols_per_chunk + wl, rows, stride=pitch), :] for wl in range(cols_per_chunk)],
                    axis=0)
                s = _dot(ws_ref[i], xk.astype(BF16)) + bs_ref[i]
                for wl in range(cols_per_chunk):
                    sf_ref[pl.ds(k * cols_per_chunk + wl, rows, stride=pitch), :] = s[wl * rows:(wl + 1) * rows, :]
            for r in range(rows):
                rr = slice(r * GRID_W, (r + 1) * GRID_W)
                us_ref[rr, cs] = (u_ref[0, rr, cs].astype(F32) * sf_ref[r * pitch:r * pitch + GRID_W, :]).astype(BF16)
        acc_ref[...] += _dot(us_ref[...], w_ref[...])

    @pl.when(gp >= mix_steps)
    def _():
        sub = TOKEN_TILE
        blk = x_ref.shape[1]
        for r in range(blk // sub):
            row0 = pl.multiple_of((gp - mix_steps) * blk + r * sub, sub)
            rs = slice(r * sub, (r + 1) * sub)
            xn, hb, aff = _mixer_epilogue(acc_ref[pl.ds(row0, sub), :], x_ref[0, rs, :], mod_ref, g_ref, rw_ref)
            xo_ref[0, rs, :] = xn
            h2_ref[0, rs, :] = hb
            aff_ref[0, :, rs] = aff


def _mlp_mix(u, v, ws, bsb, w, x, mods, g, rw2):
    b, l, inner = u.shape
    d = w.shape[1]
    per_step = MLP_GROUPS_PER_STEP
    mix_steps = MLP_GROUPS // per_step
    kc = per_step * LANES
    rows = l // GRID_W
    tm = 2 * TOKEN_TILE
    epi_steps = l // tm
    last = mix_steps - 1
    mix = lambda s: jnp.minimum(s, last)
    epi = lambda s: jnp.maximum(s - mix_steps, 0)
    return pl.pallas_call(
        functools.partial(_mlp_mix_kernel, rows=rows, mix_steps=mix_steps),
        grid=(b, mix_steps + epi_steps),
        in_specs=[
            pl.BlockSpec((1, l, kc), lambda i, s: (i, 0, mix(s))),
            pl.BlockSpec((1, l, kc), lambda i, s: (i, 0, mix(s))),
            pl.BlockSpec((per_step, MLP_CHUNK, MLP_CHUNK), lambda i, s: (mix(s), 0, 0)),
            pl.BlockSpec((per_step, MLP_CHUNK, LANES), lambda i, s: (mix(s), 0, 0)),
            pl.BlockSpec((kc, d), lambda i, s: (mix(s), 0)),
            pl.BlockSpec((1, tm, d), lambda i, s: (i, epi(s), 0)),
            pl.BlockSpec((1, SUBLANES, d), lambda i, s: (i, 0, 0)),
            pl.BlockSpec((SUBLANES, d), lambda i, s: (0, 0)),
            pl.BlockSpec(rw2.shape, lambda i, s: (0, 0)),
        ],
        out_specs=[pl.BlockSpec((1, tm, d), lambda i, s: (i, epi(s), 0)),
                   pl.BlockSpec((1, tm, d), lambda i, s: (i, epi(s), 0)),
                   pl.BlockSpec((1, N_EXPERTS, tm), lambda i, s: (i, 0, epi(s)))],
        out_shape=[jax.ShapeDtypeStruct((b, l, d), F32),
                   jax.ShapeDtypeStruct((b, l, d), BF16),
                   jax.ShapeDtypeStruct((b, N_EXPERTS, l), F32)],
        scratch_shapes=[pltpu.VMEM((l, d), F32),
                        pltpu.VMEM((rows * MLP_COL_PITCH, LANES), F32),
                        pltpu.VMEM((rows * MLP_COL_PITCH, LANES), F32),
                        pltpu.VMEM((l, kc), BF16)],
        compiler_params=_cparams(2, 48),
        name="mlp_mix",
    )(u, v, ws, bsb, w, x, mods, g, rw2)


def _pad_rows(a, rows):
    return jnp.pad(a, ((0, rows - a.shape[0]),) + ((0, 0),) * (a.ndim - 1))


def kernel(x, c, ctx, c_ctx, mod_w, mod_b, norm_g, ssd_in_w, ssd_conv_w, ssd_conv_b, ssd_dt_bias, ssd_a_log,
           ssd_d, ssd_norm_g, ssd_out_w, mlp_in_w, mlp_v_g, mlp_ws, mlp_bs, mlp_out_w, router_w, exp_w_gate,
           exp_w_up, exp_w_down):
    b, l, d = x.shape
    inner = SSD_HEADS * SSD_HEADDIM
    n_chunks = l // SSD_CHUNK
    n_ctx_chunks = ctx.shape[1] // SSD_CHUNK

    crows = _pad_rows(jnp.concatenate([c, c_ctx[None, :]], axis=0), -(-(b + 1) // SUBLANES) * SUBLANES)
    mod = _modulation(crows, mod_w, mod_b)
    mods_lat = [jnp.pad(mod[i, :b].reshape(b, N_MOD, d), ((0, 0), (0, SUBLANES - N_MOD), (0, 0)))
                for i in range(2)]
    mods_ctx = jnp.pad(mod[0, b].reshape(1, N_MOD, d), ((0, 0), (0, SUBLANES - N_MOD), (0, 0)))
    mods01 = jnp.stack([jnp.broadcast_to(mods_ctx, (b, SUBLANES, d)), mods_lat[0]], axis=1)
    gains = [_pad_rows(norm_g[i], SUBLANES) for i in range(2)]
    rw2 = []
    for i in range(2):
        rw_t = router_w[i].T
        rw_hi = rw_t.astype(BF16)
        rw2.append(jnp.concatenate([rw_hi, (rw_t - rw_hi.astype(F32)).astype(BF16)], axis=0))

    in_w = ssd_in_w[0]
    conv_dim = ssd_conv_w.shape[2]
    w_in = jnp.pad(in_w, ((0, 0), (0, LANES - 2 * SSD_HEADS))).astype(BF16)
    dt_bias = jnp.pad(ssd_dt_bias[0].reshape(1, 2 * SSD_HEADS), ((0, 0), (0, LANES - 2 * SSD_HEADS)))
    z, xs_all, bc_all, dt_all = _ssd_in_proj(
        ctx, x, mods01, gains[0], w_in, _pad_rows(ssd_conv_w[0], SUBLANES), ssd_conv_b[0].reshape(1, conv_dim),
        dt_bias)

    alog = _pad_rows(jnp.stack([jnp.pad(ssd_a_log[0, 0], (0, LANES - SSD_HEADS)),
                                jnp.pad(ssd_a_log[0, 1], (SSD_HEADS, LANES - 2 * SSD_HEADS))]), SUBLANES)
    head_of_col = jnp.arange(inner, dtype=jnp.int32) // SSD_HEADDIM
    rows128 = jnp.arange(LANES, dtype=jnp.int32)[:, None]
    expand = jnp.stack([(rows128 == head_of_col[None, :] + SSD_HEADS * dd) for dd in range(2)]).astype(BF16)
    dskip = jnp.repeat(ssd_d[0], SSD_HEADDIM).reshape(1, inner)
    y = _ssd_scan(xs_all, bc_all, dt_all, alog, expand, dskip, n_ctx_chunks, n_chunks)

    xa, h2, aff_t = _ssd_out(y, z, x, mods_lat[0], gains[0], ssd_norm_g[0].reshape(1, inner),
                             ssd_out_w[0].astype(BF16), rw2[0])
    x1 = _moe(h2, aff_t, exp_w_gate, exp_w_up, exp_w_down, 0, xa, mods_lat[0], gains[0])

    u, v = _mlp_in(x1, mods_lat[1], gains[1], mlp_in_w[0].astype(BF16), mlp_v_g[0].reshape(1, -1))
    bsb = jnp.broadcast_to(mlp_bs[0][:, :, None], (MLP_GROUPS, MLP_CHUNK, LANES))
    xb, h2b, aff_tb = _mlp_mix(u, v, mlp_ws[0].astype(BF16), bsb, mlp_out_w[0].astype(BF16), x1, mods_lat[1],
                               gains[1], rw2[1])
    return _moe(h2b, aff_tb, exp_w_gate, exp_w_up, exp_w_down, 1, xb, mods_lat[1], gains[1])
```
